```python
import jax, jax.numpy as jnp
from jax import lax
import numpy as np

D_MODEL = 1024
BATCH = 4
SEQ = 4096
DEPTH = 2

GM_GROUPS = 4
GM_DIM = 256
GM_CHUNK = 128
GM_WIDTH = GM_GROUPS * GM_DIM
RET_HEADS = 4
RET_QK_DIM = 128
RET_V_DIM = 256
RET_CHUNK = 128
RET_QK_WIDTH = RET_HEADS * RET_QK_DIM
RET_V_WIDTH = RET_HEADS * RET_V_DIM
HY_SPLITS = [GM_WIDTH, 2 * GM_WIDTH, 2 * GM_WIDTH + RET_QK_WIDTH,
             2 * GM_WIDTH + 2 * RET_QK_WIDTH, 2 * GM_WIDTH + 2 * RET_QK_WIDTH + RET_V_WIDTH]
HY_IN_WIDTH = 2 * GM_WIDTH + 2 * RET_QK_WIDTH + 2 * RET_V_WIDTH
HY_OUT_WIDTH = GM_WIDTH + RET_V_WIDTH
MLA_HEADS = 8
MLA_Q_RANK = 384
MLA_KV_RANK = 256
MLA_NOPE_DIM = 128
MLA_ROPE_DIM = 64
MLA_V_DIM = 128
MLA_QK_DIM = MLA_NOPE_DIM + MLA_ROPE_DIM
MLA_DOWN_WIDTH = MLA_Q_RANK + MLA_KV_RANK + MLA_ROPE_DIM
Q_BLOCK = 128
ROPE_THETA = 10000.0
PEER_HEADS = 8
PEER_N_KEYS = 128
PEER_N_EXPERTS = PEER_N_KEYS * PEER_N_KEYS
PEER_QUERY_DIM = 256
PEER_HALF = PEER_QUERY_DIM // 2
PEER_TOPK = 16
PEER_TOKEN_BLOCK = 128
NORM_EPS = 1e-6

kernel_name = "gmlp_retention_mla_peer_hybrid"


def rmsnorm(x, g):
    xf = x.astype(jnp.float32)
    y = xf * lax.rsqrt(jnp.mean(xf * xf, axis=-1, keepdims=True) + NORM_EPS)
    return (y * g.astype(jnp.float32)).astype(x.dtype)


def rope(x):
    s, d = x.shape[1], x.shape[-1]
    half = d // 2
    inv = 1.0 / (ROPE_THETA ** (jnp.arange(half, dtype=jnp.float32) / half))
    ang = jnp.arange(s, dtype=jnp.float32)[:, None] * inv[None, :]
    cos = jnp.cos(ang)[None, :, None, :]
    sin = jnp.sin(ang)[None, :, None, :]
    xf = x.astype(jnp.float32)
    x1, x2 = xf[..., :half], xf[..., half:]
    return jnp.concatenate([x1 * cos - x2 * sin, x1 * sin + x2 * cos], axis=-1).astype(x.dtype)


def chunk_gating_mlp(u, v, v_norm, w_s, b_s):
    b, s, _ = u.shape
    nc = s // GM_CHUNK
    v = rmsnorm(v.reshape(b, s, GM_GROUPS, GM_DIM), v_norm.reshape(GM_GROUPS, GM_DIM))
    v = v.reshape(b, nc, GM_CHUNK, GM_GROUPS, GM_DIM)
    causal = jnp.tril(jnp.ones((GM_CHUNK, GM_CHUNK), dtype=bool))
    w = jnp.where(causal[None], w_s, 0.0).astype(v.dtype)
    mixed = jnp.einsum('gts,bnsgd->bntgd', w, v) + jnp.transpose(b_s)[None, None, :, :, None]
    return u * mixed.reshape(b, s, GM_WIDTH)


def retention(q, k, v):
    b, s, h, dk = q.shape
    dv = v.shape[-1]
    c = RET_CHUNK
    nc = s // c
    log_gamma = jnp.log(1.0 - 2.0 ** (-5.0 - jnp.arange(h, dtype=jnp.float32)))
    pos = jnp.arange(c, dtype=jnp.float32)
    diff = pos[:, None] - pos[None, :]
    decay_mask = jnp.where(diff[None] >= 0, jnp.exp(diff[None] * log_gamma[:, None, None]), 0.0)
    q_decay = jnp.exp((pos[:, None] + 1.0) * log_gamma[None, :])
    k_decay = jnp.exp((c - 1.0 - pos[:, None]) * log_gamma[None, :])
    chunk_decay = jnp.exp(c * log_gamma)
    qf = q.astype(jnp.float32).reshape(b, nc, c, h, dk)
    kf = k.astype(jnp.float32).reshape(b, nc, c, h, dk) * (dk ** -0.5)
    vf = v.astype(jnp.float32).reshape(b, nc, c, h, dv)
    scores = jnp.einsum('bnihd,bnjhd->bnhij', qf, kf) * decay_mask[None, None]
    intra = jnp.einsum('bnhij,bnjhe->bnihe', scores, vf)

    def step(state, inp):
        qc, kc, vc = inp
        cross = jnp.einsum('bchd,bhde->bche', qc, state) * q_decay[None, :, :, None]
        state = state * chunk_decay[None, :, None, None] + jnp.einsum(
            'bchd,bche->bhde', kc, vc * k_decay[None, :, :, None])
        return state, cross

    init = jnp.zeros((b, h, dk, dv), jnp.float32)
    _, cross = lax.scan(step, init, (jnp.moveaxis(qf, 1, 0), jnp.moveaxis(kf, 1, 0), jnp.moveaxis(vf, 1, 0)))
    out = intra + jnp.moveaxis(cross, 0, 1)
    return out.reshape(b, s, h, dv).astype(v.dtype)


def gmlp_retention_mixer(h, w_in, gm_v_norm, gm_w_s, gm_b_s, ret_norm, w_out):
    b, s, _ = h.shape
    proj = h @ w_in
    a_u, a_v, r_q, r_k, r_v, r_g = jnp.split(proj, HY_SPLITS, axis=-1)
    y_a = chunk_gating_mlp(jax.nn.gelu(a_u, approximate=False), jax.nn.gelu(a_v, approximate=False),
                           gm_v_norm, gm_w_s, gm_b_s)
    q = rope(r_q.reshape(b, s, RET_HEADS, RET_QK_DIM))
    k = rope(r_k.reshape(b, s, RET_HEADS, RET_QK_DIM))
    y_b = retention(q, k, r_v.reshape(b, s, RET_HEADS, RET_V_DIM))
    y_b = rmsnorm(y_b, ret_norm.reshape(RET_HEADS, RET_V_DIM)).reshape(b, s, RET_V_WIDTH)
    y_b = jax.nn.silu(r_g) * y_b
    return jnp.concatenate([y_a, y_b], axis=-1) @ w_out


def causal_block_attention(q, k, v):
    b, s, h, d = q.shape
    dv = v.shape[-1]
    nb = s // Q_BLOCK
    scale = d ** -0.5
    qb = jnp.moveaxis(q.reshape(b, nb, Q_BLOCK, h, d), 1, 0)
    k_pos = jnp.arange(s)

    def one_block(inp):
        q_blk, i = inp
        scores = jnp.einsum('bqhd,bkhd->bhqk', q_blk, k).astype(jnp.float32) * scale
        q_pos = i * Q_BLOCK + jnp.arange(Q_BLOCK)
        mask = k_pos[None, :] <= q_pos[:, None]
        scores = jnp.where(mask[None, None], scores, -jnp.inf)
        p = jax.nn.softmax(scores, axis=-1).astype(v.dtype)
        return jnp.einsum('bhqk,bkhe->bqhe', p, v)

    out = lax.map(one_block, (qb, jnp.arange(nb)))
    return jnp.moveaxis(out, 0, 1).reshape(b, s, h, dv)


def mla(h, w_down, q_norm, w_q_up, kv_norm, w_kv_up, w_out):
    b, s, _ = h.shape
    down = h @ w_down
    c_q, c_kv, k_rope = jnp.split(down, [MLA_Q_RANK, MLA_Q_RANK + MLA_KV_RANK], axis=-1)
    q = (rmsnorm(c_q, q_norm) @ w_q_up).reshape(b, s, MLA_HEADS, MLA_QK_DIM)
    q = jnp.concatenate([q[..., :MLA_NOPE_DIM], rope(q[..., MLA_NOPE_DIM:])], axis=-1)
    kv = (rmsnorm(c_kv, kv_norm) @ w_kv_up).reshape(b, s, MLA_HEADS, MLA_NOPE_DIM + MLA_V_DIM)
    k_nope, v = kv[..., :MLA_NOPE_DIM], kv[..., MLA_NOPE_DIM:]
    k_rope = rope(k_rope[:, :, None, :])
    k = jnp.concatenate([k_nope, jnp.broadcast_to(k_rope, (b, s, MLA_HEADS, MLA_ROPE_DIM))], axis=-1)
    o = causal_block_attention(q, k, v)
    return o.reshape(b, s, MLA_HEADS * MLA_V_DIM) @ w_out


def peer(h, w_q, sub_keys, u_tab, v_tab):
    b, s, dm = h.shape
    t = b * s
    xt = h.reshape(t, dm)
    qry = (xt @ w_q).reshape(t, PEER_HEADS, 2, PEER_HALF)
    scores = jnp.einsum('thpd,pkd->thpk', qry, sub_keys).astype(jnp.float32)
    top_s, top_i = lax.top_k(scores, PEER_TOPK)
    cand = top_s[:, :, 0, :, None] + top_s[:, :, 1, None, :]
    best_s, best_c = lax.top_k(cand.reshape(t, PEER_HEADS, PEER_TOPK * PEER_TOPK), PEER_TOPK)
    i1 = jnp.take_along_axis(top_i[:, :, 0], best_c // PEER_TOPK, axis=-1)
    i2 = jnp.take_along_axis(top_i[:, :, 1], best_c % PEER_TOPK, axis=-1)
    experts = i1 * PEER_N_KEYS + i2
    gates = jax.nn.softmax(best_s, axis=-1).astype(h.dtype)
    nb = t // PEER_TOKEN_BLOCK

    def block(inp):
        xb, eb, gb = inp
        act = jax.nn.gelu(jnp.einsum('td,thkd->thk', xb, u_tab[eb]), approximate=False)
        return jnp.einsum('thk,thkd->td', gb * act, v_tab[eb])

    out = lax.map(block, (xt.reshape(nb, PEER_TOKEN_BLOCK, dm),
                          experts.reshape(nb, PEER_TOKEN_BLOCK, PEER_HEADS, PEER_TOPK),
                          gates.reshape(nb, PEER_TOKEN_BLOCK, PEER_HEADS, PEER_TOPK)))
    return out.reshape(b, s, dm)


def setup_inputs(seed: int = 0) -> dict:
    key = jax.random.key(seed)
    ks = jax.random.split(key, 24)
    n_even = (DEPTH + 1) // 2
    n_odd = DEPTH // 2
    f32 = jnp.float32

    def nrm(k, shape, scale):
        return jax.random.normal(k, shape, f32) * scale

    def gain(k, shape):
        return 1.0 + 0.01 * jax.random.normal(k, shape, f32)

    return {
        "x": jax.random.normal(ks[0], (BATCH, SEQ, D_MODEL), f32),
        "norm_mix": gain(ks[1], (DEPTH, D_MODEL)),
        "norm_ffn": gain(ks[2], (DEPTH, D_MODEL)),
        "hy_w_in": nrm(ks[3], (n_even, D_MODEL, HY_IN_WIDTH), D_MODEL ** -0.5),
        "gm_v_norm": gain(ks[4], (n_even, GM_WIDTH)),
        "gm_w_s": nrm(ks[5], (n_even, GM_GROUPS, GM_CHUNK, GM_CHUNK), GM_CHUNK ** -0.5),
        "gm_b_s": gain(ks[6], (n_even, GM_GROUPS, GM_CHUNK)),
        "ret_norm": gain(ks[7], (n_even, RET_V_WIDTH)),
        "hy_w_out": nrm(ks[8], (n_even, HY_OUT_WIDTH, D_MODEL), HY_OUT_WIDTH ** -0.5),
        "mla_w_down": nrm(ks[9], (n_odd, D_MODEL, MLA_DOWN_WIDTH), D_MODEL ** -0.5),
        "mla_q_norm": gain(ks[10], (n_odd, MLA_Q_RANK)),
        "mla_w_q_up": nrm(ks[11], (n_odd, MLA_Q_RANK, MLA_HEADS * MLA_QK_DIM), MLA_Q_RANK ** -0.5),
        "mla_kv_norm": gain(ks[12], (n_odd, MLA_KV_RANK)),
        "mla_w_kv_up": nrm(ks[13], (n_odd, MLA_KV_RANK, MLA_HEADS * (MLA_NOPE_DIM + MLA_V_DIM)), MLA_KV_RANK ** -0.5),
        "mla_w_out": nrm(ks[14], (n_odd, MLA_HEADS * MLA_V_DIM, D_MODEL), (MLA_HEADS * MLA_V_DIM) ** -0.5),
        "peer_w_q": nrm(ks[15], (DEPTH, D_MODEL, PEER_HEADS * PEER_QUERY_DIM), D_MODEL ** -0.5),
        "peer_sub_keys": nrm(ks[16], (DEPTH, 2, PEER_N_KEYS, PEER_HALF), PEER_HALF ** -0.5),
        "peer_u": nrm(ks[17], (DEPTH, PEER_N_EXPERTS, D_MODEL), D_MODEL ** -0.5),
        "peer_v": nrm(ks[18], (DEPTH, PEER_N_EXPERTS, D_MODEL), PEER_HEADS ** -0.5),
        "final_norm": gain(ks[19], (D_MODEL,)),
    }


def reference(x, norm_mix, norm_ffn, hy_w_in, gm_v_norm, gm_w_s, gm_b_s, ret_norm, hy_w_out,
              mla_w_down, mla_q_norm, mla_w_q_up, mla_kv_norm, mla_w_kv_up, mla_w_out,
              peer_w_q, peer_sub_keys, peer_u, peer_v, final_norm):
    for layer in range(DEPTH):
        j = layer // 2
        h = rmsnorm(x, norm_mix[layer])
        if layer % 2 == 0:
            x = x + gmlp_retention_mixer(h, hy_w_in[j], gm_v_norm[j], gm_w_s[j], gm_b_s[j],
                                         ret_norm[j], hy_w_out[j])
        else:
            x = x + mla(h, mla_w_down[j], mla_q_norm[j], mla_w_q_up[j], mla_kv_norm[j],
                        mla_w_kv_up[j], mla_w_out[j])
        h = rmsnorm(x, norm_ffn[layer])
        x = x + peer(h, peer_w_q[layer], peer_sub_keys[layer], peer_u[layer], peer_v[layer])
    return rmsnorm(x, final_norm)
```

```python
import functools
import math

import jax
import jax.numpy as jnp
from jax import lax
from jax.experimental import pallas as pl
from jax.experimental.pallas import tpu as pltpu

F32 = jnp.float32
BF16 = jnp.bfloat16

LANES = 128
V7X_VMEM_BYTES = 64 * 1024 * 1024
VMEM_LIMIT_BYTES = 56 * 1024 * 1024

NORM_EPS = 1e-6
ROPE_THETA = 10000.0

GM_GROUPS = 4
GM_DIM = 256
GM_CHUNK = 128
GM_WIDTH = GM_GROUPS * GM_DIM
RET_HEADS = 4
RET_QK_DIM = 128
RET_V_DIM = 256
RET_CHUNK = 128
RET_QK_WIDTH = RET_HEADS * RET_QK_DIM
RET_V_WIDTH = RET_HEADS * RET_V_DIM
HY_OUT_WIDTH = GM_WIDTH + RET_V_WIDTH
MLA_HEADS = 8
MLA_Q_RANK = 384
MLA_KV_RANK = 256
MLA_NOPE_DIM = 128
MLA_ROPE_DIM = 64
MLA_V_DIM = 128
MLA_QK_DIM = MLA_NOPE_DIM + MLA_ROPE_DIM
MLA_QK_PAD = 2 * LANES
PEER_HEADS = 8
PEER_N_KEYS = 128
PEER_HALF = 128
PEER_TOPK = 16

HY_TOKENS = 256
MLA_TOKENS = 512
ATT_BLOCK = 512
SEL_TOKENS = 256
PEER_TOKENS = 512
PEER_EXPERTS = 1024


def _cparams(*semantics):
    return pltpu.CompilerParams(dimension_semantics=semantics, vmem_limit_bytes=VMEM_LIMIT_BYTES)


def _const_spec(shape):
    return pl.BlockSpec(shape, lambda *_: (0,) * len(shape))


def _rms(x, g):
    return x * lax.rsqrt(jnp.mean(x * x, axis=-1, keepdims=True) + NORM_EPS) * g


def _gelu(x):
    return 0.5 * x * (1.0 + lax.erf(x * (2.0 ** -0.5)))


def _dot(a, b):
    return jnp.dot(a, b, preferred_element_type=F32)


def _dot_nt(a, b):
    return lax.dot_general(a, b, (((1,), (1,)), ((), ())), preferred_element_type=F32)


def _dot_tn(a, b):
    return lax.dot_general(a, b, (((0,), (0,)), ((), ())), preferred_element_type=F32)


def _hy_body(x_ref, gmix_ref, win_ref, vnorm_ref, ws_ref, bs_ref, cos_ref, sin_ref, dmask_ref, qdec_ref, kdec_ref,
             cdec_ref, rnorm_ref, wout_ref, o_ref, state_ref, y_ref):
    @pl.when(pl.program_id(1) == 0)
    def _():
        state_ref[...] = jnp.zeros_like(state_ref)

    x = x_ref[0]
    h = _rms(x, gmix_ref[...]).astype(BF16)
    tokens = x.shape[0]
    chunks = tokens // GM_CHUNK

    u_all = _gelu(_dot(h, win_ref[:, 0:GM_WIDTH]))
    v_all = _gelu(_dot(h, win_ref[:, GM_WIDTH:2 * GM_WIDTH]))
    row = lax.broadcasted_iota(jnp.int32, (GM_CHUNK, GM_CHUNK), 0)
    col = lax.broadcasted_iota(jnp.int32, (GM_CHUNK, GM_CHUNK), 1)
    causal = row >= col
    for g in range(GM_GROUPS):
        cols = slice(g * GM_DIM, (g + 1) * GM_DIM)
        vg = _rms(v_all[:, cols], vnorm_ref[:, cols]).astype(BF16)
        wg = jnp.where(causal, ws_ref[g], 0.0).astype(BF16)
        for c in range(chunks):
            rows = slice(c * GM_CHUNK, (c + 1) * GM_CHUNK)
            mixed = _dot(wg, vg[rows]) + bs_ref[g]
            y_ref[rows, cols] = (u_all[rows, cols] * mixed).astype(BF16)

    base = 2 * GM_WIDTH
    q_all = _dot(h, win_ref[:, base:base + RET_QK_WIDTH])
    k_all = _dot(h, win_ref[:, base + RET_QK_WIDTH:base + 2 * RET_QK_WIDTH])
    base += 2 * RET_QK_WIDTH
    v_all = _dot(h, win_ref[:, base:base + RET_V_WIDTH])
    g_all = _dot(h, win_ref[:, base + RET_V_WIDTH:base + 2 * RET_V_WIDTH])
    cos = cos_ref[...]
    sin = sin_ref[...]
    for hd in range(RET_HEADS):
        qk_cols = slice(hd * RET_QK_DIM, (hd + 1) * RET_QK_DIM)
        v_cols = slice(hd * RET_V_DIM, (hd + 1) * RET_V_DIM)
        q = q_all[:, qk_cols]
        k = k_all[:, qk_cols]
        q = q * cos + pltpu.roll(q, RET_QK_DIM // 2, 1) * sin
        k = (k * cos + pltpu.roll(k, RET_QK_DIM // 2, 1) * sin) * (RET_QK_DIM ** -0.5)
        for c in range(chunks):
            rows = slice(c * RET_CHUNK, (c + 1) * RET_CHUNK)
            qc, kc, vc = q[rows], k[rows], v_all[rows, v_cols]
            scores = _dot_nt(qc.astype(BF16), kc.astype(BF16)) * dmask_ref[hd]
            intra = _dot(scores.astype(BF16), vc.astype(BF16))
            state = state_ref[hd]
            cross = _dot((qc * qdec_ref[hd]).astype(BF16), state.astype(BF16))
            state_ref[hd] = state * cdec_ref[hd] + _dot_tn(kc.astype(BF16), (vc * kdec_ref[hd]).astype(BF16))
            gate = g_all[rows, v_cols]
            yb = _rms(intra + cross, rnorm_ref[:, v_cols]) * (gate * jax.nn.sigmoid(gate))
            y_ref[rows, GM_WIDTH + hd * RET_V_DIM:GM_WIDTH + (hd + 1) * RET_V_DIM] = yb.astype(BF16)

    o_ref[0] = x + _dot(y_ref[...], wout_ref[...])


def _rope_angles(seq, half):
    inv = 1.0 / (ROPE_THETA ** (jnp.arange(half, dtype=F32) / half))
    ang = jnp.arange(seq, dtype=F32)[:, None] * inv[None, :]
    return jnp.cos(ang), jnp.sin(ang)


def _hy_mixer(x, gmix, w_in, v_norm, w_s, b_s, ret_norm, w_out):
    b, s, d = x.shape
    ts = HY_TOKENS
    cos, sin = _rope_angles(s, RET_QK_DIM // 2)
    cos2 = jnp.concatenate([cos, cos], axis=1)
    sin2 = jnp.concatenate([-sin, sin], axis=1)
    log_gamma = jnp.log(1.0 - 2.0 ** (-5.0 - jnp.arange(RET_HEADS, dtype=F32)))
    pos = jnp.arange(RET_CHUNK, dtype=F32)
    diff = pos[:, None] - pos[None, :]
    dmask = jnp.where(diff[None] >= 0, jnp.exp(diff[None] * log_gamma[:, None, None]), 0.0)
    qdec = jnp.exp((pos[None, :] + 1.0) * log_gamma[:, None])[:, :, None]
    kdec = jnp.exp((RET_CHUNK - 1.0 - pos[None, :]) * log_gamma[:, None])[:, :, None]
    cdec = jnp.exp(RET_CHUNK * log_gamma)[:, None, None]
    in_width = w_in.shape[1]
    return pl.pallas_call(
        _hy_body,
        out_shape=jax.ShapeDtypeStruct((b, s, d), F32),
        grid=(b, s // ts),
        in_specs=[
            pl.BlockSpec((1, ts, d), lambda i, j: (i, j, 0)),
            _const_spec((1, d)),
            _const_spec((d, in_width)),
            _const_spec((1, GM_WIDTH)),
            _const_spec((GM_GROUPS, GM_CHUNK, GM_CHUNK)),
            _const_spec((GM_GROUPS, GM_CHUNK, 1)),
            pl.BlockSpec((ts, RET_QK_DIM), lambda i, j: (j, 0)),
            pl.BlockSpec((ts, RET_QK_DIM), lambda i, j: (j, 0)),
            _const_spec((RET_HEADS, RET_CHUNK, RET_CHUNK)),
            _const_spec((RET_HEADS, RET_CHUNK, 1)),
            _const_spec((RET_HEADS, RET_CHUNK, 1)),
            _const_spec((RET_HEADS, 1, 1)),
            _const_spec((1, RET_V_WIDTH)),
            _const_spec((HY_OUT_WIDTH, d)),
        ],
        out_specs=pl.BlockSpec((1, ts, d), lambda i, j: (i, j, 0)),
        scratch_shapes=[
            pltpu.VMEM((RET_HEADS, RET_QK_DIM, RET_V_DIM), F32),
            pltpu.VMEM((ts, HY_OUT_WIDTH), BF16),
        ],
        compiler_params=_cparams("parallel", "arbitrary"),
        name="hy_mixer",
    )(x, gmix[None, :], w_in.astype(BF16), v_norm[None, :], w_s, b_s[:, :, None], cos2, sin2, dmask, qdec, kdec, cdec,
      ret_norm[None, :], w_out.astype(BF16))


def _mla_proj_body(x_ref, gmix_ref, wdown_ref, qn_ref, kvn_ref, wq_ref, wqrot_ref, wk_ref, wv_ref, cq_ref, sq_ref,
                   ck_ref, sk_ref, q_ref, k_ref, v_ref):
    h = _rms(x_ref[0], gmix_ref[...]).astype(BF16)
    down = _dot(h, wdown_ref[...])
    kv_end = MLA_Q_RANK + MLA_KV_RANK
    cq = _rms(down[:, :MLA_Q_RANK], qn_ref[...]).astype(BF16)
    ckv = _rms(down[:, MLA_Q_RANK:kv_end], kvn_ref[...]).astype(BF16)
    k_rope = (down[:, kv_end:kv_end + LANES] * ck_ref[...] + down[:, kv_end + LANES:kv_end + 2 * LANES] * sk_ref[...])
    k_rope = k_rope.astype(BF16)
    qa = _dot(cq, wq_ref[...])
    qb = _dot(cq, wqrot_ref[...])
    kn = _dot(ckv, wk_ref[...])
    scale = MLA_QK_DIM ** -0.5
    cq_tab = cq_ref[...] * scale
    sq_tab = sq_ref[...] * scale
    for hd in range(MLA_HEADS):
        seg = slice(hd * MLA_QK_PAD, (hd + 1) * MLA_QK_PAD)
        q_ref[0, :, seg] = (qa[:, seg] * cq_tab + qb[:, seg] * sq_tab).astype(BF16)
        k_ref[0, :, hd * MLA_QK_PAD:hd * MLA_QK_PAD + MLA_NOPE_DIM] = (
            kn[:, hd * MLA_NOPE_DIM:(hd + 1) * MLA_NOPE_DIM].astype(BF16))
        k_ref[0, :, hd * MLA_QK_PAD + MLA_NOPE_DIM:(hd + 1) * MLA_QK_PAD] = k_rope
    v_ref[0] = _dot(ckv, wv_ref[...]).astype(BF16)


def _rot_half_cols(w):
    half = w.shape[-1] // 2
    return jnp.concatenate([-w[..., half:], w[..., :half]], axis=-1)


def _mla_proj(x, gmix, w_down, q_norm, w_q_up, kv_norm, w_kv_up):
    b, s, d = x.shape
    tm = MLA_TOKENS
    kv_end = MLA_Q_RANK + MLA_KV_RANK
    pad = MLA_QK_PAD - MLA_QK_DIM
    w_kr = w_down[:, kv_end:]
    zk = jnp.zeros((d, LANES - MLA_ROPE_DIM), F32)
    w_down_x = jnp.concatenate([w_down[:, :kv_end], w_kr, zk, _rot_half_cols(w_kr), zk], axis=1).astype(BF16)
    wq = w_q_up.reshape(MLA_Q_RANK, MLA_HEADS, MLA_QK_DIM)
    zq = jnp.zeros((MLA_Q_RANK, MLA_HEADS, pad), F32)
    wq_a = jnp.concatenate([wq, zq], axis=2).reshape(MLA_Q_RANK, MLA_HEADS * MLA_QK_PAD).astype(BF16)
    wq_b = jnp.concatenate([jnp.zeros((MLA_Q_RANK, MLA_HEADS, MLA_NOPE_DIM), F32),
                            _rot_half_cols(wq[:, :, MLA_NOPE_DIM:]), zq], axis=2)
    wq_b = wq_b.reshape(MLA_Q_RANK, MLA_HEADS * MLA_QK_PAD).astype(BF16)
    wkv = w_kv_up.reshape(MLA_KV_RANK, MLA_HEADS, MLA_NOPE_DIM + MLA_V_DIM)
    wk = wkv[:, :, :MLA_NOPE_DIM].reshape(MLA_KV_RANK, MLA_HEADS * MLA_NOPE_DIM).astype(BF16)
    wv = wkv[:, :, MLA_NOPE_DIM:].reshape(MLA_KV_RANK, MLA_HEADS * MLA_V_DIM).astype(BF16)
    cos, sin = _rope_angles(s, MLA_ROPE_DIM // 2)
    zt = jnp.zeros((s, pad), F32)
    cq_tab = jnp.concatenate([jnp.ones((s, MLA_NOPE_DIM), F32), cos, cos, zt], axis=1)
    sq_tab = jnp.concatenate([jnp.zeros((s, MLA_NOPE_DIM), F32), sin, sin, zt], axis=1)
    ck_tab = jnp.concatenate([cos, cos, zt], axis=1)
    sk_tab = jnp.concatenate([sin, sin, zt], axis=1)
    qk_width = MLA_HEADS * MLA_QK_PAD
    v_width = MLA_HEADS * MLA_V_DIM
    tab = lambda w: pl.BlockSpec((tm, w), lambda i, j: (j, 0))
    out = lambda w: pl.BlockSpec((1, tm, w), lambda i, j: (i, j, 0))
    return pl.pallas_call(
        _mla_proj_body,
        out_shape=(jax.ShapeDtypeStruct((b, s, qk_width), BF16), jax.ShapeDtypeStruct((b, s, qk_width), BF16),
                   jax.ShapeDtypeStruct((b, s, v_width), BF16)),
        grid=(b, s // tm),
        in_specs=[
            out(d),
            _const_spec((1, d)),
            _const_spec(w_down_x.shape),
            _const_spec((1, MLA_Q_RANK)),
            _const_spec((1, MLA_KV_RANK)),
            _const_spec(wq_a.shape),
            _const_spec(wq_b.shape),
            _const_spec(wk.shape),
            _const_spec(wv.shape),
            tab(MLA_QK_PAD), tab(MLA_QK_PAD), tab(LANES), tab(LANES),
        ],
        out_specs=(out(qk_width), out(qk_width), out(v_width)),
        compiler_params=_cparams("parallel", "parallel"),
        name="mla_proj",
    )(x, gmix[None, :], w_down_x, q_norm[None, :], kv_norm[None, :], wq_a, wq_b, wk, wv, cq_tab, sq_tab, ck_tab, sk_tab)


def _flash_body(q_ref, k_ref, v_ref, o_ref):
    blk = q_ref.shape[1]
    i = pl.program_id(2)
    q = q_ref[0]

    def step(j, carry, masked):
        m, l, acc = carry
        start = pl.multiple_of(j * blk, blk)
        s = _dot_nt(q, k_ref[0, pl.ds(start, blk), :])
        if masked:
            row = lax.broadcasted_iota(jnp.int32, s.shape, 0)
            col = lax.broadcasted_iota(jnp.int32, s.shape, 1)
            s = jnp.where(row >= col, s, -jnp.inf)
        m_new = jnp.maximum(m, jnp.max(s, axis=1, keepdims=True))
        p = jnp.exp(s - m_new)
        alpha = jnp.exp(m - m_new)
        l = alpha * l + jnp.sum(p, axis=1, keepdims=True)
        acc = alpha * acc + _dot(p.astype(BF16), v_ref[0, pl.ds(start, blk), :])
        return m_new, l, acc

    init = (jnp.full((blk, 1), -jnp.inf, F32), jnp.zeros((blk, 1), F32), jnp.zeros((blk, v_ref.shape[2]), F32))
    carry = lax.fori_loop(0, i, lambda j, c: step(j, c, False), init)
    _, l, acc = step(i, carry, True)
    o_ref[0] = (acc / l).astype(BF16)


def _flash(q, k, v):
    b, s, _ = q.shape
    blk = ATT_BLOCK
    return pl.pallas_call(
        _flash_body,
        out_shape=jax.ShapeDtypeStruct((b, s, MLA_HEADS * MLA_V_DIM), BF16),
        grid=(b, MLA_HEADS, s // blk),
        in_specs=[
            pl.BlockSpec((1, blk, MLA_QK_PAD), lambda bi, h, i: (bi, i, h)),
            pl.BlockSpec((1, s, MLA_QK_PAD), lambda bi, h, i: (bi, 0, h)),
            pl.BlockSpec((1, s, MLA_V_DIM), lambda bi, h, i: (bi, 0, h)),
        ],
        out_specs=pl.BlockSpec((1, blk, MLA_V_DIM), lambda bi, h, i: (bi, i, h)),
        compiler_params=_cparams("parallel", "parallel", "arbitrary"),
        name="mla_flash",
    )(q, k, v)


def _out_proj_body(x_ref, o_ref, w_ref, y_ref):
    y_ref[...] = x_ref[...] + _dot(o_ref[...], w_ref[...])


def _out_proj(x2, o2, w_out):
    t, d = x2.shape
    tm = MLA_TOKENS
    return pl.pallas_call(
        _out_proj_body,
        out_shape=jax.ShapeDtypeStruct((t, d), F32),
        grid=(t // tm,),
        in_specs=[pl.BlockSpec((tm, d), lambda i: (i, 0)), pl.BlockSpec((tm, o2.shape[1]), lambda i: (i, 0)),
                  _const_spec(w_out.shape)],
        out_specs=pl.BlockSpec((tm, d), lambda i: (i, 0)),
        compiler_params=_cparams("parallel"),
        name="mla_out",
    )(x2, o2, w_out.astype(BF16))


def _top_ranks(scores, sorted_ref):
    keys = lax.broadcasted_iota(jnp.int32, scores.shape, 0).astype(F32)

    def body(a, carry):
        s, rank = carry
        m = jnp.max(s, axis=0, keepdims=True)
        first = jnp.min(jnp.where(s == m, keys, float(PEER_N_KEYS)), axis=0, keepdims=True)
        hit = keys == first
        sorted_ref[pl.ds(a, 1), :] = m
        return jnp.where(hit, -jnp.inf, s), jnp.where(hit, a.astype(F32), rank)

    _, rank = lax.fori_loop(0, PEER_TOPK, body, (scores, jnp.full(scores.shape, float(PEER_N_KEYS), F32)))
    return rank


def _peer_sel_body(x_ref, g_ref, wq_ref, keys_ref, hb_ref, e1_ref, n_ref, e2_ref, rb_ref, sc_ref, s1_ref, s2_ref):
    hb = _rms(x_ref[...], g_ref[...]).astype(BF16)
    hb_ref[...] = hb
    qry = _dot(hb, wq_ref[...]).astype(BF16)
    for hp in range(2 * PEER_HEADS):
        sc_ref[hp] = _dot_nt(keys_ref[hp % 2], qry[:, hp * PEER_HALF:(hp + 1) * PEER_HALF])

    tokens = x_ref.shape[0]
    slot = lax.broadcasted_iota(jnp.int32, (PEER_TOPK, tokens), 0).astype(F32)

    def head(h, _):
        sc1 = sc_ref[2 * h]
        sc2 = sc_ref[2 * h + 1]
        ra = _top_ranks(sc1, s1_ref)
        rb = _top_ranks(sc2, s2_ref)
        s1 = s1_ref[...]
        s2 = s2_ref[...]

        def merge(_, carry):
            n, front = carry
            m = jnp.max(front, axis=0, keepdims=True)
            first = jnp.min(jnp.where(front == m, slot, float(PEER_TOPK)), axis=0, keepdims=True)
            hit = slot == first
            n = n + jnp.where(hit, 1.0, 0.0)
            n_hit = jnp.max(jnp.where(hit, n, 0.0), axis=0, keepdims=True)
            nxt = jnp.full(n_hit.shape, -jnp.inf, F32)
            for b2 in range(1, PEER_TOPK):
                nxt = jnp.where(n_hit == float(b2), s2[b2:b2 + 1, :], nxt)
            return n, jnp.where(hit, s1 + nxt, front)

        n, _ = lax.fori_loop(0, PEER_TOPK, merge, (jnp.zeros((PEER_TOPK, tokens), F32), s1 + s2[0:1, :]))

        e1s = jnp.exp(s1 - s1[0:1, :])
        e2s = jnp.exp(s2 - s2[0:1, :])
        z = jnp.zeros((1, tokens), F32)
        for b2 in range(PEER_TOPK):
            z = z + jnp.sum(jnp.where(n > float(b2), e1s * e2s[b2:b2 + 1, :], 0.0), axis=0, keepdims=True)
        n_key = jnp.zeros(ra.shape, F32)
        for a in range(PEER_TOPK):
            n_key = jnp.where(ra == float(a), n[a:a + 1, :], n_key)
        e1_ref[h] = jnp.exp(sc1 - s1[0:1, :]) / z
        n_ref[h] = n_key
        e2_ref[h] = jnp.exp(sc2 - s2[0:1, :])
        rb_ref[h] = rb
        return 0

    lax.fori_loop(0, PEER_HEADS, head, 0)


def _peer_select(x2, gain, w_q, sub_keys):
    t, d = x2.shape
    tb = SEL_TOKENS
    sel_shape = jax.ShapeDtypeStruct((PEER_HEADS, PEER_N_KEYS, t), F32)
    sel_spec = pl.BlockSpec((PEER_HEADS, PEER_N_KEYS, tb), lambda i: (0, 0, i))
    return pl.pallas_call(
        _peer_sel_body,
        out_shape=(jax.ShapeDtypeStruct((t, d), BF16), sel_shape, sel_shape, sel_shape, sel_shape),
        grid=(t // tb,),
        in_specs=[pl.BlockSpec((tb, d), lambda i: (i, 0)), _const_spec((1, d)), _const_spec(w_q.shape),
                  _const_spec(sub_keys.shape)],
        out_specs=(pl.BlockSpec((tb, d), lambda i: (i, 0)), sel_spec, sel_spec, sel_spec, sel_spec),
        scratch_shapes=[pltpu.VMEM((2 * PEER_HEADS, PEER_N_KEYS, tb), F32), pltpu.VMEM((PEER_TOPK, tb), F32),
                        pltpu.VMEM((PEER_TOPK, tb), F32)],
        compiler_params=_cparams("parallel"),
        name="peer_select",
    )(x2, gain[None, :], w_q.astype(BF16), sub_keys.astype(BF16))


def _peer_main_body(final_norm, x_ref, hb_ref, u_ref, vt_ref, e1_ref, n_ref, e2_ref, rb_ref, fg_ref, o_ref, acc_ref,
                    a_ref, g_ref):
    j = pl.program_id(1)

    @pl.when(j == 0)
    def _():
        acc_ref[...] = jnp.zeros_like(acc_ref)

    experts, tokens = a_ref.shape
    a_ref[...] = _dot_nt(u_ref[...], hb_ref[...])
    key_blocks = experts // PEER_N_KEYS
    for cc in range(key_blocks):
        rows = slice(cc * PEER_N_KEYS, (cc + 1) * PEER_N_KEYS)
        for tb in range(tokens // LANES):
            cols = slice(tb * LANES, (tb + 1) * LANES)
            w = jnp.zeros((PEER_N_KEYS, LANES), F32)
            for h in range(PEER_HEADS):
                n_row = n_ref[h, cc:cc + 1, cols]
                e1_row = e1_ref[h, cc:cc + 1, cols]
                w = w + jnp.where(rb_ref[h, :, cols] < n_row, e2_ref[h, :, cols] * e1_row, 0.0)
            g_ref[rows, cols] = (w * _gelu(a_ref[rows, cols])).astype(BF16)
    acc_ref[...] += _dot(vt_ref[...], g_ref[...])

    @pl.when(j == pl.num_programs(1) - 1)
    def _():
        y = x_ref[...] + acc_ref[...].T
        if final_norm:
            y = _rms(y, fg_ref[...])
        o_ref[...] = y


def _peer_experts(x2, hb, e1, n, e2, rb, u_tab, v_tab, final_gain=None):
    t, d = x2.shape
    n_experts = u_tab.shape[0]
    tb, eb = PEER_TOKENS, PEER_EXPERTS
    final_norm = final_gain is not None
    fg = (final_gain if final_norm else jnp.ones((d,), F32))[None, :]
    sel_spec = pl.BlockSpec((PEER_HEADS, PEER_N_KEYS, tb), lambda i, j: (0, 0, i))
    key_spec = pl.BlockSpec((PEER_HEADS, eb // PEER_N_KEYS, tb), lambda i, j: (0, j, i))
    return pl.pallas_call(
        functools.partial(_peer_main_body, final_norm),
        out_shape=jax.ShapeDtypeStruct((t, d), F32),
        grid=(t // tb, n_experts // eb),
        in_specs=[
            pl.BlockSpec((tb, d), lambda i, j: (i, 0)),
            pl.BlockSpec((tb, d), lambda i, j: (i, 0)),
            pl.BlockSpec((eb, d), lambda i, j: (j, 0)),
            pl.BlockSpec((d, eb), lambda i, j: (0, j)),
            key_spec, key_spec, sel_spec, sel_spec,
            _const_spec((1, d)),
        ],
        out_specs=pl.BlockSpec((tb, d), lambda i, j: (i, 0)),
        scratch_shapes=[pltpu.VMEM((d, tb), F32), pltpu.VMEM((eb, tb), F32), pltpu.VMEM((eb, tb), BF16)],
        compiler_params=_cparams("parallel", "arbitrary"),
        name="peer_experts",
    )(x2, hb, u_tab.astype(BF16), v_tab.astype(BF16).T, e1, n, e2, rb, fg)


def _peer(x2, gain, w_q, sub_keys, u_tab, v_tab, final_gain=None):
    hb, e1, n, e2, rb = _peer_select(x2, gain, w_q, sub_keys)
    return _peer_experts(x2, hb, e1, n, e2, rb, u_tab, v_tab, final_gain)


def kernel(x, norm_mix, norm_ffn, hy_w_in, gm_v_norm, gm_w_s, gm_b_s, ret_norm, hy_w_out, mla_w_down, mla_q_norm,
           mla_w_q_up, mla_kv_norm, mla_w_kv_up, mla_w_out, peer_w_q, peer_sub_keys, peer_u, peer_v, final_norm):
    b, s, d = x.shape
    depth = norm_mix.shape[0]
    for layer in range(depth):
        j = layer // 2
        if layer % 2 == 0:
            x = _hy_mixer(x, norm_mix[layer], hy_w_in[j], gm_v_norm[j], gm_w_s[j], gm_b_s[j], ret_norm[j], hy_w_out[j])
            x2 = x.reshape(b * s, d)
        else:
            q, k, v = _mla_proj(x, norm_mix[layer], mla_w_down[j], mla_q_norm[j], mla_w_q_up[j], mla_kv_norm[j],
                                mla_w_kv_up[j])
            o = _flash(q, k, v)
            x2 = _out_proj(x.reshape(b * s, d), o.reshape(b * s, -1), mla_w_out[j])
        last = layer == depth - 1
        x2 = _peer(x2, norm_ffn[layer], peer_w_q[layer], peer_sub_keys[layer], peer_u[layer], peer_v[layer],
                   final_norm if last else None)
        x = x2.reshape(b, s, d)
    return x
```

```python
import functools
import math

import jax
import jax.numpy as jnp
from jax import lax
from jax.experimental import pallas as pl
from jax.experimental.pallas import tpu as pltpu

F32 = jnp.float32
BF16 = jnp.bfloat16

LANES = 128
BF16_SUBLANES = 16
MXU_WIDTH = 256
V7X_VMEM_BYTES = 64 * 1024 * 1024
VMEM_LIMIT_BYTES = 56 * 1024 * 1024

NORM_EPS = 1e-6
ROPE_THETA = 10000.0

GM_GROUPS = 4
GM_DIM = 256
GM_CHUNK = 128
GM_WIDTH = GM_GROUPS * GM_DIM
RET_HEADS = 4
RET_QK_DIM = 128
RET_V_DIM = 256
RET_CHUNK = 128
RET_QK_WIDTH = RET_HEADS * RET_QK_DIM
RET_V_WIDTH = RET_HEADS * RET_V_DIM
HY_OUT_WIDTH = GM_WIDTH + RET_V_WIDTH
MLA_HEADS = 8
MLA_Q_RANK = 384
MLA_KV_RANK = 256
MLA_NOPE_DIM = 128
MLA_ROPE_DIM = 64
MLA_V_DIM = 128
MLA_QK_DIM = MLA_NOPE_DIM + MLA_ROPE_DIM
MLA_QK_PAD = 2 * LANES
PEER_HEADS = 8
PEER_N_KEYS = 128
PEER_HALF = 128
PEER_TOPK = 16

HY_TOKENS = 256
MLA_TOKENS = 512
ATT_BLOCK = 512
SEL_TOKENS = 256
PEER_TOKENS = 1024
PEER_EXPERTS = 512
PEER_KEY_GROUP = PEER_EXPERTS // PEER_N_KEYS


def _cparams(*semantics):
    return pltpu.CompilerParams(dimension_semantics=semantics, vmem_limit_bytes=VMEM_LIMIT_BYTES)


def _const_spec(shape):
    return pl.BlockSpec(shape, lambda *_: (0,) * len(shape))


def _rms(x, g):
    return x * lax.rsqrt(jnp.mean(x * x, axis=-1, keepdims=True) + NORM_EPS) * g


def _gelu(x):
    return 0.5 * x * (1.0 + lax.erf(x * (2.0 ** -0.5)))


def _dot(a, b):
    return jnp.dot(a, b, preferred_element_type=F32)


def _dot_nt(a, b):
    return lax.dot_general(a, b, (((1,), (1,)), ((), ())), preferred_element_type=F32)


def _dot_tn(a, b):
    return lax.dot_general(a, b, (((0,), (0,)), ((), ())), preferred_element_type=F32)


def _hy_body(x_ref, gmix_ref, win_ref, vnorm_ref, ws_ref, bs_ref, cos_ref, sin_ref, dmask_ref, qdec_ref, kdec_ref,
             cdec_ref, rnorm_ref, wout_ref, o_ref, state_ref, y_ref):
    @pl.when(pl.program_id(1) == 0)
    def _():
        state_ref[...] = jnp.zeros_like(state_ref)

    x = x_ref[0]
    h = _rms(x, gmix_ref[...]).astype(BF16)
    tokens = x.shape[0]
    chunks = tokens // GM_CHUNK

    u_all = _gelu(_dot(h, win_ref[:, 0:GM_WIDTH]))
    v_all = _gelu(_dot(h, win_ref[:, GM_WIDTH:2 * GM_WIDTH]))
    row = lax.broadcasted_iota(jnp.int32, (GM_CHUNK, GM_CHUNK), 0)
    col = lax.broadcasted_iota(jnp.int32, (GM_CHUNK, GM_CHUNK), 1)
    causal = row >= col
    for g in range(GM_GROUPS):
        cols = slice(g * GM_DIM, (g + 1) * GM_DIM)
        vg = _rms(v_all[:, cols], vnorm_ref[:, cols]).astype(BF16)
        wg = jnp.where(causal, ws_ref[g], 0.0).astype(BF16)
        for c in range(chunks):
            rows = slice(c * GM_CHUNK, (c + 1) * GM_CHUNK)
            mixed = _dot(wg, vg[rows]) + bs_ref[g]
            y_ref[rows, cols] = (u_all[rows, cols] * mixed).astype(BF16)

    base = 2 * GM_WIDTH
    q_all = _dot(h, win_ref[:, base:base + RET_QK_WIDTH])
    k_all = _dot(h, win_ref[:, base + RET_QK_WIDTH:base + 2 * RET_QK_WIDTH])
    base += 2 * RET_QK_WIDTH
    v_all = _dot(h, win_ref[:, base:base + RET_V_WIDTH])
    g_all = _dot(h, win_ref[:, base + RET_V_WIDTH:base + 2 * RET_V_WIDTH])
    cos = cos_ref[...]
    sin = sin_ref[...]
    for hd in range(RET_HEADS):
        qk_cols = slice(hd * RET_QK_DIM, (hd + 1) * RET_QK_DIM)
        v_cols = slice(hd * RET_V_DIM, (hd + 1) * RET_V_DIM)
        q = q_all[:, qk_cols]
        k = k_all[:, qk_cols]
        q = q * cos + pltpu.roll(q, RET_QK_DIM // 2, 1) * sin
        k = (k * cos + pltpu.roll(k, RET_QK_DIM // 2, 1) * sin) * (RET_QK_DIM ** -0.5)
        for c in range(chunks):
            rows = slice(c * RET_CHUNK, (c + 1) * RET_CHUNK)
            qc, kc, vc = q[rows], k[rows], v_all[rows, v_cols]
            scores = _dot_nt(qc.astype(BF16), kc.astype(BF16)) * dmask_ref[hd]
            intra = _dot(scores.astype(BF16), vc.astype(BF16))
            state = state_ref[hd]
            cross = _dot((qc * qdec_ref[hd]).astype(BF16), state.astype(BF16))
            state_ref[hd] = state * cdec_ref[hd] + _dot_tn(kc.astype(BF16), (vc * kdec_ref[hd]).astype(BF16))
            gate = g_all[rows, v_cols]
            yb = _rms(intra + cross, rnorm_ref[:, v_cols]) * (gate * jax.nn.sigmoid(gate))
            y_ref[rows, GM_WIDTH + hd * RET_V_DIM:GM_WIDTH + (hd + 1) * RET_V_DIM] = yb.astype(BF16)

    o_ref[0] = x + _dot(y_ref[...], wout_ref[...])


def _rope_angles(seq, half):
    inv = 1.0 / (ROPE_THETA ** (jnp.arange(half, dtype=F32) / half))
    ang = jnp.arange(seq, dtype=F32)[:, None] * inv[None, :]
    return jnp.cos(ang), jnp.sin(ang)


def _hy_mixer(x, gmix, w_in, v_norm, w_s, b_s, ret_norm, w_out):
    b, s, d = x.shape
    ts = HY_TOKENS
    cos, sin = _rope_angles(s, RET_QK_DIM // 2)
    cos2 = jnp.concatenate([cos, cos], axis=1)
    sin2 = jnp.concatenate([-sin, sin], axis=1)
    log_gamma = jnp.log(1.0 - 2.0 ** (-5.0 - jnp.arange(RET_HEADS, dtype=F32)))
    pos = jnp.arange(RET_CHUNK, dtype=F32)
    diff = pos[:, None] - pos[None, :]
    dmask = jnp.where(diff[None] >= 0, jnp.exp(diff[None] * log_gamma[:, None, None]), 0.0)
    qdec = jnp.exp((pos[None, :] + 1.0) * log_gamma[:, None])[:, :, None]
    kdec = jnp.exp((RET_CHUNK - 1.0 - pos[None, :]) * log_gamma[:, None])[:, :, None]
    cdec = jnp.exp(RET_CHUNK * log_gamma)[:, None, None]
    in_width = w_in.shape[1]
    return pl.pallas_call(
        _hy_body,
        out_shape=jax.ShapeDtypeStruct((b, s, d), F32),
        grid=(b, s // ts),
        in_specs=[
            pl.BlockSpec((1, ts, d), lambda i, j: (i, j, 0)),
            _const_spec((1, d)),
            _const_spec((d, in_width)),
            _const_spec((1, GM_WIDTH)),
            _const_spec((GM_GROUPS, GM_CHUNK, GM_CHUNK)),
            _const_spec((GM_GROUPS, GM_CHUNK, 1)),
            pl.BlockSpec((ts, RET_QK_DIM), lambda i, j: (j, 0)),
            pl.BlockSpec((ts, RET_QK_DIM), lambda i, j: (j, 0)),
            _const_spec((RET_HEADS, RET_CHUNK, RET_CHUNK)),
            _const_spec((RET_HEADS, RET_CHUNK, 1)),
            _const_spec((RET_HEADS, RET_CHUNK, 1)),
            _const_spec((RET_HEADS, 1, 1)),
            _const_spec((1, RET_V_WIDTH)),
            _const_spec((HY_OUT_WIDTH, d)),
        ],
        out_specs=pl.BlockSpec((1, ts, d), lambda i, j: (i, j, 0)),
        scratch_shapes=[
            pltpu.VMEM((RET_HEADS, RET_QK_DIM, RET_V_DIM), F32),
            pltpu.VMEM((ts, HY_OUT_WIDTH), BF16),
        ],
        compiler_params=_cparams("parallel", "arbitrary"),
        name="hy_mixer",
    )(x, gmix[None, :], w_in.astype(BF16), v_norm[None, :], w_s, b_s[:, :, None], cos2, sin2, dmask, qdec, kdec, cdec,
      ret_norm[None, :], w_out.astype(BF16))


def _mla_proj_body(x_ref, gmix_ref, wdown_ref, qn_ref, kvn_ref, wq_ref, wqrot_ref, wk_ref, wv_ref, cq_ref, sq_ref,
                   ck_ref, sk_ref, q_ref, k_ref, v_ref):
    h = _rms(x_ref[0], gmix_ref[...]).astype(BF16)
    down = _dot(h, wdown_ref[...])
    kv_end = MLA_Q_RANK + MLA_KV_RANK
    cq = _rms(down[:, :MLA_Q_RANK], qn_ref[...]).astype(BF16)
    ckv = _rms(down[:, MLA_Q_RANK:kv_end], kvn_ref[...]).astype(BF16)
    k_rope = (down[:, kv_end:kv_end + LANES] * ck_ref[...] + down[:, kv_end + LANES:kv_end + 2 * LANES] * sk_ref[...])
    k_rope = k_rope.astype(BF16)
    qa = _dot(cq, wq_ref[...])
    qb = _dot(cq, wqrot_ref[...])
    kn = _dot(ckv, wk_ref[...])
    scale = MLA_QK_DIM ** -0.5
    cq_tab = cq_ref[...] * scale
    sq_tab = sq_ref[...] * scale
    for hd in range(MLA_HEADS):
        seg = slice(hd * MLA_QK_PAD, (hd + 1) * MLA_QK_PAD)
        q_ref[0, :, seg] = (qa[:, seg] * cq_tab + qb[:, seg] * sq_tab).astype(BF16)
        k_ref[0, :, hd * MLA_QK_PAD:hd * MLA_QK_PAD + MLA_NOPE_DIM] = (
            kn[:, hd * MLA_NOPE_DIM:(hd + 1) * MLA_NOPE_DIM].astype(BF16))
        k_ref[0, :, hd * MLA_QK_PAD + MLA_NOPE_DIM:(hd + 1) * MLA_QK_PAD] = k_rope
    v_ref[0] = _dot(ckv, wv_ref[...]).astype(BF16)


def _rot_half_cols(w):
    half = w.shape[-1] // 2
    return jnp.concatenate([-w[..., half:], w[..., :half]], axis=-1)


def _mla_proj(x, gmix, w_down, q_norm, w_q_up, kv_norm, w_kv_up):
    b, s, d = x.shape
    tm = MLA_TOKENS
    kv_end = MLA_Q_RANK + MLA_KV_RANK
    pad = MLA_QK_PAD - MLA_QK_DIM
    w_kr = w_down[:, kv_end:]
    zk = jnp.zeros((d, LANES - MLA_ROPE_DIM), F32)
    w_down_x = jnp.concatenate([w_down[:, :kv_end], w_kr, zk, _rot_half_cols(w_kr), zk], axis=1).astype(BF16)
    wq = w_q_up.reshape(MLA_Q_RANK, MLA_HEADS, MLA_QK_DIM)
    zq = jnp.zeros((MLA_Q_RANK, MLA_HEADS, pad), F32)
    wq_a = jnp.concatenate([wq, zq], axis=2).reshape(MLA_Q_RANK, MLA_HEADS * MLA_QK_PAD).astype(BF16)
    wq_b = jnp.concatenate([jnp.zeros((MLA_Q_RANK, MLA_HEADS, MLA_NOPE_DIM), F32),
                            _rot_half_cols(wq[:, :, MLA_NOPE_DIM:]), zq], axis=2)
    wq_b = wq_b.reshape(MLA_Q_RANK, MLA_HEADS * MLA_QK_PAD).astype(BF16)
    wkv = w_kv_up.reshape(MLA_KV_RANK, MLA_HEADS, MLA_NOPE_DIM + MLA_V_DIM)
    wk = wkv[:, :, :MLA_NOPE_DIM].reshape(MLA_KV_RANK, MLA_HEADS * MLA_NOPE_DIM).astype(BF16)
    wv = wkv[:, :, MLA_NOPE_DIM:].reshape(MLA_KV_RANK, MLA_HEADS * MLA_V_DIM).astype(BF16)
    cos, sin = _rope_angles(s, MLA_ROPE_DIM // 2)
    zt = jnp.zeros((s, pad), F32)
    cq_tab = jnp.concatenate([jnp.ones((s, MLA_NOPE_DIM), F32), cos, cos, zt], axis=1)
    sq_tab = jnp.concatenate([jnp.zeros((s, MLA_NOPE_DIM), F32), sin, sin, zt], axis=1)
    ck_tab = jnp.concatenate([cos, cos, zt], axis=1)
    sk_tab = jnp.concatenate([sin, sin, zt], axis=1)
    qk_width = MLA_HEADS * MLA_QK_PAD
    v_width = MLA_HEADS * MLA_V_DIM
    tab = lambda w: pl.BlockSpec((tm, w), lambda i, j: (j, 0))
    out = lambda w: pl.BlockSpec((1, tm, w), lambda i, j: (i, j, 0))
    return pl.pallas_call(
        _mla_proj_body,
        out_shape=(jax.ShapeDtypeStruct((b, s, qk_width), BF16), jax.ShapeDtypeStruct((b, s, qk_width), BF16),
                   jax.ShapeDtypeStruct((b, s, v_width), BF16)),
        grid=(b, s // tm),
        in_specs=[
            out(d),
            _const_spec((1, d)),
            _const_spec(w_down_x.shape),
            _const_spec((1, MLA_Q_RANK)),
            _const_spec((1, MLA_KV_RANK)),
            _const_spec(wq_a.shape),
            _const_spec(wq_b.shape),
            _const_spec(wk.shape),
            _const_spec(wv.shape),
            tab(MLA_QK_PAD), tab(MLA_QK_PAD), tab(LANES), tab(LANES),
        ],
        out_specs=(out(qk_width), out(qk_width), out(v_width)),
        compiler_params=_cparams("parallel", "parallel"),
        name="mla_proj",
    )(x, gmix[None, :], w_down_x, q_norm[None, :], kv_norm[None, :], wq_a, wq_b, wk, wv, cq_tab, sq_tab, ck_tab, sk_tab)


def _flash_body(q_ref, k_ref, v_ref, o_ref):
    blk = q_ref.shape[1]
    i = pl.program_id(2)
    q = q_ref[0]

    def step(j, carry, masked):
        m, l, acc = carry
        start = pl.multiple_of(j * blk, blk)
        s = _dot_nt(q, k_ref[0, pl.ds(start, blk), :])
        if masked:
            row = lax.broadcasted_iota(jnp.int32, s.shape, 0)
            col = lax.broadcasted_iota(jnp.int32, s.shape, 1)
            s = jnp.where(row >= col, s, -jnp.inf)
        m_new = jnp.maximum(m, jnp.max(s, axis=1, keepdims=True))
        p = jnp.exp(s - m_new)
        alpha = jnp.exp(m - m_new)
        l = alpha * l + jnp.sum(p, axis=1, keepdims=True)
        acc = alpha * acc + _dot(p.astype(BF16), v_ref[0, pl.ds(start, blk), :])
        return m_new, l, acc

    init = (jnp.full((blk, 1), -jnp.inf, F32), jnp.zeros((blk, 1), F32), jnp.zeros((blk, v_ref.shape[2]), F32))
    carry = lax.fori_loop(0, i, lambda j, c: step(j, c, False), init)
    _, l, acc = step(i, carry, True)
    o_ref[0] = (acc / l).astype(BF16)


def _flash(q, k, v):
    b, s, _ = q.shape
    blk = ATT_BLOCK
    return pl.pallas_call(
        _flash_body,
        out_shape=jax.ShapeDtypeStruct((b, s, MLA_HEADS * MLA_V_DIM), BF16),
        grid=(b, MLA_HEADS, s // blk),
        in_specs=[
            pl.BlockSpec((1, blk, MLA_QK_PAD), lambda bi, h, i: (bi, i, h)),
            pl.BlockSpec((1, s, MLA_QK_PAD), lambda bi, h, i: (bi, 0, h)),
            pl.BlockSpec((1, s, MLA_V_DIM), lambda bi, h, i: (bi, 0, h)),
        ],
        out_specs=pl.BlockSpec((1, blk, MLA_V_DIM), lambda bi, h, i: (bi, i, h)),
        compiler_params=_cparams("parallel", "parallel", "arbitrary"),
        name="mla_flash",
    )(q, k, v)


def _out_proj_body(x_ref, o_ref, w_ref, y_ref):
    y_ref[...] = x_ref[...] + _dot(o_ref[...], w_ref[...])


def _out_proj(x2, o2, w_out):
    t, d = x2.shape
    tm = MLA_TOKENS
    return pl.pallas_call(
        _out_proj_body,
        out_shape=jax.ShapeDtypeStruct((t, d), F32),
        grid=(t // tm,),
        in_specs=[pl.BlockSpec((tm, d), lambda i: (i, 0)), pl.BlockSpec((tm, o2.shape[1]), lambda i: (i, 0)),
                  _const_spec(w_out.shape)],
        out_specs=pl.BlockSpec((tm, d), lambda i: (i, 0)),
        compiler_params=_cparams("parallel"),
        name="mla_out",
    )(x2, o2, w_out.astype(BF16))


def _top_ranks(scores, sorted_ref):
    keys = lax.broadcasted_iota(jnp.int32, scores.shape, 0).astype(F32)

    def body(a, carry):
        s, rank = carry
        m = jnp.max(s, axis=0, keepdims=True)
        first = jnp.min(jnp.where(s == m, keys, float(PEER_N_KEYS)), axis=0, keepdims=True)
        hit = keys == first
        sorted_ref[pl.ds(a, 1), :] = m
        return jnp.where(hit, -jnp.inf, s), jnp.where(hit, a.astype(F32), rank)

    _, rank = lax.fori_loop(0, PEER_TOPK, body, (scores, jnp.full(scores.shape, float(PEER_N_KEYS), F32)))
    return rank


def _peer_sel_body(x_ref, g_ref, wq_ref, keys_ref, hb_ref, e1_ref, n_ref, e2_ref, rb_ref, sc_ref, s1_ref, s2_ref):
    hb = _rms(x_ref[...], g_ref[...]).astype(BF16)
    hb_ref[...] = hb
    qry = _dot(hb, wq_ref[...]).astype(BF16)
    for hp in range(2 * PEER_HEADS):
        sc_ref[hp] = _dot_nt(keys_ref[hp % 2], qry[:, hp * PEER_HALF:(hp + 1) * PEER_HALF])

    tokens = x_ref.shape[0]
    slot = lax.broadcasted_iota(jnp.int32, (PEER_TOPK, tokens), 0).astype(F32)

    def head(h, _):
        sc1 = sc_ref[2 * h]
        sc2 = sc_ref[2 * h + 1]
        ra = _top_ranks(sc1, s1_ref)
        rb = _top_ranks(sc2, s2_ref)
        s1 = s1_ref[...]
        s2 = s2_ref[...]

        def merge(_, carry):
            n, front = carry
            m = jnp.max(front, axis=0, keepdims=True)
            first = jnp.min(jnp.where(front == m, slot, float(PEER_TOPK)), axis=0, keepdims=True)
            hit = slot == first
            n = n + jnp.where(hit, 1.0, 0.0)
            n_hit = jnp.max(jnp.where(hit, n, 0.0), axis=0, keepdims=True)
            nxt = jnp.full(n_hit.shape, -jnp.inf, F32)
            for b2 in range(1, PEER_TOPK):
                nxt = jnp.where(n_hit == float(b2), s2[b2:b2 + 1, :], nxt)
            return n, jnp.where(hit, s1 + nxt, front)

        n, _ = lax.fori_loop(0, PEER_TOPK, merge, (jnp.zeros((PEER_TOPK, tokens), F32), s1 + s2[0:1, :]))

        e1s = jnp.exp(s1 - s1[0:1, :])
        e2s = jnp.exp(s2 - s2[0:1, :])
        z = jnp.zeros((1, tokens), F32)
        for b2 in range(PEER_TOPK):
            z = z + jnp.sum(jnp.where(n > float(b2), e1s * e2s[b2:b2 + 1, :], 0.0), axis=0, keepdims=True)
        n_key = jnp.zeros(ra.shape, F32)
        for a in range(PEER_TOPK):
            n_key = jnp.where(ra == float(a), n[a:a + 1, :], n_key)
        e1_ref[h] = jnp.exp(sc1 - s1[0:1, :]) / z
        n_ref[h] = n_key
        e2p = pltpu.bitcast(jnp.exp(sc2 - s2[0:1, :]).astype(BF16), F32)
        rbp = pltpu.bitcast(rb.astype(BF16), F32)
        for tt in range(tokens // LANES):
            e2_ref[h, tt] = e2p[:, tt * LANES:(tt + 1) * LANES]
            rb_ref[h, tt] = rbp[:, tt * LANES:(tt + 1) * LANES]
        return 0

    lax.fori_loop(0, PEER_HEADS, head, 0)


def _peer_select(x2, gain, w_q, sub_keys):
    t, d = x2.shape
    tb = SEL_TOKENS
    sel_shape = jax.ShapeDtypeStruct((PEER_HEADS, PEER_N_KEYS, t), F32)
    sel_spec = pl.BlockSpec((PEER_HEADS, PEER_N_KEYS, tb), lambda i: (0, 0, i))
    pair_shape = jax.ShapeDtypeStruct((PEER_HEADS, t // LANES, PEER_N_KEYS // 2, LANES), F32)
    pair_spec = pl.BlockSpec((PEER_HEADS, tb // LANES, PEER_N_KEYS // 2, LANES), lambda i: (0, i, 0, 0))
    return pl.pallas_call(
        _peer_sel_body,
        out_shape=(jax.ShapeDtypeStruct((t, d), BF16), sel_shape, sel_shape, pair_shape, pair_shape),
        grid=(t // tb,),
        in_specs=[pl.BlockSpec((tb, d), lambda i: (i, 0)), _const_spec((1, d)), _const_spec(w_q.shape),
                  _const_spec(sub_keys.shape)],
        out_specs=(pl.BlockSpec((tb, d), lambda i: (i, 0)), sel_spec, sel_spec, pair_spec, pair_spec),
        scratch_shapes=[pltpu.VMEM((2 * PEER_HEADS, PEER_N_KEYS, tb), F32), pltpu.VMEM((PEER_TOPK, tb), F32),
                        pltpu.VMEM((PEER_TOPK, tb), F32)],
        compiler_params=_cparams("parallel"),
        name="peer_select",
    )(x2, gain[None, :], w_q.astype(BF16), sub_keys.astype(BF16))


def _key_rows_bf16(row):
    packed = jnp.broadcast_to(row, (BF16_SUBLANES, row.shape[1])).astype(BF16)
    return jnp.concatenate([packed] * (PEER_N_KEYS // BF16_SUBLANES), axis=0)


def _peer_stages(hb_ref, u_ref, vt_ref, e1_ref, n_ref, e2_ref, rb_ref, acc_ref, a_new, a_cur, g_new, g_cur):
    token_tiles = a_cur.shape[0]
    packed_rows = PEER_N_KEYS // 2
    for tb in range(token_tiles):
        p, sub = tb // 2, (tb % 2) * LANES
        if tb % 2 == 0:
            a_piece = _dot_nt(u_ref[...], hb_ref[p * MXU_WIDTH:(p + 1) * MXU_WIDTH, :])
            a_new[tb] = a_piece[:, :LANES]
            a_new[tb + 1] = a_piece[:, LANES:]
        else:
            acc_ref[p] += _dot(vt_ref[...], pltpu.bitcast(g_cur[p], BF16))
        cols = slice(tb * LANES, (tb + 1) * LANES)
        for k in range(PEER_KEY_GROUP):
            w = None
            for h in range(PEER_HEADS):
                rb_t = pltpu.bitcast(rb_ref[h, tb], BF16)
                e2_t = pltpu.bitcast(e2_ref[h, tb], BF16)
                n_b = _key_rows_bf16(n_ref[h, 0, k:k + 1, cols])
                e1_b = _key_rows_bf16(e1_ref[h, 0, k:k + 1, cols])
                term = jnp.where(rb_t < n_b, e2_t * e1_b, 0.0)
                w = term if h == 0 else w + term
            act = _gelu(a_cur[tb, k * PEER_N_KEYS:(k + 1) * PEER_N_KEYS, :]).astype(BF16)
            g_new[p, k * packed_rows:(k + 1) * packed_rows, sub:sub + LANES] = pltpu.bitcast(w * act, F32)


def _peer_main_body(final_norm, x_ref, hb_ref, u_ref, vt_ref, e1_ref, n_ref, e2_ref, rb_ref, fg_ref, o_ref, acc_ref,
                    a0_ref, a1_ref, g0_ref, g1_ref):
    s = pl.program_id(1)

    @pl.when(s == 0)
    def _():
        acc_ref[...] = jnp.zeros_like(acc_ref)
        a1_ref[...] = jnp.zeros_like(a1_ref)
        g0_ref[...] = jnp.zeros_like(g0_ref)

    stages = functools.partial(_peer_stages, hb_ref, u_ref, vt_ref, e1_ref, n_ref, e2_ref, rb_ref, acc_ref)

    @pl.when(s % 2 == 0)
    def _():
        stages(a0_ref, a1_ref, g1_ref, g0_ref)

    @pl.when(s % 2 == 1)
    def _():
        stages(a1_ref, a0_ref, g0_ref, g1_ref)

    @pl.when(s == pl.num_programs(1) - 1)
    def _():
        for p in range(acc_ref.shape[0]):
            rows = slice(p * MXU_WIDTH, (p + 1) * MXU_WIDTH)
            y = x_ref[rows, :] + acc_ref[p].T
            if final_norm:
                y = _rms(y, fg_ref[...])
            o_ref[rows, :] = y


def _peer_experts(x2, hb, e1, n, e2, rb, u_tab, v_tab, final_gain=None):
    t, d = x2.shape
    n_experts = u_tab.shape[0]
    tb, eb = PEER_TOKENS, PEER_EXPERTS
    chunks = n_experts // eb
    final_norm = final_gain is not None
    fg = (final_gain if final_norm else jnp.ones((d,), F32))[None, :]
    once = dict(pipeline_mode=pl.Buffered(1))
    tok_spec = pl.BlockSpec((tb, d), lambda i, s: (i, 0), **once)
    sel_spec = pl.BlockSpec((PEER_HEADS, tb // LANES, PEER_N_KEYS // 2, LANES), lambda i, s: (0, i, 0, 0), **once)
    key_spec = pl.BlockSpec((PEER_HEADS, 1, PEER_KEY_GROUP, tb),
                            lambda i, s: (0, jnp.clip(s - 1, 0, chunks - 1), 0, i))
    key_shape = (PEER_HEADS, chunks, PEER_KEY_GROUP, t)
    return pl.pallas_call(
        functools.partial(_peer_main_body, final_norm),
        out_shape=jax.ShapeDtypeStruct((t, d), F32),
        grid=(t // tb, chunks + 2),
        in_specs=[
            tok_spec,
            tok_spec,
            pl.BlockSpec((eb, d), lambda i, s: (jnp.minimum(s, chunks - 1), 0)),
            pl.BlockSpec((d, eb), lambda i, s: (0, jnp.maximum(s - 2, 0))),
            key_spec, key_spec, sel_spec, sel_spec,
            _const_spec((1, d)),
        ],
        out_specs=pl.BlockSpec((tb, d), lambda i, s: (i, 0)),
        scratch_shapes=[pltpu.VMEM((tb // MXU_WIDTH, d, MXU_WIDTH), F32),
                        pltpu.VMEM((tb // LANES, eb, LANES), F32), pltpu.VMEM((tb // LANES, eb, LANES), F32),
                        pltpu.VMEM((tb // MXU_WIDTH, eb // 2, MXU_WIDTH), F32),
                        pltpu.VMEM((tb // MXU_WIDTH, eb // 2, MXU_WIDTH), F32)],
        compiler_params=_cparams("parallel", "arbitrary"),
        name="peer_experts",
    )(x2, hb, u_tab.astype(BF16), v_tab.astype(BF16).T, e1.reshape(key_shape), n.reshape(key_shape), e2, rb, fg)


def _peer(x2, gain, w_q, sub_keys, u_tab, v_tab, final_gain=None):
    hb, e1, n, e2, rb = _peer_select(x2, gain, w_q, sub_keys)
    return _peer_experts(x2, hb, e1, n, e2, rb, u_tab, v_tab, final_gain)


def kernel(x, norm_mix, norm_ffn, hy_w_in, gm_v_norm, gm_w_s, gm_b_s, ret_norm, hy_w_out, mla_w_down, mla_q_norm,
           mla_w_q_up, mla_kv_norm, mla_w_kv_up, mla_w_out, peer_w_q, peer_sub_keys, peer_u, peer_v, final_norm):
    b, s, d = x.shape
    depth = norm_mix.shape[0]
    for layer in range(depth):
        j = layer // 2
        if layer % 2 == 0:
            x = _hy_mixer(x, norm_mix[layer], hy_w_in[j], gm_v_norm[j], gm_w_s[j], gm_b_s[j], ret_norm[j], hy_w_out[j])
            x2 = x.reshape(b * s, d)
        else:
            q, k, v = _mla_proj(x, norm_mix[layer], mla_w_down[j], mla_q_norm[j], mla_w_q_up[j], mla_kv_norm[j],
                                mla_w_kv_up[j])
            o = _flash(q, k, v)
            x2 = _out_proj(x.reshape(b * s, d), o.reshape(b * s, -1), mla_w_out[j])
        last = layer == depth - 1
        x2 = _peer(x2, norm_ffn[layer], peer_w_q[layer], peer_sub_keys[layer], peer_u[layer], peer_v[layer],
                   final_norm if last else None)
        x = x2.reshape(b, s, d)
    return x
```

```python
import functools
import math

import jax
import jax.numpy as jnp
from jax import lax
from jax.experimental import pallas as pl
from jax.experimental.pallas import tpu as pltpu

F32 = jnp.float32
BF16 = jnp.bfloat16

LANES = 128
BF16_SUBLANES = 16
MXU_WIDTH = 256
V7X_VMEM_BYTES = 64 * 1024 * 1024
VMEM_LIMIT_BYTES = 56 * 1024 * 1024

NORM_EPS = 1e-6
ROPE_THETA = 10000.0

GM_GROUPS = 4
GM_DIM = 256
GM_CHUNK = 128
GM_WIDTH = GM_GROUPS * GM_DIM
RET_HEADS = 4
RET_QK_DIM = 128
RET_V_DIM = 256
RET_CHUNK = 128
RET_QK_WIDTH = RET_HEADS * RET_QK_DIM
RET_V_WIDTH = RET_HEADS * RET_V_DIM
HY_OUT_WIDTH = GM_WIDTH + RET_V_WIDTH
MLA_HEADS = 8
MLA_Q_RANK = 384
MLA_KV_RANK = 256
MLA_NOPE_DIM = 128
MLA_ROPE_DIM = 64
MLA_V_DIM = 128
MLA_QK_DIM = MLA_NOPE_DIM + MLA_ROPE_DIM
MLA_QK_PAD = 2 * LANES
PEER_HEADS = 8
PEER_N_KEYS = 128
PEER_HALF = 128
PEER_TOPK = 16

HY_TOKENS = 256
MLA_TOKENS = 512
ATT_BLOCK = 512
SEL_TOKENS = 256
PEER_TOKENS = 1024
PEER_EXPERTS = 512
PEER_KEY_GROUP = PEER_EXPERTS // PEER_N_KEYS


def _cparams(*semantics):
    return pltpu.CompilerParams(dimension_semantics=semantics, vmem_limit_bytes=VMEM_LIMIT_BYTES)


def _const_spec(shape):
    return pl.BlockSpec(shape, lambda *_: (0,) * len(shape))


def _rms(x, g):
    return x * lax.rsqrt(jnp.mean(x * x, axis=-1, keepdims=True) + NORM_EPS) * g


def _gelu(x):
    return 0.5 * x * (1.0 + lax.erf(x * (2.0 ** -0.5)))


def _dot(a, b):
    return jnp.dot(a, b, preferred_element_type=F32)


def _dot_nt(a, b):
    return lax.dot_general(a, b, (((1,), (1,)), ((), ())), preferred_element_type=F32)


def _dot_tn(a, b):
    return lax.dot_general(a, b, (((0,), (0,)), ((), ())), preferred_element_type=F32)


def _hy_body(x_ref, gmix_ref, win_ref, vnorm_ref, ws_ref, bs_ref, cos_ref, sin_ref, dmask_ref, qdec_ref, kdec_ref,
             cdec_ref, rnorm_ref, wout_ref, o_ref, state_ref, y_ref):
    @pl.when(pl.program_id(1) == 0)
    def _():
        state_ref[...] = jnp.zeros_like(state_ref)

    x = x_ref[0]
    h = _rms(x, gmix_ref[...]).astype(BF16)
    tokens = x.shape[0]
    chunks = tokens // GM_CHUNK

    u_all = _gelu(_dot(h, win_ref[:, 0:GM_WIDTH]))
    v_all = _gelu(_dot(h, win_ref[:, GM_WIDTH:2 * GM_WIDTH]))
    row = lax.broadcasted_iota(jnp.int32, (GM_CHUNK, GM_CHUNK), 0)
    col = lax.broadcasted_iota(jnp.int32, (GM_CHUNK, GM_CHUNK), 1)
    causal = row >= col
    for g in range(GM_GROUPS):
        cols = slice(g * GM_DIM, (g + 1) * GM_DIM)
        vg = _rms(v_all[:, cols], vnorm_ref[:, cols]).astype(BF16)
        wg = jnp.where(causal, ws_ref[g], 0.0).astype(BF16)
        for c in range(chunks):
            rows = slice(c * GM_CHUNK, (c + 1) * GM_CHUNK)
            mixed = _dot(wg, vg[rows]) + bs_ref[g]
            y_ref[rows, cols] = (u_all[rows, cols] * mixed).astype(BF16)

    base = 2 * GM_WIDTH
    q_all = _dot(h, win_ref[:, base:base + RET_QK_WIDTH])
    k_all = _dot(h, win_ref[:, base + RET_QK_WIDTH:base + 2 * RET_QK_WIDTH])
    base += 2 * RET_QK_WIDTH
    v_all = _dot(h, win_ref[:, base:base + RET_V_WIDTH])
    g_all = _dot(h, win_ref[:, base + RET_V_WIDTH:base + 2 * RET_V_WIDTH])
    cos = cos_ref[...]
    sin = sin_ref[...]
    for hd in range(RET_HEADS):
        qk_cols = slice(hd * RET_QK_DIM, (hd + 1) * RET_QK_DIM)
        v_cols = slice(hd * RET_V_DIM, (hd + 1) * RET_V_DIM)
        q = q_all[:, qk_cols]
        k = k_all[:, qk_cols]
        q = q * cos + pltpu.roll(q, RET_QK_DIM // 2, 1) * sin
        k = (k * cos + pltpu.roll(k, RET_QK_DIM // 2, 1) * sin) * (RET_QK_DIM ** -0.5)
        for c in range(chunks):
            rows = slice(c * RET_CHUNK, (c + 1) * RET_CHUNK)
            qc, kc, vc = q[rows], k[rows], v_all[rows, v_cols]
            scores = _dot_nt(qc.astype(BF16), kc.astype(BF16)) * dmask_ref[hd]
            intra = _dot(scores.astype(BF16), vc.astype(BF16))
            state = state_ref[hd]
            cross = _dot((qc * qdec_ref[hd]).astype(BF16), state.astype(BF16))
            state_ref[hd] = state * cdec_ref[hd] + _dot_tn(kc.astype(BF16), (vc * kdec_ref[hd]).astype(BF16))
            gate = g_all[rows, v_cols]
            yb = _rms(intra + cross, rnorm_ref[:, v_cols]) * (gate * jax.nn.sigmoid(gate))
            y_ref[rows, GM_WIDTH + hd * RET_V_DIM:GM_WIDTH + (hd + 1) * RET_V_DIM] = yb.astype(BF16)

    o_ref[0] = x + _dot(y_ref[...], wout_ref[...])


def _rope_angles(seq, half):
    inv = 1.0 / (ROPE_THETA ** (jnp.arange(half, dtype=F32) / half))
    ang = jnp.arange(seq, dtype=F32)[:, None] * inv[None, :]
    return jnp.cos(ang), jnp.sin(ang)


def _hy_mixer(x, gmix, w_in, v_norm, w_s, b_s, ret_norm, w_out):
    b, s, d = x.shape
    ts = HY_TOKENS
    cos, sin = _rope_angles(s, RET_QK_DIM // 2)
    cos2 = jnp.concatenate([cos, cos], axis=1)
    sin2 = jnp.concatenate([-sin, sin], axis=1)
    log_gamma = jnp.log(1.0 - 2.0 ** (-5.0 - jnp.arange(RET_HEADS, dtype=F32)))
    pos = jnp.arange(RET_CHUNK, dtype=F32)
    diff = pos[:, None] - pos[None, :]
    dmask = jnp.where(diff[None] >= 0, jnp.exp(diff[None] * log_gamma[:, None, None]), 0.0)
    qdec = jnp.exp((pos[None, :] + 1.0) * log_gamma[:, None])[:, :, None]
    kdec = jnp.exp((RET_CHUNK - 1.0 - pos[None, :]) * log_gamma[:, None])[:, :, None]
    cdec = jnp.exp(RET_CHUNK * log_gamma)[:, None, None]
    in_width = w_in.shape[1]
    return pl.pallas_call(
        _hy_body,
        out_shape=jax.ShapeDtypeStruct((b, s, d), F32),
        grid=(b, s // ts),
        in_specs=[
            pl.BlockSpec((1, ts, d), lambda i, j: (i, j, 0)),
            _const_spec((1, d)),
            _const_spec((d, in_width)),
            _const_spec((1, GM_WIDTH)),
            _const_spec((GM_GROUPS, GM_CHUNK, GM_CHUNK)),
            _const_spec((GM_GROUPS, GM_CHUNK, 1)),
            pl.BlockSpec((ts, RET_QK_DIM), lambda i, j: (j, 0)),
            pl.BlockSpec((ts, RET_QK_DIM), lambda i, j: (j, 0)),
            _const_spec((RET_HEADS, RET_CHUNK, RET_CHUNK)),
            _const_spec((RET_HEADS, RET_CHUNK, 1)),
            _const_spec((RET_HEADS, RET_CHUNK, 1)),
            _const_spec((RET_HEADS, 1, 1)),
            _const_spec((1, RET_V_WIDTH)),
            _const_spec((HY_OUT_WIDTH, d)),
        ],
        out_specs=pl.BlockSpec((1, ts, d), lambda i, j: (i, j, 0)),
        scratch_shapes=[
            pltpu.VMEM((RET_HEADS, RET_QK_DIM, RET_V_DIM), F32),
            pltpu.VMEM((ts, HY_OUT_WIDTH), BF16),
        ],
        compiler_params=_cparams("parallel", "arbitrary"),
        name="hy_mixer",
    )(x, gmix[None, :], w_in.astype(BF16), v_norm[None, :], w_s, b_s[:, :, None], cos2, sin2, dmask, qdec, kdec, cdec,
      ret_norm[None, :], w_out.astype(BF16))


def _mla_proj_body(x_ref, gmix_ref, wdown_ref, qn_ref, kvn_ref, wq_ref, wqrot_ref, wk_ref, wv_ref, cq_ref, sq_ref,
                   ck_ref, sk_ref, q_ref, k_ref, v_ref):
    h = _rms(x_ref[0], gmix_ref[...]).astype(BF16)
    down = _dot(h, wdown_ref[...])
    kv_end = MLA_Q_RANK + MLA_KV_RANK
    cq = _rms(down[:, :MLA_Q_RANK], qn_ref[...]).astype(BF16)
    ckv = _rms(down[:, MLA_Q_RANK:kv_end], kvn_ref[...]).astype(BF16)
    k_rope = (down[:, kv_end:kv_end + LANES] * ck_ref[...] + down[:, kv_end + LANES:kv_end + 2 * LANES] * sk_ref[...])
    k_rope = k_rope.astype(BF16)
    qa = _dot(cq, wq_ref[...])
    qb = _dot(cq, wqrot_ref[...])
    kn = _dot(ckv, wk_ref[...])
    scale = MLA_QK_DIM ** -0.5
    cq_tab = cq_ref[...] * scale
    sq_tab = sq_ref[...] * scale
    for hd in range(MLA_HEADS):
        seg = slice(hd * MLA_QK_PAD, (hd + 1) * MLA_QK_PAD)
        q_ref[0, :, seg] = (qa[:, seg] * cq_tab + qb[:, seg] * sq_tab).astype(BF16)
        k_ref[0, :, hd * MLA_QK_PAD:hd * MLA_QK_PAD + MLA_NOPE_DIM] = (
            kn[:, hd * MLA_NOPE_DIM:(hd + 1) * MLA_NOPE_DIM].astype(BF16))
        k_ref[0, :, hd * MLA_QK_PAD + MLA_NOPE_DIM:(hd + 1) * MLA_QK_PAD] = k_rope
    v_ref[0] = _dot(ckv, wv_ref[...]).astype(BF16)


def _rot_half_cols(w):
    half = w.shape[-1] // 2
    return jnp.concatenate([-w[..., half:], w[..., :half]], axis=-1)


def _mla_proj(x, gmix, w_down, q_norm, w_q_up, kv_norm, w_kv_up):
    b, s, d = x.shape
    tm = MLA_TOKENS
    kv_end = MLA_Q_RANK + MLA_KV_RANK
    pad = MLA_QK_PAD - MLA_QK_DIM
    w_kr = w_down[:, kv_end:]
    zk = jnp.zeros((d, LANES - MLA_ROPE_DIM), F32)
    w_down_x = jnp.concatenate([w_down[:, :kv_end], w_kr, zk, _rot_half_cols(w_kr), zk], axis=1).astype(BF16)
    wq = w_q_up.reshape(MLA_Q_RANK, MLA_HEADS, MLA_QK_DIM)
    zq = jnp.zeros((MLA_Q_RANK, MLA_HEADS, pad), F32)
    wq_a = jnp.concatenate([wq, zq], axis=2).reshape(MLA_Q_RANK, MLA_HEADS * MLA_QK_PAD).astype(BF16)
    wq_b = jnp.concatenate([jnp.zeros((MLA_Q_RANK, MLA_HEADS, MLA_NOPE_DIM), F32),
                            _rot_half_cols(wq[:, :, MLA_NOPE_DIM:]), zq], axis=2)
    wq_b = wq_b.reshape(MLA_Q_RANK, MLA_HEADS * MLA_QK_PAD).astype(BF16)
    wkv = w_kv_up.reshape(MLA_KV_RANK, MLA_HEADS, MLA_NOPE_DIM + MLA_V_DIM)
    wk = wkv[:, :, :MLA_NOPE_DIM].reshape(MLA_KV_RANK, MLA_HEADS * MLA_NOPE_DIM).astype(BF16)
    wv = wkv[:, :, MLA_NOPE_DIM:].reshape(MLA_KV_RANK, MLA_HEADS * MLA_V_DIM).astype(BF16)
    cos, sin = _rope_angles(s, MLA_ROPE_DIM // 2)
    zt = jnp.zeros((s, pad), F32)
    cq_tab = jnp.concatenate([jnp.ones((s, MLA_NOPE_DIM), F32), cos, cos, zt], axis=1)
    sq_tab = jnp.concatenate([jnp.zeros((s, MLA_NOPE_DIM), F32), sin, sin, zt], axis=1)
    ck_tab = jnp.concatenate([cos, cos, zt], axis=1)
    sk_tab = jnp.concatenate([sin, sin, zt], axis=1)
    qk_width = MLA_HEADS * MLA_QK_PAD
    v_width = MLA_HEADS * MLA_V_DIM
    tab = lambda w: pl.BlockSpec((tm, w), lambda i, j: (j, 0))
    out = lambda w: pl.BlockSpec((1, tm, w), lambda i, j: (i, j, 0))
    return pl.pallas_call(
        _mla_proj_body,
        out_shape=(jax.ShapeDtypeStruct((b, s, qk_width), BF16), jax.ShapeDtypeStruct((b, s, qk_width), BF16),
                   jax.ShapeDtypeStruct((b, s, v_width), BF16)),
        grid=(b, s // tm),
        in_specs=[
            out(d),
            _const_spec((1, d)),
            _const_spec(w_down_x.shape),
            _const_spec((1, MLA_Q_RANK)),
            _const_spec((1, MLA_KV_RANK)),
            _const_spec(wq_a.shape),
            _const_spec(wq_b.shape),
            _const_spec(wk.shape),
            _const_spec(wv.shape),
            tab(MLA_QK_PAD), tab(MLA_QK_PAD), tab(LANES), tab(LANES),
        ],
        out_specs=(out(qk_width), out(qk_width), out(v_width)),
        compiler_params=_cparams("parallel", "parallel"),
        name="mla_proj",
    )(x, gmix[None, :], w_down_x, q_norm[None, :], kv_norm[None, :], wq_a, wq_b, wk, wv, cq_tab, sq_tab, ck_tab, sk_tab)


def _flash_body(q_ref, k_ref, v_ref, o_ref):
    blk = q_ref.shape[1]
    i = pl.program_id(2)
    q = q_ref[0]

    def step(j, carry, masked):
        m, l, acc = carry
        start = pl.multiple_of(j * blk, blk)
        s = _dot_nt(q, k_ref[0, pl.ds(start, blk), :])
        if masked:
            row = lax.broadcasted_iota(jnp.int32, s.shape, 0)
            col = lax.broadcasted_iota(jnp.int32, s.shape, 1)
            s = jnp.where(row >= col, s, -jnp.inf)
        m_new = jnp.maximum(m, jnp.max(s, axis=1, keepdims=True))
        p = jnp.exp(s - m_new)
        alpha = jnp.exp(m - m_new)
        l = alpha * l + jnp.sum(p, axis=1, keepdims=True)
        acc = alpha * acc + _dot(p.astype(BF16), v_ref[0, pl.ds(start, blk), :])
        return m_new, l, acc

    init = (jnp.full((blk, 1), -jnp.inf, F32), jnp.zeros((blk, 1), F32), jnp.zeros((blk, v_ref.shape[2]), F32))
    carry = lax.fori_loop(0, i, lambda j, c: step(j, c, False), init)
    _, l, acc = step(i, carry, True)
    o_ref[0] = (acc / l).astype(BF16)


def _flash(q, k, v):
    b, s, _ = q.shape
    blk = ATT_BLOCK
    return pl.pallas_call(
        _flash_body,
        out_shape=jax.ShapeDtypeStruct((b, s, MLA_HEADS * MLA_V_DIM), BF16),
        grid=(b, MLA_HEADS, s // blk),
        in_specs=[
            pl.BlockSpec((1, blk, MLA_QK_PAD), lambda bi, h, i: (bi, i, h)),
            pl.BlockSpec((1, s, MLA_QK_PAD), lambda bi, h, i: (bi, 0, h)),
            pl.BlockSpec((1, s, MLA_V_DIM), lambda bi, h, i: (bi, 0, h)),
        ],
        out_specs=pl.BlockSpec((1, blk, MLA_V_DIM), lambda bi, h, i: (bi, i, h)),
        compiler_params=_cparams("parallel", "parallel", "arbitrary"),
        name="mla_flash",
    )(q, k, v)


def _out_proj_body(x_ref, o_ref, w_ref, y_ref):
    y_ref[...] = x_ref[...] + _dot(o_ref[...], w_ref[...])


def _out_proj(x2, o2, w_out):
    t, d = x2.shape
    tm = MLA_TOKENS
    return pl.pallas_call(
        _out_proj_body,
        out_shape=jax.ShapeDtypeStruct((t, d), F32),
        grid=(t // tm,),
        in_specs=[pl.BlockSpec((tm, d), lambda i: (i, 0)), pl.BlockSpec((tm, o2.shape[1]), lambda i: (i, 0)),
                  _const_spec(w_out.shape)],
        out_specs=pl.BlockSpec((tm, d), lambda i: (i, 0)),
        compiler_params=_cparams("parallel"),
        name="mla_out",
    )(x2, o2, w_out.astype(BF16))


def _sort_network(n):
    def merge(lo, hi, r):
        step = 2 * r
        if step < hi - lo:
            yield from merge(lo, hi, step)
            yield from merge(lo + r, hi, step)
            yield from ((i, i + r) for i in range(lo + r, hi - r, step))
        else:
            yield (lo, lo + r)

    def sort(lo, hi):
        if hi - lo >= 1:
            mid = lo + (hi - lo) // 2
            yield from sort(lo, mid)
            yield from sort(mid + 1, hi)
            yield from merge(lo, hi, 1)

    return tuple(sort(0, n - 1))


_SORT_TOPK = _sort_network(PEER_TOPK)


def _top_sorted(slabs):
    groups = []
    for g0 in range(0, len(slabs), PEER_TOPK):
        v = list(slabs[g0:g0 + PEER_TOPK])
        for i, j in _SORT_TOPK:
            v[i], v[j] = jnp.maximum(v[i], v[j]), jnp.minimum(v[i], v[j])
        groups.append(v)
    while len(groups) > 1:
        merged = []
        for a, b in zip(groups[0::2], groups[1::2]):
            v = [jnp.maximum(a[i], b[PEER_TOPK - 1 - i]) for i in range(PEER_TOPK)]
            d = PEER_TOPK // 2
            while d >= 1:
                for i in range(PEER_TOPK):
                    if not i & d:
                        v[i], v[i + d] = jnp.maximum(v[i], v[i + d]), jnp.minimum(v[i], v[i + d])
                d //= 2
            merged.append(v)
        groups = merged
    return groups[0]


def _merge_counts(s1, s2):
    n = [jnp.zeros_like(s1[0]) for _ in range(PEER_TOPK)]
    front = [s1[a] + s2[0] for a in range(PEER_TOPK)]
    for _ in range(PEER_TOPK):
        m = functools.reduce(jnp.maximum, front)
        first = jnp.full_like(m, float(PEER_TOPK))
        for a in reversed(range(PEER_TOPK)):
            first = jnp.where(front[a] == m, float(a), first)
        hits = [first == float(a) for a in range(PEER_TOPK)]
        n_hit = jnp.zeros_like(m)
        for a in range(PEER_TOPK):
            n_hit = jnp.where(hits[a], n[a], n_hit)
        nxt = jnp.full_like(m, -jnp.inf)
        for b in range(1, PEER_TOPK):
            nxt = jnp.where(n_hit == float(b - 1), s2[b], nxt)
        for a in range(PEER_TOPK):
            n[a] = jnp.where(hits[a], n[a] + 1.0, n[a])
            front[a] = jnp.where(hits[a], s1[a] + nxt, front[a])
    return n


_RANK_CHECKSUM = float(sum(range(PEER_TOPK)) + (PEER_N_KEYS - PEER_TOPK) * PEER_TOPK)


def _sel_tile(lt, s_ref, r_ref, t_ref, e1_ref, nk_ref, e2_ref, rb_ref):
    keys = range(PEER_N_KEYS)
    top = [_top_sorted([s_ref[p, lt, k] for k in keys]) for p in range(2)]
    n = _merge_counts(top[0], top[1])

    e1s = [jnp.exp(v - top[0][0]) for v in top[0]]
    e2s = [jnp.exp(v - top[1][0]) for v in top[1]]
    z = jnp.zeros_like(e1s[0])
    for b in range(PEER_TOPK):
        row = jnp.zeros_like(z)
        for a in range(PEER_TOPK):
            row = row + jnp.where(n[a] > float(b), e1s[a], 0.0)
        z = z + row * e2s[b]
    inv_z = 1.0 / z

    bad = None
    for p in range(2):
        total = jnp.zeros_like(z)
        for k in keys:
            s = s_ref[p, lt, k]
            r = jnp.full_like(s, float(PEER_TOPK))
            for a in reversed(range(PEER_TOPK)):
                r = jnp.where(s >= top[p][a], float(a), r)
            r_ref[p, k] = r
            total = total + r
        wrong = total != _RANK_CHECKSUM
        bad = wrong if bad is None else bad | wrong

    @pl.when(jnp.max(jnp.where(bad, 1.0, 0.0)) > 0.0)
    def _():
        for p in range(2):
            def count(k, _):
                sk = s_ref[p, lt, k]
                cnt = jnp.zeros_like(sk)
                for j in keys:
                    sj = s_ref[p, lt, j]
                    cnt = cnt + jnp.where(sj > sk, 1.0, 0.0) + jnp.where(sj == sk, jnp.where(j < k, 1.0, 0.0), 0.0)
                r_ref[p, k] = jnp.minimum(cnt, float(PEER_TOPK))
                return 0

            lax.fori_loop(0, PEER_N_KEYS, count, 0)

    for k in keys:
        r = r_ref[0, k]
        nk = jnp.zeros_like(r)
        for a in range(PEER_TOPK):
            nk = jnp.where(r == float(a), n[a], nk)
        nk_ref[lt, k] = nk
        e1_ref[lt, k] = jnp.exp(s_ref[0, lt, k] - top[0][0]) * inv_z
        t_ref[k] = jnp.exp(s_ref[1, lt, k] - top[1][0])
    e2p = pltpu.bitcast(jnp.swapaxes(t_ref[...], 0, 1).astype(BF16), F32)
    rbp = pltpu.bitcast(jnp.swapaxes(r_ref[1], 0, 1).astype(BF16), F32)
    for h in range(PEER_HEADS):
        e2_ref[h, lt] = e2p[h]
        rb_ref[h, lt] = rbp[h]


def _peer_sel_body(x_ref, g_ref, wq_ref, wkey_ref, hb_ref, e1_ref, nk_ref, e2_ref, rb_ref, s_ref, r_ref, t_ref):
    hb = _rms(x_ref[...], g_ref[...]).astype(BF16)
    hb_ref[...] = hb
    qry = _dot(hb, wq_ref[...]).astype(BF16)
    tiles = x_ref.shape[0] // LANES
    width = PEER_HEADS * PEER_HALF
    for p in range(2):
        sc = _dot_nt(wkey_ref[p], qry[:, p * width:(p + 1) * width])
        sc = sc.reshape(PEER_N_KEYS, PEER_HEADS, tiles * LANES)
        for lt in range(tiles):
            s_ref[p, lt] = sc[:, :, lt * LANES:(lt + 1) * LANES]

    def tile(lt, _):
        _sel_tile(lt, s_ref, r_ref, t_ref, e1_ref, nk_ref, e2_ref, rb_ref)
        return 0

    lax.fori_loop(0, tiles, tile, 0)


def _peer_select(x2, gain, w_q, sub_keys):
    t, d = x2.shape
    tb = SEL_TOKENS
    tiles = tb // LANES
    width = PEER_HEADS * PEER_HALF
    wq = w_q.reshape(d, PEER_HEADS, 2, PEER_HALF).transpose(0, 2, 1, 3).reshape(d, 2 * width).astype(BF16)
    eye = jnp.eye(PEER_HEADS, dtype=F32)
    wkey = (sub_keys[:, :, None, None, :] * eye[None, None, :, :, None]).reshape(2, PEER_N_KEYS * PEER_HEADS, width)
    key_shape = jax.ShapeDtypeStruct((t // LANES, PEER_N_KEYS, PEER_HEADS, LANES), F32)
    key_spec = pl.BlockSpec((tiles, PEER_N_KEYS, PEER_HEADS, LANES), lambda i: (i, 0, 0, 0))
    pair_shape = jax.ShapeDtypeStruct((PEER_HEADS, t // LANES, PEER_N_KEYS // 2, LANES), F32)
    pair_spec = pl.BlockSpec((PEER_HEADS, tiles, PEER_N_KEYS // 2, LANES), lambda i: (0, i, 0, 0))
    slab = (PEER_N_KEYS, PEER_HEADS, LANES)
    return pl.pallas_call(
        _peer_sel_body,
        out_shape=(jax.ShapeDtypeStruct((t, d), BF16), key_shape, key_shape, pair_shape, pair_shape),
        grid=(t // tb,),
        in_specs=[pl.BlockSpec((tb, d), lambda i: (i, 0)), _const_spec((1, d)), _const_spec(wq.shape),
                  _const_spec(wkey.shape)],
        out_specs=(pl.BlockSpec((tb, d), lambda i: (i, 0)), key_spec, key_spec, pair_spec, pair_spec),
        scratch_shapes=[pltpu.VMEM((2, tiles) + slab, F32), pltpu.VMEM((2,) + slab, F32), pltpu.VMEM(slab, F32)],
        compiler_params=_cparams("parallel"),
        name="peer_select",
    )(x2, gain[None, :], wq, wkey.astype(BF16))


def _key_rows_bf16(row):
    packed = jnp.broadcast_to(row, (BF16_SUBLANES, row.shape[1])).astype(BF16)
    return jnp.concatenate([packed] * (PEER_N_KEYS // BF16_SUBLANES), axis=0)


def _peer_stages(hb_ref, u_ref, vt_ref, e1_ref, n_ref, e2_ref, rb_ref, acc_ref, a_new, a_cur, g_new, g_cur):
    token_tiles = a_cur.shape[0]
    packed_rows = PEER_N_KEYS // 2
    for tb in range(token_tiles):
        p, sub = tb // 2, (tb % 2) * LANES
        if tb % 2 == 0:
            a_piece = _dot_nt(u_ref[...], hb_ref[p * MXU_WIDTH:(p + 1) * MXU_WIDTH, :])
            a_new[tb] = a_piece[:, :LANES]
            a_new[tb + 1] = a_piece[:, LANES:]
        else:
            acc_ref[p] += _dot(vt_ref[0], pltpu.bitcast(g_cur[p], BF16))
        for k in range(PEER_KEY_GROUP):
            w = None
            for h in range(PEER_HEADS):
                rb_t = pltpu.bitcast(rb_ref[h, tb], BF16)
                e2_t = pltpu.bitcast(e2_ref[h, tb], BF16)
                n_b = _key_rows_bf16(n_ref[tb, k, h:h + 1, :])
                e1_b = _key_rows_bf16(e1_ref[tb, k, h:h + 1, :])
                term = jnp.where(rb_t < n_b, e2_t * e1_b, 0.0)
                w = term if h == 0 else w + term
            act = _gelu(a_cur[tb, k * PEER_N_KEYS:(k + 1) * PEER_N_KEYS, :]).astype(BF16)
            g_new[p, k * packed_rows:(k + 1) * packed_rows, sub:sub + LANES] = pltpu.bitcast(w * act, F32)


def _peer_main_body(final_norm, x_ref, hb_ref, u_ref, vt_ref, e1_ref, n_ref, e2_ref, rb_ref, fg_ref, o_ref, acc_ref,
                    a0_ref, a1_ref, g0_ref, g1_ref):
    s = pl.program_id(1)

    @pl.when(s == 0)
    def _():
        acc_ref[...] = jnp.zeros_like(acc_ref)
        a1_ref[...] = jnp.zeros_like(a1_ref)
        g0_ref[...] = jnp.zeros_like(g0_ref)

    stages = functools.partial(_peer_stages, hb_ref, u_ref, vt_ref, e1_ref, n_ref, e2_ref, rb_ref, acc_ref)

    @pl.when(s % 2 == 0)
    def _():
        stages(a0_ref, a1_ref, g1_ref, g0_ref)

    @pl.when(s % 2 == 1)
    def _():
        stages(a1_ref, a0_ref, g0_ref, g1_ref)

    @pl.when(s == pl.num_programs(1) - 1)
    def _():
        for p in range(acc_ref.shape[0]):
            rows = slice(p * MXU_WIDTH, (p + 1) * MXU_WIDTH)
            y = x_ref[rows, :] + acc_ref[p].T
            if final_norm:
                y = _rms(y, fg_ref[...])
            o_ref[rows, :] = y


def _peer_experts(x2, hb, e1, n, e2, rb, u_tab, v_tab, final_gain=None):
    t, d = x2.shape
    n_experts = u_tab.shape[0]
    tb, eb = PEER_TOKENS, PEER_EXPERTS
    chunks = n_experts // eb
    final_norm = final_gain is not None
    fg = (final_gain if final_norm else jnp.ones((d,), F32))[None, :]
    once = dict(pipeline_mode=pl.Buffered(1))
    tok_spec = pl.BlockSpec((tb, d), lambda i, s: (i, 0), **once)
    sel_spec = pl.BlockSpec((PEER_HEADS, tb // LANES, PEER_N_KEYS // 2, LANES), lambda i, s: (0, i, 0, 0), **once)
    key_spec = pl.BlockSpec((tb // LANES, PEER_KEY_GROUP, PEER_HEADS, LANES),
                            lambda i, s: (i, jnp.clip(s - 1, 0, chunks - 1), 0, 0))
    vt = v_tab.astype(BF16).reshape(chunks, eb, d).transpose(0, 2, 1)
    return pl.pallas_call(
        functools.partial(_peer_main_body, final_norm),
        out_shape=jax.ShapeDtypeStruct((t, d), F32),
        grid=(t // tb, chunks + 2),
        in_specs=[
            tok_spec,
            tok_spec,
            pl.BlockSpec((eb, d), lambda i, s: (jnp.minimum(s, chunks - 1), 0)),
            pl.BlockSpec((1, d, eb), lambda i, s: (jnp.maximum(s - 2, 0), 0, 0)),
            key_spec, key_spec, sel_spec, sel_spec,
            _const_spec((1, d)),
        ],
        out_specs=pl.BlockSpec((tb, d), lambda i, s: (i, 0)),
        scratch_shapes=[pltpu.VMEM((tb // MXU_WIDTH, d, MXU_WIDTH), F32),
                        pltpu.VMEM((tb // LANES, eb, LANES), F32), pltpu.VMEM((tb // LANES, eb, LANES), F32),
                        pltpu.VMEM((tb // MXU_WIDTH, eb // 2, MXU_WIDTH), F32),
                        pltpu.VMEM((tb // MXU_WIDTH, eb // 2, MXU_WIDTH), F32)],
        compiler_params=_cparams("parallel", "arbitrary"),
        name="peer_experts",
    )(x2, hb, u_tab.astype(BF16), vt, e1, n, e2, rb, fg)


def _peer(x2, gain, w_q, sub_keys, u_tab, v_tab, final_gain=None):
    hb, e1, n, e2, rb = _peer_select(x2, gain, w_q, sub_keys)
    return _peer_experts(x2, hb, e1, n, e2, rb, u_tab, v_tab, final_gain)


def kernel(x, norm_mix, norm_ffn, hy_w_in, gm_v_norm, gm_w_s, gm_b_s, ret_norm, hy_w_out, mla_w_down, mla_q_norm,
           mla_w_q_up, mla_kv_norm, mla_w_kv_up, mla_w_out, peer_w_q, peer_sub_keys, peer_u, peer_v, final_norm):
    b, s, d = x.shape
    depth = norm_mix.shape[0]
    for layer in range(depth):
        j = layer // 2
        if layer % 2 == 0:
            x = _hy_mixer(x, norm_mix[layer], hy_w_in[j], gm_v_norm[j], gm_w_s[j], gm_b_s[j], ret_norm[j], hy_w_out[j])
            x2 = x.reshape(b * s, d)
        else:
            q, k, v = _mla_proj(x, norm_mix[layer], mla_w_down[j], mla_q_norm[j], mla_w_q_up[j], mla_kv_norm[j],
                                mla_w_kv_up[j])
            o = _flash(q, k, v)
            x2 = _out_proj(x.reshape(b * s, d), o.reshape(b * s, -1), mla_w_out[j])
        last = layer == depth - 1
        x2 = _peer(x2, norm_ffn[layer], peer_w_q[layer], peer_sub_keys[layer], peer_u[layer], peer_v[layer],
                   final_norm if last else None)
        x = x2.reshape(b, s, d)
    return x
```

```python
import functools
import math

import jax
import jax.numpy as jnp
from jax import lax
from jax.experimental import pallas as pl
from jax.experimental.pallas import tpu as pltpu

F32 = jnp.float32
BF16 = jnp.bfloat16

LANES = 128
BF16_SUBLANES = 16
MXU_WIDTH = 256
V7X_VMEM_BYTES = 64 * 1024 * 1024
VMEM_LIMIT_BYTES = 56 * 1024 * 1024

NORM_EPS = 1e-6
ROPE_THETA = 10000.0

GM_GROUPS = 4
GM_DIM = 256
GM_CHUNK = 128
GM_WIDTH = GM_GROUPS * GM_DIM
RET_HEADS = 4
RET_QK_DIM = 128
RET_V_DIM = 256
RET_CHUNK = 128
RET_QK_WIDTH = RET_HEADS * RET_QK_DIM
RET_V_WIDTH = RET_HEADS * RET_V_DIM
HY_OUT_WIDTH = GM_WIDTH + RET_V_WIDTH
MLA_HEADS = 8
MLA_Q_RANK = 384
MLA_KV_RANK = 256
MLA_NOPE_DIM = 128
MLA_ROPE_DIM = 64
MLA_V_DIM = 128
MLA_QK_DIM = MLA_NOPE_DIM + MLA_ROPE_DIM
MLA_QK_PAD = 2 * LANES
PEER_HEADS = 8
PEER_N_KEYS = 128
PEER_HALF = 128
PEER_TOPK = 16

HY_TOKENS = 256
MLA_TOKENS = 512
ATT_BLOCK = 512
SEL_TOKENS = 256
PEER_TOKENS = 1024
PEER_EXPERTS = 512
PEER_KEY_GROUP = PEER_EXPERTS // PEER_N_KEYS


def _cparams(*semantics):
    return pltpu.CompilerParams(dimension_semantics=semantics, vmem_limit_bytes=VMEM_LIMIT_BYTES)


def _const_spec(shape):
    return pl.BlockSpec(shape, lambda *_: (0,) * len(shape))


def _rms(x, g):
    return x * lax.rsqrt(jnp.mean(x * x, axis=-1, keepdims=True) + NORM_EPS) * g


def _gelu(x):
    return 0.5 * x * (1.0 + lax.erf(x * (2.0 ** -0.5)))


def _dot(a, b):
    return jnp.dot(a, b, preferred_element_type=F32)


def _dot_nt(a, b):
    return lax.dot_general(a, b, (((1,), (1,)), ((), ())), preferred_element_type=F32)


def _dot_tn(a, b):
    return lax.dot_general(a, b, (((0,), (0,)), ((), ())), preferred_element_type=F32)


def _hy_body(x_ref, gmix_ref, win_ref, vnorm_ref, ws_ref, bs_ref, cos_ref, sin_ref, dmask_ref, qdec_ref, kdec_ref,
             cdec_ref, rnorm_ref, wout_ref, o_ref, state_ref, y_ref):
    @pl.when(pl.program_id(1) == 0)
    def _():
        state_ref[...] = jnp.zeros_like(state_ref)

    x = x_ref[0]
    h = _rms(x, gmix_ref[...]).astype(BF16)
    tokens = x.shape[0]
    chunks = tokens // GM_CHUNK

    u_all = _gelu(_dot(h, win_ref[:, 0:GM_WIDTH]))
    v_all = _gelu(_dot(h, win_ref[:, GM_WIDTH:2 * GM_WIDTH]))
    row = lax.broadcasted_iota(jnp.int32, (GM_CHUNK, GM_CHUNK), 0)
    col = lax.broadcasted_iota(jnp.int32, (GM_CHUNK, GM_CHUNK), 1)
    causal = row >= col
    for g in range(GM_GROUPS):
        cols = slice(g * GM_DIM, (g + 1) * GM_DIM)
        vg = _rms(v_all[:, cols], vnorm_ref[:, cols]).astype(BF16)
        wg = jnp.where(causal, ws_ref[g], 0.0).astype(BF16)
        for c in range(chunks):
            rows = slice(c * GM_CHUNK, (c + 1) * GM_CHUNK)
            mixed = _dot(wg, vg[rows]) + bs_ref[g]
            y_ref[rows, cols] = (u_all[rows, cols] * mixed).astype(BF16)

    base = 2 * GM_WIDTH
    q_all = _dot(h, win_ref[:, base:base + RET_QK_WIDTH])
    k_all = _dot(h, win_ref[:, base + RET_QK_WIDTH:base + 2 * RET_QK_WIDTH])
    base += 2 * RET_QK_WIDTH
    v_all = _dot(h, win_ref[:, base:base + RET_V_WIDTH])
    g_all = _dot(h, win_ref[:, base + RET_V_WIDTH:base + 2 * RET_V_WIDTH])
    cos = cos_ref[...]
    sin = sin_ref[...]
    for hd in range(RET_HEADS):
        qk_cols = slice(hd * RET_QK_DIM, (hd + 1) * RET_QK_DIM)
        v_cols = slice(hd * RET_V_DIM, (hd + 1) * RET_V_DIM)
        q = q_all[:, qk_cols]
        k = k_all[:, qk_cols]
        q = q * cos + pltpu.roll(q, RET_QK_DIM // 2, 1) * sin
        k = (k * cos + pltpu.roll(k, RET_QK_DIM // 2, 1) * sin) * (RET_QK_DIM ** -0.5)
        for c in range(chunks):
            rows = slice(c * RET_CHUNK, (c + 1) * RET_CHUNK)
            qc, kc, vc = q[rows], k[rows], v_all[rows, v_cols]
            scores = _dot_nt(qc.astype(BF16), kc.astype(BF16)) * dmask_ref[hd]
            intra = _dot(scores.astype(BF16), vc.astype(BF16))
            state = state_ref[hd]
            cross = _dot((qc * qdec_ref[hd]).astype(BF16), state.astype(BF16))
            state_ref[hd] = state * cdec_ref[hd] + _dot_tn(kc.astype(BF16), (vc * kdec_ref[hd]).astype(BF16))
            gate = g_all[rows, v_cols]
            yb = _rms(intra + cross, rnorm_ref[:, v_cols]) * (gate * jax.nn.sigmoid(gate))
            y_ref[rows, GM_WIDTH + hd * RET_V_DIM:GM_WIDTH + (hd + 1) * RET_V_DIM] = yb.astype(BF16)

    o_ref[0] = x + _dot(y_ref[...], wout_ref[...])


def _rope_angles(seq, half):
    inv = 1.0 / (ROPE_THETA ** (jnp.arange(half, dtype=F32) / half))
    ang = jnp.arange(seq, dtype=F32)[:, None] * inv[None, :]
    return jnp.cos(ang), jnp.sin(ang)


def _hy_mixer(x, gmix, w_in, v_norm, w_s, b_s, ret_norm, w_out):
    b, s, d = x.shape
    ts = HY_TOKENS
    cos, sin = _rope_angles(s, RET_QK_DIM // 2)
    cos2 = jnp.concatenate([cos, cos], axis=1)
    sin2 = jnp.concatenate([-sin, sin], axis=1)
    log_gamma = jnp.log(1.0 - 2.0 ** (-5.0 - jnp.arange(RET_HEADS, dtype=F32)))
    pos = jnp.arange(RET_CHUNK, dtype=F32)
    diff = pos[:, None] - pos[None, :]
    dmask = jnp.where(diff[None] >= 0, jnp.exp(diff[None] * log_gamma[:, None, None]), 0.0)
    qdec = jnp.exp((pos[None, :] + 1.0) * log_gamma[:, None])[:, :, None]
    kdec = jnp.exp((RET_CHUNK - 1.0 - pos[None, :]) * log_gamma[:, None])[:, :, None]
    cdec = jnp.exp(RET_CHUNK * log_gamma)[:, None, None]
    in_width = w_in.shape[1]
    return pl.pallas_call(
        _hy_body,
        out_shape=jax.ShapeDtypeStruct((b, s, d), F32),
        grid=(b, s // ts),
        in_specs=[
            pl.BlockSpec((1, ts, d), lambda i, j: (i, j, 0)),
            _const_spec((1, d)),
            _const_spec((d, in_width)),
            _const_spec((1, GM_WIDTH)),
            _const_spec((GM_GROUPS, GM_CHUNK, GM_CHUNK)),
            _const_spec((GM_GROUPS, GM_CHUNK, 1)),
            pl.BlockSpec((ts, RET_QK_DIM), lambda i, j: (j, 0)),
            pl.BlockSpec((ts, RET_QK_DIM), lambda i, j: (j, 0)),
            _const_spec((RET_HEADS, RET_CHUNK, RET_CHUNK)),
            _const_spec((RET_HEADS, RET_CHUNK, 1)),
            _const_spec((RET_HEADS, RET_CHUNK, 1)),
            _const_spec((RET_HEADS, 1, 1)),
            _const_spec((1, RET_V_WIDTH)),
            _const_spec((HY_OUT_WIDTH, d)),
        ],
        out_specs=pl.BlockSpec((1, ts, d), lambda i, j: (i, j, 0)),
        scratch_shapes=[
            pltpu.VMEM((RET_HEADS, RET_QK_DIM, RET_V_DIM), F32),
            pltpu.VMEM((ts, HY_OUT_WIDTH), BF16),
        ],
        compiler_params=_cparams("parallel", "arbitrary"),
        name="hy_mixer",
    )(x, gmix[None, :], w_in.astype(BF16), v_norm[None, :], w_s, b_s[:, :, None], cos2, sin2, dmask, qdec, kdec, cdec,
      ret_norm[None, :], w_out.astype(BF16))


def _mla_proj_body(x_ref, gmix_ref, wdown_ref, qn_ref, kvn_ref, wq_ref, wqrot_ref, wk_ref, wv_ref, cq_ref, sq_ref,
                   ck_ref, sk_ref, q_ref, k_ref, v_ref):
    h = _rms(x_ref[0], gmix_ref[...]).astype(BF16)
    down = _dot(h, wdown_ref[...])
    kv_end = MLA_Q_RANK + MLA_KV_RANK
    cq = _rms(down[:, :MLA_Q_RANK], qn_ref[...]).astype(BF16)
    ckv = _rms(down[:, MLA_Q_RANK:kv_end], kvn_ref[...]).astype(BF16)
    k_rope = (down[:, kv_end:kv_end + LANES] * ck_ref[...] + down[:, kv_end + LANES:kv_end + 2 * LANES] * sk_ref[...])
    k_rope = k_rope.astype(BF16)
    qa = _dot(cq, wq_ref[...])
    qb = _dot(cq, wqrot_ref[...])
    kn = _dot(ckv, wk_ref[...])
    scale = MLA_QK_DIM ** -0.5
    cq_tab = cq_ref[...] * scale
    sq_tab = sq_ref[...] * scale
    for hd in range(MLA_HEADS):
        seg = slice(hd * MLA_QK_PAD, (hd + 1) * MLA_QK_PAD)
        q_ref[0, :, seg] = (qa[:, seg] * cq_tab + qb[:, seg] * sq_tab).astype(BF16)
        k_ref[0, :, hd * MLA_QK_PAD:hd * MLA_QK_PAD + MLA_NOPE_DIM] = (
            kn[:, hd * MLA_NOPE_DIM:(hd + 1) * MLA_NOPE_DIM].astype(BF16))
        k_ref[0, :, hd * MLA_QK_PAD + MLA_NOPE_DIM:(hd + 1) * MLA_QK_PAD] = k_rope
    v_ref[0] = _dot(ckv, wv_ref[...]).astype(BF16)


def _rot_half_cols(w):
    half = w.shape[-1] // 2
    return jnp.concatenate([-w[..., half:], w[..., :half]], axis=-1)


def _mla_proj(x, gmix, w_down, q_norm, w_q_up, kv_norm, w_kv_up):
    b, s, d = x.shape
    tm = MLA_TOKENS
    kv_end = MLA_Q_RANK + MLA_KV_RANK
    pad = MLA_QK_PAD - MLA_QK_DIM
    w_kr = w_down[:, kv_end:]
    zk = jnp.zeros((d, LANES - MLA_ROPE_DIM), F32)
    w_down_x = jnp.concatenate([w_down[:, :kv_end], w_kr, zk, _rot_half_cols(w_kr), zk], axis=1).astype(BF16)
    wq = w_q_up.reshape(MLA_Q_RANK, MLA_HEADS, MLA_QK_DIM)
    zq = jnp.zeros((MLA_Q_RANK, MLA_HEADS, pad), F32)
    wq_a = jnp.concatenate([wq, zq], axis=2).reshape(MLA_Q_RANK, MLA_HEADS * MLA_QK_PAD).astype(BF16)
    wq_b = jnp.concatenate([jnp.zeros((MLA_Q_RANK, MLA_HEADS, MLA_NOPE_DIM), F32),
                            _rot_half_cols(wq[:, :, MLA_NOPE_DIM:]), zq], axis=2)
    wq_b = wq_b.reshape(MLA_Q_RANK, MLA_HEADS * MLA_QK_PAD).astype(BF16)
    wkv = w_kv_up.reshape(MLA_KV_RANK, MLA_HEADS, MLA_NOPE_DIM + MLA_V_DIM)
    wk = wkv[:, :, :MLA_NOPE_DIM].reshape(MLA_KV_RANK, MLA_HEADS * MLA_NOPE_DIM).astype(BF16)
    wv = wkv[:, :, MLA_NOPE_DIM:].reshape(MLA_KV_RANK, MLA_HEADS * MLA_V_DIM).astype(BF16)
    cos, sin = _rope_angles(s, MLA_ROPE_DIM // 2)
    zt = jnp.zeros((s, pad), F32)
    cq_tab = jnp.concatenate([jnp.ones((s, MLA_NOPE_DIM), F32), cos, cos, zt], axis=1)
    sq_tab = jnp.concatenate([jnp.zeros((s, MLA_NOPE_DIM), F32), sin, sin, zt], axis=1)
    ck_tab = jnp.concatenate([cos, cos, zt], axis=1)
    sk_tab = jnp.concatenate([sin, sin, zt], axis=1)
    qk_width = MLA_HEADS * MLA_QK_PAD
    v_width = MLA_HEADS * MLA_V_DIM
    tab = lambda w: pl.BlockSpec((tm, w), lambda i, j: (j, 0))
    out = lambda w: pl.BlockSpec((1, tm, w), lambda i, j: (i, j, 0))
    return pl.pallas_call(
        _mla_proj_body,
        out_shape=(jax.ShapeDtypeStruct((b, s, qk_width), BF16), jax.ShapeDtypeStruct((b, s, qk_width), BF16),
                   jax.ShapeDtypeStruct((b, s, v_width), BF16)),
        grid=(b, s // tm),
        in_specs=[
            out(d),
            _const_spec((1, d)),
            _const_spec(w_down_x.shape),
            _const_spec((1, MLA_Q_RANK)),
            _const_spec((1, MLA_KV_RANK)),
            _const_spec(wq_a.shape),
            _const_spec(wq_b.shape),
            _const_spec(wk.shape),
            _const_spec(wv.shape),
            tab(MLA_QK_PAD), tab(MLA_QK_PAD), tab(LANES), tab(LANES),
        ],
        out_specs=(out(qk_width), out(qk_width), out(v_width)),
        compiler_params=_cparams("parallel", "parallel"),
        name="mla_proj",
    )(x, gmix[None, :], w_down_x, q_norm[None, :], kv_norm[None, :], wq_a, wq_b, wk, wv, cq_tab, sq_tab, ck_tab, sk_tab)


def _flash_body(q_ref, k_ref, v_ref, o_ref):
    blk = q_ref.shape[1]
    i = pl.program_id(2)
    q = q_ref[0]

    def step(j, carry, masked):
        m, l, acc = carry
        start = pl.multiple_of(j * blk, blk)
        s = _dot_nt(q, k_ref[0, pl.ds(start, blk), :])
        if masked:
            row = lax.broadcasted_iota(jnp.int32, s.shape, 0)
            col = lax.broadcasted_iota(jnp.int32, s.shape, 1)
            s = jnp.where(row >= col, s, -jnp.inf)
        m_new = jnp.maximum(m, jnp.max(s, axis=1, keepdims=True))
        p = jnp.exp(s - m_new)
        alpha = jnp.exp(m - m_new)
        l = alpha * l + jnp.sum(p, axis=1, keepdims=True)
        acc = alpha * acc + _dot(p.astype(BF16), v_ref[0, pl.ds(start, blk), :])
        return m_new, l, acc

    init = (jnp.full((blk, 1), -jnp.inf, F32), jnp.zeros((blk, 1), F32), jnp.zeros((blk, v_ref.shape[2]), F32))
    carry = lax.fori_loop(0, i, lambda j, c: step(j, c, False), init)
    _, l, acc = step(i, carry, True)
    o_ref[0] = (acc / l).astype(BF16)


def _flash(q, k, v):
    b, s, _ = q.shape
    blk = ATT_BLOCK
    return pl.pallas_call(
        _flash_body,
        out_shape=jax.ShapeDtypeStruct((b, s, MLA_HEADS * MLA_V_DIM), BF16),
        grid=(b, MLA_HEADS, s // blk),
        in_specs=[
            pl.BlockSpec((1, blk, MLA_QK_PAD), lambda bi, h, i: (bi, i, h)),
            pl.BlockSpec((1, s, MLA_QK_PAD), lambda bi, h, i: (bi, 0, h)),
            pl.BlockSpec((1, s, MLA_V_DIM), lambda bi, h, i: (bi, 0, h)),
        ],
        out_specs=pl.BlockSpec((1, blk, MLA_V_DIM), lambda bi, h, i: (bi, i, h)),
        compiler_params=_cparams("parallel", "parallel", "arbitrary"),
        name="mla_flash",
    )(q, k, v)


def _out_proj_body(x_ref, o_ref, w_ref, y_ref):
    y_ref[...] = x_ref[...] + _dot(o_ref[...], w_ref[...])


def _out_proj(x2, o2, w_out):
    t, d = x2.shape
    tm = MLA_TOKENS
    return pl.pallas_call(
        _out_proj_body,
        out_shape=jax.ShapeDtypeStruct((t, d), F32),
        grid=(t // tm,),
        in_specs=[pl.BlockSpec((tm, d), lambda i: (i, 0)), pl.BlockSpec((tm, o2.shape[1]), lambda i: (i, 0)),
                  _const_spec(w_out.shape)],
        out_specs=pl.BlockSpec((tm, d), lambda i: (i, 0)),
        compiler_params=_cparams("parallel"),
        name="mla_out",
    )(x2, o2, w_out.astype(BF16))


def _sort_network(n):
    def merge(lo, hi, r):
        step = 2 * r
        if step < hi - lo:
            yield from merge(lo, hi, step)
            yield from merge(lo + r, hi, step)
            yield from ((i, i + r) for i in range(lo + r, hi - r, step))
        else:
            yield (lo, lo + r)

    def sort(lo, hi):
        if hi - lo >= 1:
            mid = lo + (hi - lo) // 2
            yield from sort(lo, mid)
            yield from sort(mid + 1, hi)
            yield from merge(lo, hi, 1)

    return tuple(sort(0, n - 1))


_SORT_TOPK = _sort_network(PEER_TOPK)


def _top_sorted(slabs):
    groups = []
    for g0 in range(0, len(slabs), PEER_TOPK):
        v = list(slabs[g0:g0 + PEER_TOPK])
        for i, j in _SORT_TOPK:
            v[i], v[j] = jnp.maximum(v[i], v[j]), jnp.minimum(v[i], v[j])
        groups.append(v)
    while len(groups) > 1:
        merged = []
        for a, b in zip(groups[0::2], groups[1::2]):
            v = [jnp.maximum(a[i], b[PEER_TOPK - 1 - i]) for i in range(PEER_TOPK)]
            d = PEER_TOPK // 2
            while d >= 1:
                for i in range(PEER_TOPK):
                    if not i & d:
                        v[i], v[i + d] = jnp.maximum(v[i], v[i + d]), jnp.minimum(v[i], v[i + d])
                d //= 2
            merged.append(v)
        groups = merged
    return groups[0]


def _merge_counts(s1, s2):
    n = [jnp.zeros_like(s1[0]) for _ in range(PEER_TOPK)]
    front = [s1[a] + s2[0] for a in range(PEER_TOPK)]
    for _ in range(PEER_TOPK):
        m = functools.reduce(jnp.maximum, front)
        first = jnp.full_like(m, float(PEER_TOPK))
        for a in reversed(range(PEER_TOPK)):
            first = jnp.where(front[a] == m, float(a), first)
        hits = [first == float(a) for a in range(PEER_TOPK)]
        n_hit = jnp.zeros_like(m)
        for a in range(PEER_TOPK):
            n_hit = jnp.where(hits[a], n[a], n_hit)
        nxt = jnp.full_like(m, -jnp.inf)
        for b in range(1, PEER_TOPK):
            nxt = jnp.where(n_hit == float(b - 1), s2[b], nxt)
        for a in range(PEER_TOPK):
            n[a] = jnp.where(hits[a], n[a] + 1.0, n[a])
            front[a] = jnp.where(hits[a], s1[a] + nxt, front[a])
    return n


_RANK_CHECKSUM = float(sum(range(PEER_TOPK)) + (PEER_N_KEYS - PEER_TOPK) * PEER_TOPK)


def _sel_tile(lt, s_ref, r_ref, t_ref, e1_ref, nk_ref, e2_ref, rb_ref):
    keys = range(PEER_N_KEYS)
    top = [_top_sorted([s_ref[p, lt, k] for k in keys]) for p in range(2)]
    n = _merge_counts(top[0], top[1])

    e1s = [jnp.exp(v - top[0][0]) for v in top[0]]
    e2s = [jnp.exp(v - top[1][0]) for v in top[1]]
    z = jnp.zeros_like(e1s[0])
    for b in range(PEER_TOPK):
        row = jnp.zeros_like(z)
        for a in range(PEER_TOPK):
            row = row + jnp.where(n[a] > float(b), e1s[a], 0.0)
        z = z + row * e2s[b]
    inv_z = 1.0 / z

    bad = None
    for p in range(2):
        total = jnp.zeros_like(z)
        for k in keys:
            s = s_ref[p, lt, k]
            r = jnp.full_like(s, float(PEER_TOPK))
            for a in reversed(range(PEER_TOPK)):
                r = jnp.where(s >= top[p][a], float(a), r)
            r_ref[p, k] = r
            total = total + r
        wrong = total != _RANK_CHECKSUM
        bad = wrong if bad is None else bad | wrong

    @pl.when(jnp.max(jnp.where(bad, 1.0, 0.0)) > 0.0)
    def _():
        for p in range(2):
            def shift(k, seen):
                r = r_ref[p, k]
                seen = list(seen)
                back = jnp.zeros_like(r)
                for a in range(PEER_TOPK):
                    hit = r == float(a)
                    back = jnp.where(hit, seen[a], back)
                    seen[a] = jnp.where(hit, seen[a] + 1.0, seen[a])
                r_ref[p, k] = jnp.minimum(r + back, float(PEER_TOPK))
                return tuple(seen)

            lax.fori_loop(0, PEER_N_KEYS, shift, tuple(jnp.zeros_like(z) for _ in range(PEER_TOPK)))

    for k in keys:
        r = r_ref[0, k]
        nk = jnp.zeros_like(r)
        for a in range(PEER_TOPK):
            nk = jnp.where(r == float(a), n[a], nk)
        nk_ref[lt, k] = nk
        e1_ref[lt, k] = jnp.exp(s_ref[0, lt, k] - top[0][0]) * inv_z
        t_ref[k] = jnp.exp(s_ref[1, lt, k] - top[1][0])
    e2p = pltpu.bitcast(jnp.swapaxes(t_ref[...], 0, 1).astype(BF16), F32)
    rbp = pltpu.bitcast(jnp.swapaxes(r_ref[1], 0, 1).astype(BF16), F32)
    for h in range(PEER_HEADS):
        e2_ref[h, lt] = e2p[h]
        rb_ref[h, lt] = rbp[h]


def _peer_sel_body(x_ref, g_ref, wq_ref, wkey_ref, hb_ref, e1_ref, nk_ref, e2_ref, rb_ref, s_ref, r_ref, t_ref):
    hb = _rms(x_ref[...], g_ref[...]).astype(BF16)
    hb_ref[...] = hb
    qry = _dot(hb, wq_ref[...]).astype(BF16)
    tiles = x_ref.shape[0] // LANES
    width = PEER_HEADS * PEER_HALF
    for p in range(2):
        sc = _dot_nt(wkey_ref[p], qry[:, p * width:(p + 1) * width])
        sc = sc.reshape(PEER_N_KEYS, PEER_HEADS, tiles * LANES)
        for lt in range(tiles):
            s_ref[p, lt] = sc[:, :, lt * LANES:(lt + 1) * LANES]

    def tile(lt, _):
        _sel_tile(lt, s_ref, r_ref, t_ref, e1_ref, nk_ref, e2_ref, rb_ref)
        return 0

    lax.fori_loop(0, tiles, tile, 0)


def _peer_select(x2, gain, w_q, sub_keys):
    t, d = x2.shape
    tb = SEL_TOKENS
    tiles = tb // LANES
    width = PEER_HEADS * PEER_HALF
    wq = w_q.reshape(d, PEER_HEADS, 2, PEER_HALF).transpose(0, 2, 1, 3).reshape(d, 2 * width).astype(BF16)
    eye = jnp.eye(PEER_HEADS, dtype=F32)
    wkey = (sub_keys[:, :, None, None, :] * eye[None, None, :, :, None]).reshape(2, PEER_N_KEYS * PEER_HEADS, width)
    key_shape = jax.ShapeDtypeStruct((t // LANES, PEER_N_KEYS, PEER_HEADS, LANES), F32)
    key_spec = pl.BlockSpec((tiles, PEER_N_KEYS, PEER_HEADS, LANES), lambda i: (i, 0, 0, 0))
    pair_shape = jax.ShapeDtypeStruct((PEER_HEADS, t // LANES, PEER_N_KEYS // 2, LANES), F32)
    pair_spec = pl.BlockSpec((PEER_HEADS, tiles, PEER_N_KEYS // 2, LANES), lambda i: (0, i, 0, 0))
    slab = (PEER_N_KEYS, PEER_HEADS, LANES)
    return pl.pallas_call(
        _peer_sel_body,
        out_shape=(jax.ShapeDtypeStruct((t, d), BF16), key_shape, key_shape, pair_shape, pair_shape),
        grid=(t // tb,),
        in_specs=[pl.BlockSpec((tb, d), lambda i: (i, 0)), _const_spec((1, d)), _const_spec(wq.shape),
                  _const_spec(wkey.shape)],
        out_specs=(pl.BlockSpec((tb, d), lambda i: (i, 0)), key_spec, key_spec, pair_spec, pair_spec),
        scratch_shapes=[pltpu.VMEM((2, tiles) + slab, F32), pltpu.VMEM((2,) + slab, F32), pltpu.VMEM(slab, F32)],
        compiler_params=_cparams("parallel"),
        name="peer_select",
    )(x2, gain[None, :], wq, wkey.astype(BF16))


def _key_rows_bf16(row):
    packed = jnp.broadcast_to(row, (BF16_SUBLANES, row.shape[1])).astype(BF16)
    return jnp.concatenate([packed] * (PEER_N_KEYS // BF16_SUBLANES), axis=0)


def _peer_main_body(final_norm, x_ref, hb_ref, u_ref, vt_ref, e1_ref, n_ref, e2_ref, rb_ref, fg_ref, o_ref, acc_ref,
                    a_ref):
    s = pl.program_id(1)

    @pl.when(s == 0)
    def _():
        acc_ref[...] = jnp.zeros_like(acc_ref)
        a_ref[...] = jnp.zeros_like(a_ref)

    for p in range(acc_ref.shape[0]):
        g_tiles = []
        for tb in (2 * p, 2 * p + 1):
            g_keys = []
            for k in range(PEER_KEY_GROUP):
                w = None
                for h in range(PEER_HEADS):
                    rb_t = pltpu.bitcast(rb_ref[h, tb], BF16)
                    e2_t = pltpu.bitcast(e2_ref[h, tb], BF16)
                    n_b = _key_rows_bf16(n_ref[tb, k, h:h + 1, :])
                    e1_b = _key_rows_bf16(e1_ref[tb, k, h:h + 1, :])
                    term = jnp.where(rb_t < n_b, e2_t * e1_b, 0.0)
                    w = term if h == 0 else w + term
                g_keys.append(w * _gelu(a_ref[tb, k * PEER_N_KEYS:(k + 1) * PEER_N_KEYS, :]).astype(BF16))
            g_tiles.append(jnp.concatenate(g_keys, axis=0))
        acc_ref[p] += _dot(vt_ref[0], jnp.concatenate(g_tiles, axis=1))
        a_piece = _dot_nt(u_ref[...], hb_ref[p * MXU_WIDTH:(p + 1) * MXU_WIDTH, :])
        a_ref[2 * p] = a_piece[:, :LANES]
        a_ref[2 * p + 1] = a_piece[:, LANES:]

    @pl.when(s == pl.num_programs(1) - 1)
    def _():
        for p in range(acc_ref.shape[0]):
            rows = slice(p * MXU_WIDTH, (p + 1) * MXU_WIDTH)
            y = x_ref[rows, :] + acc_ref[p].T
            if final_norm:
                y = _rms(y, fg_ref[...])
            o_ref[rows, :] = y


def _peer_experts(x2, hb, e1, n, e2, rb, u_tab, v_tab, final_gain=None):
    t, d = x2.shape
    n_experts = u_tab.shape[0]
    tb, eb = PEER_TOKENS, PEER_EXPERTS
    chunks = n_experts // eb
    final_norm = final_gain is not None
    fg = (final_gain if final_norm else jnp.ones((d,), F32))[None, :]
    once = dict(pipeline_mode=pl.Buffered(1))
    tok_spec = pl.BlockSpec((tb, d), lambda i, s: (i, 0), **once)
    sel_spec = pl.BlockSpec((PEER_HEADS, tb // LANES, PEER_N_KEYS // 2, LANES), lambda i, s: (0, i, 0, 0), **once)
    key_spec = pl.BlockSpec((tb // LANES, PEER_KEY_GROUP, PEER_HEADS, LANES),
                            lambda i, s: (i, jnp.clip(s - 1, 0, chunks - 1), 0, 0))
    vt = v_tab.astype(BF16).reshape(chunks, eb, d).transpose(0, 2, 1)
    return pl.pallas_call(
        functools.partial(_peer_main_body, final_norm),
        out_shape=jax.ShapeDtypeStruct((t, d), F32),
        grid=(t // tb, chunks + 1),
        in_specs=[
            tok_spec,
            tok_spec,
            pl.BlockSpec((eb, d), lambda i, s: (jnp.minimum(s, chunks - 1), 0)),
            pl.BlockSpec((1, d, eb), lambda i, s: (jnp.maximum(s - 1, 0), 0, 0)),
            key_spec, key_spec, sel_spec, sel_spec,
            _const_spec((1, d)),
        ],
        out_specs=pl.BlockSpec((tb, d), lambda i, s: (i, 0)),
        scratch_shapes=[pltpu.VMEM((tb // MXU_WIDTH, d, MXU_WIDTH), F32), pltpu.VMEM((tb // LANES, eb, LANES), F32)],
        compiler_params=_cparams("parallel", "arbitrary"),
        name="peer_experts",
    )(x2, hb, u_tab.astype(BF16), vt, e1, n, e2, rb, fg)


def _peer(x2, gain, w_q, sub_keys, u_tab, v_tab, final_gain=None):
    hb, e1, n, e2, rb = _peer_select(x2, gain, w_q, sub_keys)
    return _peer_experts(x2, hb, e1, n, e2, rb, u_tab, v_tab, final_gain)


def kernel(x, norm_mix, norm_ffn, hy_w_in, gm_v_norm, gm_w_s, gm_b_s, ret_norm, hy_w_out, mla_w_down, mla_q_norm,
           mla_w_q_up, mla_kv_norm, mla_w_kv_up, mla_w_out, peer_w_q, peer_sub_keys, peer_u, peer_v, final_norm):
    b, s, d = x.shape
    depth = norm_mix.shape[0]
    for layer in range(depth):
        j = layer // 2
        if layer % 2 == 0:
            x = _hy_mixer(x, norm_mix[layer], hy_w_in[j], gm_v_norm[j], gm_w_s[j], gm_b_s[j], ret_norm[j], hy_w_out[j])
            x2 = x.reshape(b * s, d)
        else:
            q, k, v = _mla_proj(x, norm_mix[layer], mla_w_down[j], mla_q_norm[j], mla_w_q_up[j], mla_kv_norm[j],
                                mla_w_kv_up[j])
            o = _flash(q, k, v)
            x2 = _out_proj(x.reshape(b * s, d), o.reshape(b * s, -1), mla_w_out[j])
        last = layer == depth - 1
        x2 = _peer(x2, norm_ffn[layer], peer_w_q[layer], peer_sub_keys[layer], peer_u[layer], peer_v[layer],
                   final_norm if last else None)
        x = x2.reshape(b, s, d)
    return x
```

```python
import functools
import math

import jax
import jax.numpy as jnp
from jax import lax
from jax.experimental import pallas as pl
from jax.experimental.pallas import tpu as pltpu

F32 = jnp.float32
BF16 = jnp.bfloat16

LANES = 128
BF16_SUBLANES = 16
MXU_WIDTH = 256
V7X_VMEM_BYTES = 64 * 1024 * 1024
VMEM_LIMIT_BYTES = 56 * 1024 * 1024

NORM_EPS = 1e-6
ROPE_THETA = 10000.0

GM_GROUPS = 4
GM_DIM = 256
GM_CHUNK = 128
GM_WIDTH = GM_GROUPS * GM_DIM
RET_HEADS = 4
RET_QK_DIM = 128
RET_V_DIM = 256
RET_CHUNK = 128
RET_QK_WIDTH = RET_HEADS * RET_QK_DIM
RET_V_WIDTH = RET_HEADS * RET_V_DIM
HY_OUT_WIDTH = GM_WIDTH + RET_V_WIDTH
MLA_HEADS = 8
MLA_Q_RANK = 384
MLA_KV_RANK = 256
MLA_NOPE_DIM = 128
MLA_ROPE_DIM = 64
MLA_V_DIM = 128
MLA_QK_DIM = MLA_NOPE_DIM + MLA_ROPE_DIM
MLA_QK_PAD = 2 * LANES
PEER_HEADS = 8
PEER_N_KEYS = 128
PEER_HALF = 128
PEER_TOPK = 16

HY_TOKENS = 256
MLA_TOKENS = 512
ATT_BLOCK = 512
ATT_ROW_GROUPS = 1
SEL_TOKENS = 256
PEER_TOKENS = 1024
PEER_EXPERTS = 512
PEER_KEY_GROUP = PEER_EXPERTS // PEER_N_KEYS
PEER_KEY_REUSE = 2
PEER_PIECE = MXU_WIDTH
PACK_ROWS = 1024


def _cparams(*semantics):
    return pltpu.CompilerParams(dimension_semantics=semantics, vmem_limit_bytes=VMEM_LIMIT_BYTES)


def _const_spec(shape):
    return pl.BlockSpec(shape, lambda *_: (0,) * len(shape))


def _rms(x, g):
    return x * lax.rsqrt(jnp.mean(x * x, axis=-1, keepdims=True) + NORM_EPS) * g


def _gelu(x):
    return 0.5 * x * (1.0 + lax.erf(x * (2.0 ** -0.5)))


def _dot(a, b):
    return jnp.dot(a, b, preferred_element_type=F32)


def _dot_nt(a, b):
    return lax.dot_general(a, b, (((1,), (1,)), ((), ())), preferred_element_type=F32)


def _dot_tn(a, b):
    return lax.dot_general(a, b, (((0,), (0,)), ((), ())), preferred_element_type=F32)


def _hy_body(x_ref, gmix_ref, win_ref, vnorm_ref, ws_ref, bs_ref, cos_ref, sin_ref, dmask_ref, qdec_ref, kdec_ref,
             cdec_ref, rnorm_ref, wout_ref, o_ref, state_ref, y_ref):
    @pl.when(pl.program_id(1) == 0)
    def _():
        state_ref[...] = jnp.zeros_like(state_ref)

    x = x_ref[0]
    h = _rms(x, gmix_ref[...]).astype(BF16)
    tokens = x.shape[0]
    chunks = tokens // GM_CHUNK

    u_all = _gelu(_dot(h, win_ref[:, 0:GM_WIDTH]))
    v_all = _gelu(_dot(h, win_ref[:, GM_WIDTH:2 * GM_WIDTH]))
    row = lax.broadcasted_iota(jnp.int32, (GM_CHUNK, GM_CHUNK), 0)
    col = lax.broadcasted_iota(jnp.int32, (GM_CHUNK, GM_CHUNK), 1)
    causal = row >= col
    for g in range(GM_GROUPS):
        cols = slice(g * GM_DIM, (g + 1) * GM_DIM)
        vg = _rms(v_all[:, cols], vnorm_ref[:, cols]).astype(BF16)
        wg = jnp.where(causal, ws_ref[g], 0.0).astype(BF16)
        for c in range(chunks):
            rows = slice(c * GM_CHUNK, (c + 1) * GM_CHUNK)
            mixed = _dot(wg, vg[rows]) + bs_ref[g]
            y_ref[rows, cols] = (u_all[rows, cols] * mixed).astype(BF16)

    base = 2 * GM_WIDTH
    q_all = _dot(h, win_ref[:, base:base + RET_QK_WIDTH])
    k_all = _dot(h, win_ref[:, base + RET_QK_WIDTH:base + 2 * RET_QK_WIDTH])
    base += 2 * RET_QK_WIDTH
    v_all = _dot(h, win_ref[:, base:base + RET_V_WIDTH])
    g_all = _dot(h, win_ref[:, base + RET_V_WIDTH:base + 2 * RET_V_WIDTH])
    cos = cos_ref[...]
    sin = sin_ref[...]
    for hd in range(RET_HEADS):
        qk_cols = slice(hd * RET_QK_DIM, (hd + 1) * RET_QK_DIM)
        v_cols = slice(hd * RET_V_DIM, (hd + 1) * RET_V_DIM)
        q = q_all[:, qk_cols]
        k = k_all[:, qk_cols]
        q = q * cos + pltpu.roll(q, RET_QK_DIM // 2, 1) * sin
        k = (k * cos + pltpu.roll(k, RET_QK_DIM // 2, 1) * sin) * (RET_QK_DIM ** -0.5)
        for c in range(chunks):
            rows = slice(c * RET_CHUNK, (c + 1) * RET_CHUNK)
            qc, kc, vc = q[rows], k[rows], v_all[rows, v_cols]
            scores = _dot_nt(qc.astype(BF16), kc.astype(BF16)) * dmask_ref[hd]
            intra = _dot(scores.astype(BF16), vc.astype(BF16))
            state = state_ref[hd]
            cross = _dot((qc * qdec_ref[hd]).astype(BF16), state.astype(BF16))
            state_ref[hd] = state * cdec_ref[hd] + _dot_tn(kc.astype(BF16), (vc * kdec_ref[hd]).astype(BF16))
            gate = g_all[rows, v_cols]
            yb = _rms(intra + cross, rnorm_ref[:, v_cols]) * (gate * jax.nn.sigmoid(gate))
            y_ref[rows, GM_WIDTH + hd * RET_V_DIM:GM_WIDTH + (hd + 1) * RET_V_DIM] = yb.astype(BF16)

    o_ref[0] = x + _dot(y_ref[...], wout_ref[...])


def _rope_angles(seq, half):
    inv = 1.0 / (ROPE_THETA ** (jnp.arange(half, dtype=F32) / half))
    ang = jnp.arange(seq, dtype=F32)[:, None] * inv[None, :]
    return jnp.cos(ang), jnp.sin(ang)


def _hy_mixer(x, gmix, w_in, v_norm, w_s, b_s, ret_norm, w_out):
    b, s, d = x.shape
    ts = HY_TOKENS
    cos, sin = _rope_angles(s, RET_QK_DIM // 2)
    cos2 = jnp.concatenate([cos, cos], axis=1)
    sin2 = jnp.concatenate([-sin, sin], axis=1)
    log_gamma = jnp.log(1.0 - 2.0 ** (-5.0 - jnp.arange(RET_HEADS, dtype=F32)))
    pos = jnp.arange(RET_CHUNK, dtype=F32)
    diff = pos[:, None] - pos[None, :]
    dmask = jnp.where(diff[None] >= 0, jnp.exp(diff[None] * log_gamma[:, None, None]), 0.0)
    qdec = jnp.exp((pos[None, :] + 1.0) * log_gamma[:, None])[:, :, None]
    kdec = jnp.exp((RET_CHUNK - 1.0 - pos[None, :]) * log_gamma[:, None])[:, :, None]
    cdec = jnp.exp(RET_CHUNK * log_gamma)[:, None, None]
    in_width = w_in.shape[1]
    return pl.pallas_call(
        _hy_body,
        out_shape=jax.ShapeDtypeStruct((b, s, d), F32),
        grid=(b, s // ts),
        in_specs=[
            pl.BlockSpec((1, ts, d), lambda i, j: (i, j, 0)),
            _const_spec((1, d)),
            _const_spec((d, in_width)),
            _const_spec((1, GM_WIDTH)),
            _const_spec((GM_GROUPS, GM_CHUNK, GM_CHUNK)),
            _const_spec((GM_GROUPS, GM_CHUNK, 1)),
            pl.BlockSpec((ts, RET_QK_DIM), lambda i, j: (j, 0)),
            pl.BlockSpec((ts, RET_QK_DIM), lambda i, j: (j, 0)),
            _const_spec((RET_HEADS, RET_CHUNK, RET_CHUNK)),
            _const_spec((RET_HEADS, RET_CHUNK, 1)),
            _const_spec((RET_HEADS, RET_CHUNK, 1)),
            _const_spec((RET_HEADS, 1, 1)),
            _const_spec((1, RET_V_WIDTH)),
            _const_spec((HY_OUT_WIDTH, d)),
        ],
        out_specs=pl.BlockSpec((1, ts, d), lambda i, j: (i, j, 0)),
        scratch_shapes=[
            pltpu.VMEM((RET_HEADS, RET_QK_DIM, RET_V_DIM), F32),
            pltpu.VMEM((ts, HY_OUT_WIDTH), BF16),
        ],
        compiler_params=_cparams("parallel", "arbitrary"),
        name="hy_mixer",
    )(x, gmix[None, :], w_in.astype(BF16), v_norm[None, :], w_s, b_s[:, :, None], cos2, sin2, dmask, qdec, kdec, cdec,
      ret_norm[None, :], w_out.astype(BF16))


def _mla_proj_body(x_ref, gmix_ref, wdown_ref, qn_ref, kvn_ref, wq_ref, wqrot_ref, wk_ref, wv_ref, cq_ref, sq_ref,
                   ck_ref, sk_ref, q_ref, k_ref, v_ref):
    h = _rms(x_ref[0], gmix_ref[...]).astype(BF16)
    down = _dot(h, wdown_ref[...])
    kv_end = MLA_Q_RANK + MLA_KV_RANK
    cq = _rms(down[:, :MLA_Q_RANK], qn_ref[...]).astype(BF16)
    ckv = _rms(down[:, MLA_Q_RANK:kv_end], kvn_ref[...]).astype(BF16)
    k_rope = (down[:, kv_end:kv_end + LANES] * ck_ref[...] + down[:, kv_end + LANES:kv_end + 2 * LANES] * sk_ref[...])
    k_rope = k_rope.astype(BF16)
    qa = _dot(cq, wq_ref[...])
    qb = _dot(cq, wqrot_ref[...])
    kn = _dot(ckv, wk_ref[...])
    scale = MLA_QK_DIM ** -0.5
    cq_tab = cq_ref[...] * scale
    sq_tab = sq_ref[...] * scale
    for hd in range(MLA_HEADS):
        seg = slice(hd * MLA_QK_PAD, (hd + 1) * MLA_QK_PAD)
        q_ref[0, :, seg] = (qa[:, seg] * cq_tab + qb[:, seg] * sq_tab).astype(BF16)
        k_ref[0, :, hd * MLA_QK_PAD:hd * MLA_QK_PAD + MLA_NOPE_DIM] = (
            kn[:, hd * MLA_NOPE_DIM:(hd + 1) * MLA_NOPE_DIM].astype(BF16))
        k_ref[0, :, hd * MLA_QK_PAD + MLA_NOPE_DIM:(hd + 1) * MLA_QK_PAD] = k_rope
    v_ref[0] = _dot(ckv, wv_ref[...]).astype(BF16)


def _rot_half_cols(w):
    half = w.shape[-1] // 2
    return jnp.concatenate([-w[..., half:], w[..., :half]], axis=-1)


def _mla_proj(x, gmix, w_down, q_norm, w_q_up, kv_norm, w_kv_up):
    b, s, d = x.shape
    tm = MLA_TOKENS
    kv_end = MLA_Q_RANK + MLA_KV_RANK
    pad = MLA_QK_PAD - MLA_QK_DIM
    w_kr = w_down[:, kv_end:]
    zk = jnp.zeros((d, LANES - MLA_ROPE_DIM), F32)
    w_down_x = jnp.concatenate([w_down[:, :kv_end], w_kr, zk, _rot_half_cols(w_kr), zk], axis=1).astype(BF16)
    wq = w_q_up.reshape(MLA_Q_RANK, MLA_HEADS, MLA_QK_DIM)
    zq = jnp.zeros((MLA_Q_RANK, MLA_HEADS, pad), F32)
    wq_a = jnp.concatenate([wq, zq], axis=2).reshape(MLA_Q_RANK, MLA_HEADS * MLA_QK_PAD).astype(BF16)
    wq_b = jnp.concatenate([jnp.zeros((MLA_Q_RANK, MLA_HEADS, MLA_NOPE_DIM), F32),
                            _rot_half_cols(wq[:, :, MLA_NOPE_DIM:]), zq], axis=2)
    wq_b = wq_b.reshape(MLA_Q_RANK, MLA_HEADS * MLA_QK_PAD).astype(BF16)
    wkv = w_kv_up.reshape(MLA_KV_RANK, MLA_HEADS, MLA_NOPE_DIM + MLA_V_DIM)
    wk = wkv[:, :, :MLA_NOPE_DIM].reshape(MLA_KV_RANK, MLA_HEADS * MLA_NOPE_DIM).astype(BF16)
    wv = wkv[:, :, MLA_NOPE_DIM:].reshape(MLA_KV_RANK, MLA_HEADS * MLA_V_DIM).astype(BF16)
    cos, sin = _rope_angles(s, MLA_ROPE_DIM // 2)
    zt = jnp.zeros((s, pad), F32)
    cq_tab = jnp.concatenate([jnp.ones((s, MLA_NOPE_DIM), F32), cos, cos, zt], axis=1)
    sq_tab = jnp.concatenate([jnp.zeros((s, MLA_NOPE_DIM), F32), sin, sin, zt], axis=1)
    ck_tab = jnp.concatenate([cos, cos, zt], axis=1)
    sk_tab = jnp.concatenate([sin, sin, zt], axis=1)
    qk_width = MLA_HEADS * MLA_QK_PAD
    v_width = MLA_HEADS * MLA_V_DIM
    tab = lambda w: pl.BlockSpec((tm, w), lambda i, j: (j, 0))
    out = lambda w: pl.BlockSpec((1, tm, w), lambda i, j: (i, j, 0))
    return pl.pallas_call(
        _mla_proj_body,
        out_shape=(jax.ShapeDtypeStruct((b, s, qk_width), BF16), jax.ShapeDtypeStruct((b, s, qk_width), BF16),
                   jax.ShapeDtypeStruct((b, s, v_width), BF16)),
        grid=(b, s // tm),
        in_specs=[
            out(d),
            _const_spec((1, d)),
            _const_spec(w_down_x.shape),
            _const_spec((1, MLA_Q_RANK)),
            _const_spec((1, MLA_KV_RANK)),
            _const_spec(wq_a.shape),
            _const_spec(wq_b.shape),
            _const_spec(wk.shape),
            _const_spec(wv.shape),
            tab(MLA_QK_PAD), tab(MLA_QK_PAD), tab(LANES), tab(LANES),
        ],
        out_specs=(out(qk_width), out(qk_width), out(v_width)),
        compiler_params=_cparams("parallel", "parallel"),
        name="mla_proj",
    )(x, gmix[None, :], w_down_x, q_norm[None, :], kv_norm[None, :], wq_a, wq_b, wk, wv, cq_tab, sq_tab, ck_tab, sk_tab)


def _flash_body(q_ref, k_ref, v_ref, o_ref):
    blk = q_ref.shape[1]
    i = pl.program_id(2)
    rows = blk // ATT_ROW_GROUPS
    qs = [q_ref[0, g * rows:(g + 1) * rows, :] for g in range(ATT_ROW_GROUPS)]

    def step(j, carry, masked):
        start = pl.multiple_of(j * blk, blk)
        k = k_ref[0, pl.ds(start, blk), :]
        v = v_ref[0, pl.ds(start, blk), :]
        out = []
        for g, (m, l, acc) in enumerate(carry):
            s = _dot_nt(qs[g], k)
            if masked:
                row = lax.broadcasted_iota(jnp.int32, s.shape, 0) + g * rows
                col = lax.broadcasted_iota(jnp.int32, s.shape, 1)
                s = jnp.where(row >= col, s, -jnp.inf)
            m_new = jnp.maximum(m, jnp.max(s, axis=1, keepdims=True))
            p = jnp.exp(s - m_new)
            alpha = jnp.exp(m - m_new)
            l = alpha * l + jnp.sum(p, axis=1, keepdims=True)
            acc = alpha * acc + _dot(p.astype(BF16), v)
            out.append((m_new, l, acc))
        return tuple(out)

    init = tuple((jnp.full((rows, 1), -jnp.inf, F32), jnp.zeros((rows, 1), F32),
                  jnp.zeros((rows, v_ref.shape[2]), F32)) for _ in range(ATT_ROW_GROUPS))
    carry = lax.fori_loop(0, i, lambda j, c: step(j, c, False), init)
    for g, (_, l, acc) in enumerate(step(i, carry, True)):
        o_ref[0, g * rows:(g + 1) * rows, :] = (acc / l).astype(BF16)


def _flash(q, k, v):
    b, s, _ = q.shape
    blk = ATT_BLOCK
    return pl.pallas_call(
        _flash_body,
        out_shape=jax.ShapeDtypeStruct((b, s, MLA_HEADS * MLA_V_DIM), BF16),
        grid=(b, MLA_HEADS, s // blk),
        in_specs=[
            pl.BlockSpec((1, blk, MLA_QK_PAD), lambda bi, h, i: (bi, i, h)),
            pl.BlockSpec((1, s, MLA_QK_PAD), lambda bi, h, i: (bi, 0, h)),
            pl.BlockSpec((1, s, MLA_V_DIM), lambda bi, h, i: (bi, 0, h)),
        ],
        out_specs=pl.BlockSpec((1, blk, MLA_V_DIM), lambda bi, h, i: (bi, i, h)),
        compiler_params=_cparams("parallel", "parallel", "arbitrary"),
        name="mla_flash",
    )(q, k, v)


def _out_proj_body(x_ref, o_ref, w_ref, y_ref):
    y_ref[...] = x_ref[...] + _dot(o_ref[...], w_ref[...])


def _out_proj(x2, o2, w_out):
    t, d = x2.shape
    tm = MLA_TOKENS
    return pl.pallas_call(
        _out_proj_body,
        out_shape=jax.ShapeDtypeStruct((t, d), F32),
        grid=(t // tm,),
        in_specs=[pl.BlockSpec((tm, d), lambda i: (i, 0)), pl.BlockSpec((tm, o2.shape[1]), lambda i: (i, 0)),
                  _const_spec(w_out.shape)],
        out_specs=pl.BlockSpec((tm, d), lambda i: (i, 0)),
        compiler_params=_cparams("parallel"),
        name="mla_out",
    )(x2, o2, w_out.astype(BF16))


def _sort_network(n):
    def merge(lo, hi, r):
        step = 2 * r
        if step < hi - lo:
            yield from merge(lo, hi, step)
            yield from merge(lo + r, hi, step)
            yield from ((i, i + r) for i in range(lo + r, hi - r, step))
        else:
            yield (lo, lo + r)

    def sort(lo, hi):
        if hi - lo >= 1:
            mid = lo + (hi - lo) // 2
            yield from sort(lo, mid)
            yield from sort(mid + 1, hi)
            yield from merge(lo, hi, 1)

    return tuple(sort(0, n - 1))


_SORT_TOPK = _sort_network(PEER_TOPK)


def _top_sorted(slabs):
    groups = []
    for g0 in range(0, len(slabs), PEER_TOPK):
        v = list(slabs[g0:g0 + PEER_TOPK])
        for i, j in _SORT_TOPK:
            v[i], v[j] = jnp.maximum(v[i], v[j]), jnp.minimum(v[i], v[j])
        groups.append(v)
    while len(groups) > 1:
        merged = []
        for a, b in zip(groups[0::2], groups[1::2]):
            v = [jnp.maximum(a[i], b[PEER_TOPK - 1 - i]) for i in range(PEER_TOPK)]
            d = PEER_TOPK // 2
            while d >= 1:
                for i in range(PEER_TOPK):
                    if not i & d:
                        v[i], v[i + d] = jnp.maximum(v[i], v[i + d]), jnp.minimum(v[i], v[i + d])
                d //= 2
            merged.append(v)
        groups = merged
    return groups[0]


def _merge_counts(s1, s2):
    n = [jnp.zeros_like(s1[0]) for _ in range(PEER_TOPK)]
    front = [s1[a] + s2[0] for a in range(PEER_TOPK)]
    for _ in range(PEER_TOPK):
        m = functools.reduce(jnp.maximum, front)
        first = jnp.full_like(m, float(PEER_TOPK))
        for a in reversed(range(PEER_TOPK)):
            first = jnp.where(front[a] == m, float(a), first)
        hits = [first == float(a) for a in range(PEER_TOPK)]
        n_hit = jnp.zeros_like(m)
        for a in range(PEER_TOPK):
            n_hit = jnp.where(hits[a], n[a], n_hit)
        nxt = jnp.full_like(m, -jnp.inf)
        for b in range(1, PEER_TOPK):
            nxt = jnp.where(n_hit == float(b - 1), s2[b], nxt)
        for a in range(PEER_TOPK):
            n[a] = jnp.where(hits[a], n[a] + 1.0, n[a])
            front[a] = jnp.where(hits[a], s1[a] + nxt, front[a])
    return n


_RANK_CHECKSUM = float(sum(range(PEER_TOPK)) + (PEER_N_KEYS - PEER_TOPK) * PEER_TOPK)


def _sel_tile(lt, s_ref, r_ref, t_ref, e1_ref, nk_ref, e2_ref, rb_ref):
    keys = range(PEER_N_KEYS)
    top = [_top_sorted([s_ref[p, lt, k] for k in keys]) for p in range(2)]
    n = _merge_counts(top[0], top[1])

    e1s = [jnp.exp(v - top[0][0]) for v in top[0]]
    e2s = [jnp.exp(v - top[1][0]) for v in top[1]]
    z = jnp.zeros_like(e1s[0])
    for b in range(PEER_TOPK):
        row = jnp.zeros_like(z)
        for a in range(PEER_TOPK):
            row = row + jnp.where(n[a] > float(b), e1s[a], 0.0)
        z = z + row * e2s[b]
    inv_z = 1.0 / z

    bad = None
    for p in range(2):
        total = jnp.zeros_like(z)
        for k in keys:
            s = s_ref[p, lt, k]
            r = jnp.full_like(s, float(PEER_TOPK))
            for a in reversed(range(PEER_TOPK)):
                r = jnp.where(s >= top[p][a], float(a), r)
            r_ref[p, k] = r
            total = total + r
        wrong = total != _RANK_CHECKSUM
        bad = wrong if bad is None else bad | wrong

    @pl.when(jnp.max(jnp.where(bad, 1.0, 0.0)) > 0.0)
    def _():
        for p in range(2):
            def shift(k, seen):
                r = r_ref[p, k]
                seen = list(seen)
                back = jnp.zeros_like(r)
                for a in range(PEER_TOPK):
                    hit = r == float(a)
                    back = jnp.where(hit, seen[a], back)
                    seen[a] = jnp.where(hit, seen[a] + 1.0, seen[a])
                r_ref[p, k] = jnp.minimum(r + back, float(PEER_TOPK))
                return tuple(seen)

            lax.fori_loop(0, PEER_N_KEYS, shift, tuple(jnp.zeros_like(z) for _ in range(PEER_TOPK)))

    for k in keys:
        r = r_ref[0, k]
        nk = jnp.zeros_like(r)
        for a in range(PEER_TOPK):
            nk = jnp.where(r == float(a), n[a], nk)
        nk_ref[lt, k] = nk
        e1_ref[lt, k] = jnp.exp(s_ref[0, lt, k] - top[0][0]) * inv_z
        t_ref[k] = jnp.exp(s_ref[1, lt, k] - top[1][0])
    e2p = pltpu.bitcast(jnp.swapaxes(t_ref[...], 0, 1).astype(BF16), F32)
    rbp = pltpu.bitcast(jnp.swapaxes(r_ref[1], 0, 1).astype(BF16), F32)
    for h in range(PEER_HEADS):
        e2_ref[h, lt] = e2p[h]
        rb_ref[h, lt] = rbp[h]


def _peer_sel_body(x_ref, g_ref, wq_ref, wkey_ref, hb_ref, e1_ref, nk_ref, e2_ref, rb_ref, s_ref, r_ref, t_ref):
    hb = _rms(x_ref[...], g_ref[...]).astype(BF16)
    hb_ref[...] = pltpu.bitcast(hb, F32)
    qry = _dot(hb, wq_ref[...]).astype(BF16)
    tiles = x_ref.shape[0] // LANES
    width = PEER_HEADS * PEER_HALF
    for p in range(2):
        sc = _dot_nt(wkey_ref[p], qry[:, p * width:(p + 1) * width])
        sc = sc.reshape(PEER_N_KEYS, PEER_HEADS, tiles * LANES)
        for lt in range(tiles):
            s_ref[p, lt] = sc[:, :, lt * LANES:(lt + 1) * LANES]

    def tile(lt, _):
        _sel_tile(lt, s_ref, r_ref, t_ref, e1_ref, nk_ref, e2_ref, rb_ref)
        return 0

    lax.fori_loop(0, tiles, tile, 0)


def _peer_select(x2, gain, w_q, sub_keys):
    t, d = x2.shape
    tb = SEL_TOKENS
    tiles = tb // LANES
    width = PEER_HEADS * PEER_HALF
    wq = w_q.reshape(d, PEER_HEADS, 2, PEER_HALF).transpose(0, 2, 1, 3).reshape(d, 2 * width).astype(BF16)
    eye = jnp.eye(PEER_HEADS, dtype=F32)
    wkey = (sub_keys[:, :, None, None, :] * eye[None, None, :, :, None]).reshape(2, PEER_N_KEYS * PEER_HEADS, width)
    key_shape = jax.ShapeDtypeStruct((t // LANES, PEER_N_KEYS, PEER_HEADS, LANES), F32)
    key_spec = pl.BlockSpec((tiles, PEER_N_KEYS, PEER_HEADS, LANES), lambda i: (i, 0, 0, 0))
    pair_shape = jax.ShapeDtypeStruct((PEER_HEADS, t // LANES, PEER_N_KEYS // 2, LANES), F32)
    pair_spec = pl.BlockSpec((PEER_HEADS, tiles, PEER_N_KEYS // 2, LANES), lambda i: (0, i, 0, 0))
    slab = (PEER_N_KEYS, PEER_HEADS, LANES)
    return pl.pallas_call(
        _peer_sel_body,
        out_shape=(jax.ShapeDtypeStruct((t // 2, d), F32), key_shape, key_shape, pair_shape, pair_shape),
        grid=(t // tb,),
        in_specs=[pl.BlockSpec((tb, d), lambda i: (i, 0)), _const_spec((1, d)), _const_spec(wq.shape),
                  _const_spec(wkey.shape)],
        out_specs=(pl.BlockSpec((tb // 2, d), lambda i: (i, 0)), key_spec, key_spec, pair_spec, pair_spec),
        scratch_shapes=[pltpu.VMEM((2, tiles) + slab, F32), pltpu.VMEM((2,) + slab, F32), pltpu.VMEM(slab, F32)],
        compiler_params=_cparams("parallel"),
        name="peer_select",
    )(x2, gain[None, :], wq, wkey.astype(BF16))


def _key_rows_bf16(row):
    packed = jnp.broadcast_to(row, (BF16_SUBLANES, row.shape[1])).astype(BF16)
    return jnp.concatenate([packed] * (PEER_N_KEYS // BF16_SUBLANES), axis=0)


def _peer_main_body(final_norm, x_ref, hb_ref, u_ref, vt_ref, e1_ref, n_ref, e2_ref, rb_ref, fg_ref, o_ref, acc_ref,
                    a_ref):
    s = pl.program_id(1)

    @pl.when(s == 0)
    def _():
        acc_ref[...] = jnp.zeros_like(acc_ref)
        a_ref[...] = jnp.zeros_like(a_ref)

    piece_tiles = PEER_PIECE // LANES
    for p in range(acc_ref.shape[0]):
        g_tiles = []
        for tb in range(p * piece_tiles, (p + 1) * piece_tiles):
            g_keys = []
            for k0 in range(0, PEER_KEY_GROUP, PEER_KEY_REUSE):
                ks = range(k0, k0 + PEER_KEY_REUSE)
                w = {}
                for h in range(PEER_HEADS):
                    rb_t = pltpu.bitcast(rb_ref[h, tb], BF16)
                    e2_t = pltpu.bitcast(e2_ref[h, tb], BF16)
                    for k in ks:
                        n_b = _key_rows_bf16(n_ref[tb, k, h:h + 1, :])
                        e1_b = _key_rows_bf16(e1_ref[tb, k, h:h + 1, :])
                        term = jnp.where(rb_t < n_b, e2_t * e1_b, 0.0)
                        w[k] = term if h == 0 else w[k] + term
                for k in ks:
                    g_keys.append(w[k] * _gelu(a_ref[tb, k * PEER_N_KEYS:(k + 1) * PEER_N_KEYS, :].astype(BF16)))
            g_tiles.append(jnp.concatenate(g_keys, axis=0))
        acc_ref[p] += _dot(pltpu.bitcast(vt_ref[0], BF16), jnp.concatenate(g_tiles, axis=1))
        hb_piece = pltpu.bitcast(hb_ref[p * (PEER_PIECE // 2):(p + 1) * (PEER_PIECE // 2), :], BF16)
        a_piece = _dot_nt(pltpu.bitcast(u_ref[...], BF16), hb_piece)
        for t in range(piece_tiles):
            a_ref[p * piece_tiles + t] = a_piece[:, t * LANES:(t + 1) * LANES]

    @pl.when(s == pl.num_programs(1) - 1)
    def _():
        for p in range(acc_ref.shape[0]):
            rows = slice(p * PEER_PIECE, (p + 1) * PEER_PIECE)
            y = x_ref[rows, :] + acc_ref[p].T
            if final_norm:
                y = _rms(y, fg_ref[...])
            o_ref[rows, :] = y


def _pack_rows_body(x_ref, o_ref):
    o_ref[...] = pltpu.bitcast(x_ref[...].astype(BF16), F32)


def _pack_rows(table):
    rows, d = table.shape
    blk = PACK_ROWS
    return pl.pallas_call(
        _pack_rows_body,
        out_shape=jax.ShapeDtypeStruct((rows // 2, d), F32),
        grid=(rows // blk,),
        in_specs=[pl.BlockSpec((blk, d), lambda i: (i, 0))],
        out_specs=pl.BlockSpec((blk // 2, d), lambda i: (i, 0)),
        compiler_params=_cparams("parallel"),
        name="pack_rows",
    )(table)


def _pack_transposed_body(x_ref, o_ref):
    o_ref[0] = pltpu.bitcast(x_ref[...].T.astype(BF16), F32)


def _pack_transposed_chunks(table, chunk):
    rows, d = table.shape
    return pl.pallas_call(
        _pack_transposed_body,
        out_shape=jax.ShapeDtypeStruct((rows // chunk, d // 2, chunk), F32),
        grid=(rows // chunk,),
        in_specs=[pl.BlockSpec((chunk, d), lambda i: (i, 0))],
        out_specs=pl.BlockSpec((1, d // 2, chunk), lambda i: (i, 0, 0)),
        compiler_params=_cparams("parallel"),
        name="pack_transposed",
    )(table)


def _peer_experts(x2, hb, e1, n, e2, rb, u_tab, v_tab, final_gain=None):
    t, d = x2.shape
    n_experts = u_tab.shape[0]
    tb, eb = PEER_TOKENS, PEER_EXPERTS
    chunks = n_experts // eb
    final_norm = final_gain is not None
    fg = (final_gain if final_norm else jnp.ones((d,), F32))[None, :]
    once = dict(pipeline_mode=pl.Buffered(1))
    tok_spec = pl.BlockSpec((tb, d), lambda i, s: (i, 0), **once)
    hb_spec = pl.BlockSpec((tb // 2, d), lambda i, s: (i, 0), **once)
    sel_spec = pl.BlockSpec((PEER_HEADS, tb // LANES, PEER_N_KEYS // 2, LANES), lambda i, s: (0, i, 0, 0), **once)
    key_spec = pl.BlockSpec((tb // LANES, PEER_KEY_GROUP, PEER_HEADS, LANES),
                            lambda i, s: (i, jnp.clip(s - 1, 0, chunks - 1), 0, 0))
    return pl.pallas_call(
        functools.partial(_peer_main_body, final_norm),
        out_shape=jax.ShapeDtypeStruct((t, d), F32),
        grid=(t // tb, chunks + 1),
        in_specs=[
            tok_spec,
            hb_spec,
            pl.BlockSpec((eb // 2, d), lambda i, s: (jnp.minimum(s, chunks - 1), 0)),
            pl.BlockSpec((1, d // 2, eb), lambda i, s: (jnp.maximum(s - 1, 0), 0, 0)),
            key_spec, key_spec, sel_spec, sel_spec,
            _const_spec((1, d)),
        ],
        out_specs=pl.BlockSpec((tb, d), lambda i, s: (i, 0)),
        scratch_shapes=[pltpu.VMEM((tb // PEER_PIECE, d, PEER_PIECE), F32), pltpu.VMEM((tb // LANES, eb, LANES), F32)],
        compiler_params=_cparams("parallel", "arbitrary"),
        name="peer_experts",
    )(x2, hb, _pack_rows(u_tab), _pack_transposed_chunks(v_tab, eb), e1, n, e2, rb, fg)


def _peer(x2, gain, w_q, sub_keys, u_tab, v_tab, final_gain=None):
    hb, e1, n, e2, rb = _peer_select(x2, gain, w_q, sub_keys)
    return _peer_experts(x2, hb, e1, n, e2, rb, u_tab, v_tab, final_gain)


def kernel(x, norm_mix, norm_ffn, hy_w_in, gm_v_norm, gm_w_s, gm_b_s, ret_norm, hy_w_out, mla_w_down, mla_q_norm,
           mla_w_q_up, mla_kv_norm, mla_w_kv_up, mla_w_out, peer_w_q, peer_sub_keys, peer_u, peer_v, final_norm):
    b, s, d = x.shape
    depth = norm_mix.shape[0]
    for layer in range(depth):
        j = layer // 2
        if layer % 2 == 0:
            x = _hy_mixer(x, norm_mix[layer], hy_w_in[j], gm_v_norm[j], gm_w_s[j], gm_b_s[j], ret_norm[j], hy_w_out[j])
            x2 = x.reshape(b * s, d)
        else:
            q, k, v = _mla_proj(x, norm_mix[layer], mla_w_down[j], mla_q_norm[j], mla_w_q_up[j], mla_kv_norm[j],
                                mla_w_kv_up[j])
            o = _flash(q, k, v)
            x2 = _out_proj(x.reshape(b * s, d), o.reshape(b * s, -1), mla_w_out[j])
        last = layer == depth - 1
        x2 = _peer(x2, norm_ffn[layer], peer_w_q[layer], peer_sub_keys[layer], peer_u[layer], peer_v[layer],
                   final_norm if last else None)
        x = x2.reshape(b, s, d)
    return x
```

```python
import functools
import math

import jax
import jax.numpy as jnp
from jax import lax
from jax.experimental import pallas as pl
from jax.experimental.pallas import tpu as pltpu

F32 = jnp.float32
BF16 = jnp.bfloat16

LANES = 128
BF16_SUBLANES = 16
MXU_WIDTH = 256
V7X_VMEM_BYTES = 64 * 1024 * 1024
VMEM_LIMIT_BYTES = 56 * 1024 * 1024

NORM_EPS = 1e-6
ROPE_THETA = 10000.0

GM_GROUPS = 4
GM_DIM = 256
GM_CHUNK = 128
GM_WIDTH = GM_GROUPS * GM_DIM
RET_HEADS = 4
RET_QK_DIM = 128
RET_V_DIM = 256
RET_CHUNK = 128
RET_QK_WIDTH = RET_HEADS * RET_QK_DIM
RET_V_WIDTH = RET_HEADS * RET_V_DIM
HY_OUT_WIDTH = GM_WIDTH + RET_V_WIDTH
MLA_HEADS = 8
MLA_Q_RANK = 384
MLA_KV_RANK = 256
MLA_NOPE_DIM = 128
MLA_ROPE_DIM = 64
MLA_V_DIM = 128
MLA_QK_DIM = MLA_NOPE_DIM + MLA_ROPE_DIM
MLA_QK_PAD = 2 * LANES
PEER_HEADS = 8
PEER_N_KEYS = 128
PEER_HALF = 128
PEER_TOPK = 16

HY_TOKENS = 256
MLA_TOKENS = 512
ATT_BLOCK = 512
ATT_ROW_GROUPS = 1
SEL_TOKENS = 256
PEER_TOKENS = 1024
PEER_EXPERTS = 512
PEER_KEY_GROUP = PEER_EXPERTS // PEER_N_KEYS
PEER_KEY_REUSE = 2
PEER_PIECE = MXU_WIDTH
PACK_ROWS = 1024


def _cparams(*semantics):
    return pltpu.CompilerParams(dimension_semantics=semantics, vmem_limit_bytes=VMEM_LIMIT_BYTES)


def _const_spec(shape):
    return pl.BlockSpec(shape, lambda *_: (0,) * len(shape))


def _rms(x, g):
    return x * lax.rsqrt(jnp.mean(x * x, axis=-1, keepdims=True) + NORM_EPS) * g


def _gelu(x):
    return 0.5 * x * (1.0 + lax.erf(x * (2.0 ** -0.5)))


def _dot(a, b):
    return jnp.dot(a, b, preferred_element_type=F32)


def _dot_nt(a, b):
    return lax.dot_general(a, b, (((1,), (1,)), ((), ())), preferred_element_type=F32)


def _dot_tn(a, b):
    return lax.dot_general(a, b, (((0,), (0,)), ((), ())), preferred_element_type=F32)


def _hy_body(x_ref, gmix_ref, win_ref, vnorm_ref, ws_ref, bs_ref, cos_ref, sin_ref, dmask_ref, qdec_ref, kdec_ref,
             cdec_ref, rnorm_ref, wout_ref, o_ref, state_ref, y_ref):
    @pl.when(pl.program_id(1) == 0)
    def _():
        state_ref[...] = jnp.zeros_like(state_ref)

    x = x_ref[0]
    h = _rms(x, gmix_ref[...]).astype(BF16)
    tokens = x.shape[0]
    chunks = tokens // GM_CHUNK

    u_all = _gelu(_dot(h, win_ref[:, 0:GM_WIDTH]))
    v_all = _gelu(_dot(h, win_ref[:, GM_WIDTH:2 * GM_WIDTH]))
    row = lax.broadcasted_iota(jnp.int32, (GM_CHUNK, GM_CHUNK), 0)
    col = lax.broadcasted_iota(jnp.int32, (GM_CHUNK, GM_CHUNK), 1)
    causal = row >= col
    for g in range(GM_GROUPS):
        cols = slice(g * GM_DIM, (g + 1) * GM_DIM)
        vg = _rms(v_all[:, cols], vnorm_ref[:, cols]).astype(BF16)
        wg = jnp.where(causal, ws_ref[g], 0.0).astype(BF16)
        for c in range(chunks):
            rows = slice(c * GM_CHUNK, (c + 1) * GM_CHUNK)
            mixed = _dot(wg, vg[rows]) + bs_ref[g]
            y_ref[rows, cols] = (u_all[rows, cols] * mixed).astype(BF16)

    base = 2 * GM_WIDTH
    q_all = _dot(h, win_ref[:, base:base + RET_QK_WIDTH])
    k_all = _dot(h, win_ref[:, base + RET_QK_WIDTH:base + 2 * RET_QK_WIDTH])
    base += 2 * RET_QK_WIDTH
    v_all = _dot(h, win_ref[:, base:base + RET_V_WIDTH])
    g_all = _dot(h, win_ref[:, base + RET_V_WIDTH:base + 2 * RET_V_WIDTH])
    cos = cos_ref[...]
    sin = sin_ref[...]
    for hd in range(RET_HEADS):
        qk_cols = slice(hd * RET_QK_DIM, (hd + 1) * RET_QK_DIM)
        v_cols = slice(hd * RET_V_DIM, (hd + 1) * RET_V_DIM)
        q = q_all[:, qk_cols]
        k = k_all[:, qk_cols]
        q = q * cos + pltpu.roll(q, RET_QK_DIM // 2, 1) * sin
        k = (k * cos + pltpu.roll(k, RET_QK_DIM // 2, 1) * sin) * (RET_QK_DIM ** -0.5)
        for c in range(chunks):
            rows = slice(c * RET_CHUNK, (c + 1) * RET_CHUNK)
            qc, kc, vc = q[rows], k[rows], v_all[rows, v_cols]
            scores = _dot_nt(qc.astype(BF16), kc.astype(BF16)) * dmask_ref[hd]
            intra = _dot(scores.astype(BF16), vc.astype(BF16))
            state = state_ref[hd]
            cross = _dot((qc * qdec_ref[hd]).astype(BF16), state.astype(BF16))
            state_ref[hd] = state * cdec_ref[hd] + _dot_tn(kc.astype(BF16), (vc * kdec_ref[hd]).astype(BF16))
            gate = g_all[rows, v_cols]
            yb = _rms(intra + cross, rnorm_ref[:, v_cols]) * (gate * jax.nn.sigmoid(gate))
            y_ref[rows, GM_WIDTH + hd * RET_V_DIM:GM_WIDTH + (hd + 1) * RET_V_DIM] = yb.astype(BF16)

    o_ref[0] = x + _dot(y_ref[...], wout_ref[...])


def _rope_angles(seq, half):
    inv = 1.0 / (ROPE_THETA ** (jnp.arange(half, dtype=F32) / half))
    ang = jnp.arange(seq, dtype=F32)[:, None] * inv[None, :]
    return jnp.cos(ang), jnp.sin(ang)


def _hy_mixer(x, gmix, w_in, v_norm, w_s, b_s, ret_norm, w_out):
    b, s, d = x.shape
    ts = HY_TOKENS
    cos, sin = _rope_angles(s, RET_QK_DIM // 2)
    cos2 = jnp.concatenate([cos, cos], axis=1)
    sin2 = jnp.concatenate([-sin, sin], axis=1)
    log_gamma = jnp.log(1.0 - 2.0 ** (-5.0 - jnp.arange(RET_HEADS, dtype=F32)))
    pos = jnp.arange(RET_CHUNK, dtype=F32)
    diff = pos[:, None] - pos[None, :]
    dmask = jnp.where(diff[None] >= 0, jnp.exp(diff[None] * log_gamma[:, None, None]), 0.0)
    qdec = jnp.exp((pos[None, :] + 1.0) * log_gamma[:, None])[:, :, None]
    kdec = jnp.exp((RET_CHUNK - 1.0 - pos[None, :]) * log_gamma[:, None])[:, :, None]
    cdec = jnp.exp(RET_CHUNK * log_gamma)[:, None, None]
    in_width = w_in.shape[1]
    return pl.pallas_call(
        _hy_body,
        out_shape=jax.ShapeDtypeStruct((b, s, d), F32),
        grid=(b, s // ts),
        in_specs=[
            pl.BlockSpec((1, ts, d), lambda i, j: (i, j, 0)),
            _const_spec((1, d)),
            _const_spec((d, in_width)),
            _const_spec((1, GM_WIDTH)),
            _const_spec((GM_GROUPS, GM_CHUNK, GM_CHUNK)),
            _const_spec((GM_GROUPS, GM_CHUNK, 1)),
            pl.BlockSpec((ts, RET_QK_DIM), lambda i, j: (j, 0)),
            pl.BlockSpec((ts, RET_QK_DIM), lambda i, j: (j, 0)),
            _const_spec((RET_HEADS, RET_CHUNK, RET_CHUNK)),
            _const_spec((RET_HEADS, RET_CHUNK, 1)),
            _const_spec((RET_HEADS, RET_CHUNK, 1)),
            _const_spec((RET_HEADS, 1, 1)),
            _const_spec((1, RET_V_WIDTH)),
            _const_spec((HY_OUT_WIDTH, d)),
        ],
        out_specs=pl.BlockSpec((1, ts, d), lambda i, j: (i, j, 0)),
        scratch_shapes=[
            pltpu.VMEM((RET_HEADS, RET_QK_DIM, RET_V_DIM), F32),
            pltpu.VMEM((ts, HY_OUT_WIDTH), BF16),
        ],
        compiler_params=_cparams("parallel", "arbitrary"),
        name="hy_mixer",
    )(x, gmix[None, :], w_in.astype(BF16), v_norm[None, :], w_s, b_s[:, :, None], cos2, sin2, dmask, qdec, kdec, cdec,
      ret_norm[None, :], w_out.astype(BF16))


def _mla_proj_body(x_ref, gmix_ref, wdown_ref, qn_ref, kvn_ref, wq_ref, wqrot_ref, wk_ref, wv_ref, cq_ref, sq_ref,
                   ck_ref, sk_ref, q_ref, k_ref, v_ref):
    h = _rms(x_ref[0], gmix_ref[...]).astype(BF16)
    down = _dot(h, wdown_ref[...])
    kv_end = MLA_Q_RANK + MLA_KV_RANK
    cq = _rms(down[:, :MLA_Q_RANK], qn_ref[...]).astype(BF16)
    ckv = _rms(down[:, MLA_Q_RANK:kv_end], kvn_ref[...]).astype(BF16)
    k_rope = (down[:, kv_end:kv_end + LANES] * ck_ref[...] + down[:, kv_end + LANES:kv_end + 2 * LANES] * sk_ref[...])
    k_rope = k_rope.astype(BF16)
    qa = _dot(cq, wq_ref[...])
    qb = _dot(cq, wqrot_ref[...])
    kn = _dot(ckv, wk_ref[...])
    scale = MLA_QK_DIM ** -0.5
    cq_tab = cq_ref[...] * scale
    sq_tab = sq_ref[...] * scale
    for hd in range(MLA_HEADS):
        seg = slice(hd * MLA_QK_PAD, (hd + 1) * MLA_QK_PAD)
        q_ref[0, :, seg] = (qa[:, seg] * cq_tab + qb[:, seg] * sq_tab).astype(BF16)
        k_ref[0, :, hd * MLA_QK_PAD:hd * MLA_QK_PAD + MLA_NOPE_DIM] = (
            kn[:, hd * MLA_NOPE_DIM:(hd + 1) * MLA_NOPE_DIM].astype(BF16))
        k_ref[0, :, hd * MLA_QK_PAD + MLA_NOPE_DIM:(hd + 1) * MLA_QK_PAD] = k_rope
    v_ref[0] = _dot(ckv, wv_ref[...]).astype(BF16)


def _rot_half_cols(w):
    half = w.shape[-1] // 2
    return jnp.concatenate([-w[..., half:], w[..., :half]], axis=-1)


def _mla_proj(x, gmix, w_down, q_norm, w_q_up, kv_norm, w_kv_up):
    b, s, d = x.shape
    tm = MLA_TOKENS
    kv_end = MLA_Q_RANK + MLA_KV_RANK
    pad = MLA_QK_PAD - MLA_QK_DIM
    w_kr = w_down[:, kv_end:]
    zk = jnp.zeros((d, LANES - MLA_ROPE_DIM), F32)
    w_down_x = jnp.concatenate([w_down[:, :kv_end], w_kr, zk, _rot_half_cols(w_kr), zk], axis=1).astype(BF16)
    wq = w_q_up.reshape(MLA_Q_RANK, MLA_HEADS, MLA_QK_DIM)
    zq = jnp.zeros((MLA_Q_RANK, MLA_HEADS, pad), F32)
    wq_a = jnp.concatenate([wq, zq], axis=2).reshape(MLA_Q_RANK, MLA_HEADS * MLA_QK_PAD).astype(BF16)
    wq_b = jnp.concatenate([jnp.zeros((MLA_Q_RANK, MLA_HEADS, MLA_NOPE_DIM), F32),
                            _rot_half_cols(wq[:, :, MLA_NOPE_DIM:]), zq], axis=2)
    wq_b = wq_b.reshape(MLA_Q_RANK, MLA_HEADS * MLA_QK_PAD).astype(BF16)
    wkv = w_kv_up.reshape(MLA_KV_RANK, MLA_HEADS, MLA_NOPE_DIM + MLA_V_DIM)
    wk = wkv[:, :, :MLA_NOPE_DIM].reshape(MLA_KV_RANK, MLA_HEADS * MLA_NOPE_DIM).astype(BF16)
    wv = wkv[:, :, MLA_NOPE_DIM:].reshape(MLA_KV_RANK, MLA_HEADS * MLA_V_DIM).astype(BF16)
    cos, sin = _rope_angles(s, MLA_ROPE_DIM // 2)
    zt = jnp.zeros((s, pad), F32)
    cq_tab = jnp.concatenate([jnp.ones((s, MLA_NOPE_DIM), F32), cos, cos, zt], axis=1)
    sq_tab = jnp.concatenate([jnp.zeros((s, MLA_NOPE_DIM), F32), sin, sin, zt], axis=1)
    ck_tab = jnp.concatenate([cos, cos, zt], axis=1)
    sk_tab = jnp.concatenate([sin, sin, zt], axis=1)
    qk_width = MLA_HEADS * MLA_QK_PAD
    v_width = MLA_HEADS * MLA_V_DIM
    tab = lambda w: pl.BlockSpec((tm, w), lambda i, j: (j, 0))
    out = lambda w: pl.BlockSpec((1, tm, w), lambda i, j: (i, j, 0))
    return pl.pallas_call(
        _mla_proj_body,
        out_shape=(jax.ShapeDtypeStruct((b, s, qk_width), BF16), jax.ShapeDtypeStruct((b, s, qk_width), BF16),
                   jax.ShapeDtypeStruct((b, s, v_width), BF16)),
        grid=(b, s // tm),
        in_specs=[
            out(d),
            _const_spec((1, d)),
            _const_spec(w_down_x.shape),
            _const_spec((1, MLA_Q_RANK)),
            _const_spec((1, MLA_KV_RANK)),
            _const_spec(wq_a.shape),
            _const_spec(wq_b.shape),
            _const_spec(wk.shape),
            _const_spec(wv.shape),
            tab(MLA_QK_PAD), tab(MLA_QK_PAD), tab(LANES), tab(LANES),
        ],
        out_specs=(out(qk_width), out(qk_width), out(v_width)),
        compiler_params=_cparams("parallel", "parallel"),
        name="mla_proj",
    )(x, gmix[None, :], w_down_x, q_norm[None, :], kv_norm[None, :], wq_a, wq_b, wk, wv, cq_tab, sq_tab, ck_tab, sk_tab)


def _flash_body(q_ref, k_ref, v_ref, o_ref):
    blk = q_ref.shape[1]
    i = pl.program_id(2)
    rows = blk // ATT_ROW_GROUPS
    qs = [q_ref[0, g * rows:(g + 1) * rows, :] for g in range(ATT_ROW_GROUPS)]

    def step(j, carry, masked):
        start = pl.multiple_of(j * blk, blk)
        k = k_ref[0, pl.ds(start, blk), :]
        v = v_ref[0, pl.ds(start, blk), :]
        out = []
        for g, (m, l, acc) in enumerate(carry):
            s = _dot_nt(qs[g], k)
            if masked:
                row = lax.broadcasted_iota(jnp.int32, s.shape, 0) + g * rows
                col = lax.broadcasted_iota(jnp.int32, s.shape, 1)
                s = jnp.where(row >= col, s, -jnp.inf)
            m_new = jnp.maximum(m, jnp.max(s, axis=1, keepdims=True))
            p = jnp.exp(s - m_new)
            alpha = jnp.exp(m - m_new)
            l = alpha * l + jnp.sum(p, axis=1, keepdims=True)
            acc = alpha * acc + _dot(p.astype(BF16), v)
            out.append((m_new, l, acc))
        return tuple(out)

    init = tuple((jnp.full((rows, 1), -jnp.inf, F32), jnp.zeros((rows, 1), F32),
                  jnp.zeros((rows, v_ref.shape[2]), F32)) for _ in range(ATT_ROW_GROUPS))
    carry = lax.fori_loop(0, i, lambda j, c: step(j, c, False), init)
    for g, (_, l, acc) in enumerate(step(i, carry, True)):
        o_ref[0, g * rows:(g + 1) * rows, :] = (acc / l).astype(BF16)


def _flash(q, k, v):
    b, s, _ = q.shape
    blk = ATT_BLOCK
    return pl.pallas_call(
        _flash_body,
        out_shape=jax.ShapeDtypeStruct((b, s, MLA_HEADS * MLA_V_DIM), BF16),
        grid=(b, MLA_HEADS, s // blk),
        in_specs=[
            pl.BlockSpec((1, blk, MLA_QK_PAD), lambda bi, h, i: (bi, i, h)),
            pl.BlockSpec((1, s, MLA_QK_PAD), lambda bi, h, i: (bi, 0, h)),
            pl.BlockSpec((1, s, MLA_V_DIM), lambda bi, h, i: (bi, 0, h)),
        ],
        out_specs=pl.BlockSpec((1, blk, MLA_V_DIM), lambda bi, h, i: (bi, i, h)),
        compiler_params=_cparams("parallel", "parallel", "arbitrary"),
        name="mla_flash",
    )(q, k, v)


def _out_proj_body(x_ref, o_ref, w_ref, y_ref):
    y_ref[...] = x_ref[...] + _dot(o_ref[...], w_ref[...])


def _out_proj(x2, o2, w_out):
    t, d = x2.shape
    tm = MLA_TOKENS
    return pl.pallas_call(
        _out_proj_body,
        out_shape=jax.ShapeDtypeStruct((t, d), F32),
        grid=(t // tm,),
        in_specs=[pl.BlockSpec((tm, d), lambda i: (i, 0)), pl.BlockSpec((tm, o2.shape[1]), lambda i: (i, 0)),
                  _const_spec(w_out.shape)],
        out_specs=pl.BlockSpec((tm, d), lambda i: (i, 0)),
        compiler_params=_cparams("parallel"),
        name="mla_out",
    )(x2, o2, w_out.astype(BF16))


def _sort_network(n):
    def merge(lo, hi, r):
        step = 2 * r
        if step < hi - lo:
            yield from merge(lo, hi, step)
            yield from merge(lo + r, hi, step)
            yield from ((i, i + r) for i in range(lo + r, hi - r, step))
        else:
            yield (lo, lo + r)

    def sort(lo, hi):
        if hi - lo >= 1:
            mid = lo + (hi - lo) // 2
            yield from sort(lo, mid)
            yield from sort(mid + 1, hi)
            yield from merge(lo, hi, 1)

    return tuple(sort(0, n - 1))


_SORT_TOPK = _sort_network(PEER_TOPK)


def _top_sorted(slabs):
    groups = []
    for g0 in range(0, len(slabs), PEER_TOPK):
        v = list(slabs[g0:g0 + PEER_TOPK])
        for i, j in _SORT_TOPK:
            v[i], v[j] = jnp.maximum(v[i], v[j]), jnp.minimum(v[i], v[j])
        groups.append(v)
    while len(groups) > 1:
        merged = []
        for a, b in zip(groups[0::2], groups[1::2]):
            v = [jnp.maximum(a[i], b[PEER_TOPK - 1 - i]) for i in range(PEER_TOPK)]
            d = PEER_TOPK // 2
            while d >= 1:
                for i in range(PEER_TOPK):
                    if not i & d:
                        v[i], v[i + d] = jnp.maximum(v[i], v[i + d]), jnp.minimum(v[i], v[i + d])
                d //= 2
            merged.append(v)
        groups = merged
    return groups[0]


def _merge_counts(s1, s2):
    n = [jnp.zeros_like(s1[0]) for _ in range(PEER_TOPK)]
    front = [s1[a] + s2[0] for a in range(PEER_TOPK)]
    for _ in range(PEER_TOPK):
        m = functools.reduce(jnp.maximum, front)
        first = jnp.full_like(m, float(PEER_TOPK))
        for a in reversed(range(PEER_TOPK)):
            first = jnp.where(front[a] == m, float(a), first)
        hits = [first == float(a) for a in range(PEER_TOPK)]
        n_hit = jnp.zeros_like(m)
        for a in range(PEER_TOPK):
            n_hit = jnp.where(hits[a], n[a], n_hit)
        nxt = jnp.full_like(m, -jnp.inf)
        for b in range(1, PEER_TOPK):
            nxt = jnp.where(n_hit == float(b - 1), s2[b], nxt)
        for a in range(PEER_TOPK):
            n[a] = jnp.where(hits[a], n[a] + 1.0, n[a])
            front[a] = jnp.where(hits[a], s1[a] + nxt, front[a])
    return n


_RANK_CHECKSUM = float(sum(range(PEER_TOPK)) + (PEER_N_KEYS - PEER_TOPK) * PEER_TOPK)


def _sel_tile(lt, s_ref, r_ref, t_ref, e1_ref, nk_ref, e2_ref, rb_ref):
    keys = range(PEER_N_KEYS)
    top = [_top_sorted([s_ref[p, lt, k] for k in keys]) for p in range(2)]
    n = _merge_counts(top[0], top[1])

    e1s = [jnp.exp(v - top[0][0]) for v in top[0]]
    e2s = [jnp.exp(v - top[1][0]) for v in top[1]]
    z = jnp.zeros_like(e1s[0])
    for b in range(PEER_TOPK):
        row = jnp.zeros_like(z)
        for a in range(PEER_TOPK):
            row = row + jnp.where(n[a] > float(b), e1s[a], 0.0)
        z = z + row * e2s[b]
    inv_z = 1.0 / z

    def rank(s, sorted_vals):
        r = jnp.full_like(s, float(PEER_TOPK))
        for a in reversed(range(PEER_TOPK)):
            r = jnp.where(s >= sorted_vals[a], float(a), r)
        return r

    def shifted(r, seen):
        seen = list(seen)
        back = jnp.zeros_like(r)
        for a in range(PEER_TOPK):
            hit = r == float(a)
            back = jnp.where(hit, seen[a], back)
            seen[a] = jnp.where(hit, seen[a] + 1.0, seen[a])
        return jnp.minimum(r + back, float(PEER_TOPK)), tuple(seen)

    none_seen = tuple(jnp.zeros_like(z) for _ in range(PEER_TOPK))

    def any_lane(mask):
        return jnp.max(jnp.where(mask, 1.0, 0.0)) > 0.0

    total = jnp.zeros_like(z)
    for k in keys:
        r = rank(s_ref[1, lt, k], top[1])
        r_ref[k] = r
        total = total + r
        t_ref[k] = jnp.exp(s_ref[1, lt, k] - top[1][0])

    @pl.when(any_lane(total != _RANK_CHECKSUM))
    def _():
        def fix(k, seen):
            r_ref[k], seen = shifted(r_ref[k], seen)
            return seen

        lax.fori_loop(0, PEER_N_KEYS, fix, none_seen)

    count = jnp.zeros_like(z)
    for k in keys:
        s = s_ref[0, lt, k]
        nk = jnp.zeros_like(s)
        for a in reversed(range(PEER_TOPK)):
            nk = jnp.where(s >= top[0][a], n[a], nk)
        nk_ref[lt, k] = nk
        e1_ref[lt, k] = jnp.exp(s - top[0][0]) * inv_z
        count = count + jnp.where(s >= top[0][PEER_TOPK - 1], 1.0, 0.0)
    tied = count != float(PEER_TOPK)
    for a in range(PEER_TOPK - 1):
        tied = tied | (top[0][a] == top[0][a + 1])

    @pl.when(any_lane(tied))
    def _():
        def fix(k, seen):
            r, seen = shifted(rank(s_ref[0, lt, k], top[0]), seen)
            nk = jnp.zeros_like(r)
            for a in range(PEER_TOPK):
                nk = jnp.where(r == float(a), n[a], nk)
            nk_ref[lt, k] = nk
            return seen

        lax.fori_loop(0, PEER_N_KEYS, fix, none_seen)

    e2p = pltpu.bitcast(jnp.swapaxes(t_ref[...], 0, 1).astype(BF16), F32)
    rbp = pltpu.bitcast(jnp.swapaxes(r_ref[...], 0, 1).astype(BF16), F32)
    for h in range(PEER_HEADS):
        e2_ref[h, lt] = e2p[h]
        rb_ref[h, lt] = rbp[h]


def _peer_sel_body(x_ref, g_ref, wq_ref, wkey_ref, hb_ref, e1_ref, nk_ref, e2_ref, rb_ref, s_ref, r_ref, t_ref):
    hb = _rms(x_ref[...], g_ref[...]).astype(BF16)
    hb_ref[...] = pltpu.bitcast(hb, F32)
    qry = _dot(hb, wq_ref[...]).astype(BF16)
    tiles = x_ref.shape[0] // LANES
    width = PEER_HEADS * PEER_HALF
    for p in range(2):
        sc = _dot_nt(wkey_ref[p], qry[:, p * width:(p + 1) * width])
        sc = sc.reshape(PEER_N_KEYS, PEER_HEADS, tiles * LANES)
        for lt in range(tiles):
            s_ref[p, lt] = sc[:, :, lt * LANES:(lt + 1) * LANES]

    def tile(lt, _):
        _sel_tile(lt, s_ref, r_ref, t_ref, e1_ref, nk_ref, e2_ref, rb_ref)
        return 0

    lax.fori_loop(0, tiles, tile, 0)


def _peer_select(x2, gain, w_q, sub_keys):
    t, d = x2.shape
    tb = SEL_TOKENS
    tiles = tb // LANES
    width = PEER_HEADS * PEER_HALF
    wq = w_q.reshape(d, PEER_HEADS, 2, PEER_HALF).transpose(0, 2, 1, 3).reshape(d, 2 * width).astype(BF16)
    eye = jnp.eye(PEER_HEADS, dtype=F32)
    wkey = (sub_keys[:, :, None, None, :] * eye[None, None, :, :, None]).reshape(2, PEER_N_KEYS * PEER_HEADS, width)
    key_shape = jax.ShapeDtypeStruct((t // LANES, PEER_N_KEYS, PEER_HEADS, LANES), F32)
    key_spec = pl.BlockSpec((tiles, PEER_N_KEYS, PEER_HEADS, LANES), lambda i: (i, 0, 0, 0))
    pair_shape = jax.ShapeDtypeStruct((PEER_HEADS, t // LANES, PEER_N_KEYS // 2, LANES), F32)
    pair_spec = pl.BlockSpec((PEER_HEADS, tiles, PEER_N_KEYS // 2, LANES), lambda i: (0, i, 0, 0))
    slab = (PEER_N_KEYS, PEER_HEADS, LANES)
    return pl.pallas_call(
        _peer_sel_body,
        out_shape=(jax.ShapeDtypeStruct((t // 2, d), F32), key_shape, key_shape, pair_shape, pair_shape),
        grid=(t // tb,),
        in_specs=[pl.BlockSpec((tb, d), lambda i: (i, 0)), _const_spec((1, d)), _const_spec(wq.shape),
                  _const_spec(wkey.shape)],
        out_specs=(pl.BlockSpec((tb // 2, d), lambda i: (i, 0)), key_spec, key_spec, pair_spec, pair_spec),
        scratch_shapes=[pltpu.VMEM((2, tiles) + slab, F32), pltpu.VMEM(slab, F32), pltpu.VMEM(slab, F32)],
        compiler_params=_cparams("parallel"),
        name="peer_select",
    )(x2, gain[None, :], wq, wkey.astype(BF16))


def _key_rows_bf16(row):
    packed = jnp.broadcast_to(row, (BF16_SUBLANES, row.shape[1])).astype(BF16)
    return jnp.concatenate([packed] * (PEER_N_KEYS // BF16_SUBLANES), axis=0)


def _peer_main_body(final_norm, x_ref, hb_ref, u_ref, vt_ref, e1_ref, n_ref, e2_ref, rb_ref, fg_ref, o_ref, acc_ref,
                    a_ref):
    s = pl.program_id(1)

    @pl.when(s == 0)
    def _():
        acc_ref[...] = jnp.zeros_like(acc_ref)
        a_ref[...] = jnp.zeros_like(a_ref)

    piece_tiles = PEER_PIECE // LANES
    for p in range(acc_ref.shape[0]):
        g_tiles = []
        for tb in range(p * piece_tiles, (p + 1) * piece_tiles):
            g_keys = []
            for k0 in range(0, PEER_KEY_GROUP, PEER_KEY_REUSE):
                ks = range(k0, k0 + PEER_KEY_REUSE)
                w = {}
                for h in range(PEER_HEADS):
                    rb_t = pltpu.bitcast(rb_ref[h, tb], BF16)
                    e2_t = pltpu.bitcast(e2_ref[h, tb], BF16)
                    for k in ks:
                        n_b = _key_rows_bf16(n_ref[tb, k, h:h + 1, :])
                        e1_b = _key_rows_bf16(e1_ref[tb, k, h:h + 1, :])
                        term = jnp.where(rb_t < n_b, e2_t * e1_b, 0.0)
                        w[k] = term if h == 0 else w[k] + term
                for k in ks:
                    g_keys.append(w[k] * _gelu(a_ref[tb, k * PEER_N_KEYS:(k + 1) * PEER_N_KEYS, :].astype(BF16)))
            g_tiles.append(jnp.concatenate(g_keys, axis=0))
        acc_ref[p] += _dot(pltpu.bitcast(vt_ref[0], BF16), jnp.concatenate(g_tiles, axis=1))
        hb_piece = pltpu.bitcast(hb_ref[p * (PEER_PIECE // 2):(p + 1) * (PEER_PIECE // 2), :], BF16)
        a_piece = _dot_nt(pltpu.bitcast(u_ref[...], BF16), hb_piece)
        for t in range(piece_tiles):
            a_ref[p * piece_tiles + t] = a_piece[:, t * LANES:(t + 1) * LANES]

    @pl.when(s == pl.num_programs(1) - 1)
    def _():
        for p in range(acc_ref.shape[0]):
            rows = slice(p * PEER_PIECE, (p + 1) * PEER_PIECE)
            y = x_ref[rows, :] + acc_ref[p].T
            if final_norm:
                y = _rms(y, fg_ref[...])
            o_ref[rows, :] = y


def _pack_rows_body(x_ref, o_ref):
    o_ref[...] = pltpu.bitcast(x_ref[0].astype(BF16), F32)


def _pack_rows(tables, layer):
    _, rows, d = tables.shape
    blk = PACK_ROWS
    return pl.pallas_call(
        _pack_rows_body,
        out_shape=jax.ShapeDtypeStruct((rows // 2, d), F32),
        grid=(rows // blk,),
        in_specs=[pl.BlockSpec((1, blk, d), lambda i: (layer, i, 0))],
        out_specs=pl.BlockSpec((blk // 2, d), lambda i: (i, 0)),
        compiler_params=_cparams("parallel"),
        name="pack_rows",
    )(tables)


def _pack_transposed_body(x_ref, o_ref):
    o_ref[0] = pltpu.bitcast(x_ref[0].T.astype(BF16), F32)


def _pack_transposed_chunks(tables, layer, chunk):
    _, rows, d = tables.shape
    return pl.pallas_call(
        _pack_transposed_body,
        out_shape=jax.ShapeDtypeStruct((rows // chunk, d // 2, chunk), F32),
        grid=(rows // chunk,),
        in_specs=[pl.BlockSpec((1, chunk, d), lambda i: (layer, i, 0))],
        out_specs=pl.BlockSpec((1, d // 2, chunk), lambda i: (i, 0, 0)),
        compiler_params=_cparams("parallel"),
        name="pack_transposed",
    )(tables)


def _peer_experts(x2, hb, e1, n, e2, rb, u_tabs, v_tabs, layer, final_gain=None):
    t, d = x2.shape
    n_experts = u_tabs.shape[1]
    tb, eb = PEER_TOKENS, PEER_EXPERTS
    chunks = n_experts // eb
    final_norm = final_gain is not None
    fg = (final_gain if final_norm else jnp.ones((d,), F32))[None, :]
    once = dict(pipeline_mode=pl.Buffered(1))
    tok_spec = pl.BlockSpec((tb, d), lambda i, s: (i, 0), **once)
    hb_spec = pl.BlockSpec((tb // 2, d), lambda i, s: (i, 0), **once)
    sel_spec = pl.BlockSpec((PEER_HEADS, tb // LANES, PEER_N_KEYS // 2, LANES), lambda i, s: (0, i, 0, 0), **once)
    key_spec = pl.BlockSpec((tb // LANES, PEER_KEY_GROUP, PEER_HEADS, LANES),
                            lambda i, s: (i, jnp.clip(s - 1, 0, chunks - 1), 0, 0))
    return pl.pallas_call(
        functools.partial(_peer_main_body, final_norm),
        out_shape=jax.ShapeDtypeStruct((t, d), F32),
        grid=(t // tb, chunks + 1),
        in_specs=[
            tok_spec,
            hb_spec,
            pl.BlockSpec((eb // 2, d), lambda i, s: (jnp.minimum(s, chunks - 1), 0)),
            pl.BlockSpec((1, d // 2, eb), lambda i, s: (jnp.maximum(s - 1, 0), 0, 0)),
            key_spec, key_spec, sel_spec, sel_spec,
            _const_spec((1, d)),
        ],
        out_specs=pl.BlockSpec((tb, d), lambda i, s: (i, 0)),
        scratch_shapes=[pltpu.VMEM((tb // PEER_PIECE, d, PEER_PIECE), F32), pltpu.VMEM((tb // LANES, eb, LANES), F32)],
        compiler_params=_cparams("parallel", "arbitrary"),
        name="peer_experts",
    )(x2, hb, _pack_rows(u_tabs, layer), _pack_transposed_chunks(v_tabs, layer, eb), e1, n, e2, rb, fg)


def _peer(x2, gain, w_q, sub_keys, u_tabs, v_tabs, layer, final_gain=None):
    hb, e1, n, e2, rb = _peer_select(x2, gain, w_q, sub_keys)
    return _peer_experts(x2, hb, e1, n, e2, rb, u_tabs, v_tabs, layer, final_gain)


def kernel(x, norm_mix, norm_ffn, hy_w_in, gm_v_norm, gm_w_s, gm_b_s, ret_norm, hy_w_out, mla_w_down, mla_q_norm,
           mla_w_q_up, mla_kv_norm, mla_w_kv_up, mla_w_out, peer_w_q, peer_sub_keys, peer_u, peer_v, final_norm):
    b, s, d = x.shape
    depth = norm_mix.shape[0]
    for layer in range(depth):
        j = layer // 2
        if layer % 2 == 0:
            x = _hy_mixer(x, norm_mix[layer], hy_w_in[j], gm_v_norm[j], gm_w_s[j], gm_b_s[j], ret_norm[j], hy_w_out[j])
            x2 = x.reshape(b * s, d)
        else:
            q, k, v = _mla_proj(x, norm_mix[layer], mla_w_down[j], mla_q_norm[j], mla_w_q_up[j], mla_kv_norm[j],
                                mla_w_kv_up[j])
            o = _flash(q, k, v)
            x2 = _out_proj(x.reshape(b * s, d), o.reshape(b * s, -1), mla_w_out[j])
        last = layer == depth - 1
        x2 = _peer(x2, norm_ffn[layer], peer_w_q[layer], peer_sub_keys[layer], peer_u, peer_v, layer,
                   final_norm if last else None)
        x = x2.reshape(b, s, d)
    return x
```

```python
import functools
import math

import jax
import jax.numpy as jnp
from jax import lax
from jax.experimental import pallas as pl
from jax.experimental.pallas import tpu as pltpu

F32 = jnp.float32
BF16 = jnp.bfloat16

LANES = 128
BF16_SUBLANES = 16
MXU_WIDTH = 256
V7X_VMEM_BYTES = 64 * 1024 * 1024
VMEM_LIMIT_BYTES = 56 * 1024 * 1024

NORM_EPS = 1e-6
ROPE_THETA = 10000.0

GM_GROUPS = 4
GM_DIM = 256
GM_CHUNK = 128
GM_WIDTH = GM_GROUPS * GM_DIM
RET_HEADS = 4
RET_QK_DIM = 128
RET_V_DIM = 256
RET_CHUNK = 128
RET_QK_WIDTH = RET_HEADS * RET_QK_DIM
RET_V_WIDTH = RET_HEADS * RET_V_DIM
HY_OUT_WIDTH = GM_WIDTH + RET_V_WIDTH
MLA_HEADS = 8
MLA_Q_RANK = 384
MLA_KV_RANK = 256
MLA_NOPE_DIM = 128
MLA_ROPE_DIM = 64
MLA_V_DIM = 128
MLA_QK_DIM = MLA_NOPE_DIM + MLA_ROPE_DIM
MLA_QK_PAD = 2 * LANES
PEER_HEADS = 8
PEER_N_KEYS = 128
PEER_HALF = 128
PEER_TOPK = 16

HY_TOKENS = 512
MLA_TOKENS = 512
ATT_BLOCK = 512
ATT_KV_BLOCK = 2048
SEL_TOKENS = 256
PEER_TOKENS = 1024
PEER_EXPERTS = 1024
PEER_KEY_GROUP = PEER_EXPERTS // PEER_N_KEYS
PEER_KEY_REUSE = 2
PEER_PIECE = MXU_WIDTH
PACK_ROWS = 1024


def _cparams(*semantics):
    return pltpu.CompilerParams(dimension_semantics=semantics, vmem_limit_bytes=VMEM_LIMIT_BYTES)


def _const_spec(shape):
    return pl.BlockSpec(shape, lambda *_: (0,) * len(shape))


def _rms(x, g):
    return x * lax.rsqrt(jnp.mean(x * x, axis=-1, keepdims=True) + NORM_EPS) * g


def _gelu(x):
    return 0.5 * x * (1.0 + lax.erf(x * (2.0 ** -0.5)))


def _dot(a, b):
    return jnp.dot(a, b, preferred_element_type=F32)


def _dot_nt(a, b):
    return lax.dot_general(a, b, (((1,), (1,)), ((), ())), preferred_element_type=F32)


def _dot_tn(a, b):
    return lax.dot_general(a, b, (((0,), (0,)), ((), ())), preferred_element_type=F32)


def _hy_body(x_ref, gmix_ref, win_ref, vnorm_ref, ws_ref, bs_ref, cos_ref, sin_ref, dmask_ref, qdec_ref, kdec_ref,
             cdec_ref, rnorm_ref, wout_ref, o_ref, state_ref, y_ref):
    @pl.when(pl.program_id(1) == 0)
    def _():
        state_ref[...] = jnp.zeros_like(state_ref)

    x = x_ref[0]
    h = _rms(x, gmix_ref[...]).astype(BF16)
    tokens = x.shape[0]
    chunks = tokens // GM_CHUNK

    u_all = _gelu(_dot(h, win_ref[:, 0:GM_WIDTH]))
    v_all = _gelu(_dot(h, win_ref[:, GM_WIDTH:2 * GM_WIDTH]))
    row = lax.broadcasted_iota(jnp.int32, (GM_CHUNK, GM_CHUNK), 0)
    col = lax.broadcasted_iota(jnp.int32, (GM_CHUNK, GM_CHUNK), 1)
    causal = row >= col
    for g in range(GM_GROUPS):
        cols = slice(g * GM_DIM, (g + 1) * GM_DIM)
        vg = _rms(v_all[:, cols], vnorm_ref[:, cols]).astype(BF16)
        wg = jnp.where(causal, ws_ref[g], 0.0).astype(BF16)
        for c in range(chunks):
            rows = slice(c * GM_CHUNK, (c + 1) * GM_CHUNK)
            mixed = _dot(wg, vg[rows]) + bs_ref[g]
            y_ref[rows, cols] = (u_all[rows, cols] * mixed).astype(BF16)

    base = 2 * GM_WIDTH
    q_all = _dot(h, win_ref[:, base:base + RET_QK_WIDTH])
    k_all = _dot(h, win_ref[:, base + RET_QK_WIDTH:base + 2 * RET_QK_WIDTH])
    base += 2 * RET_QK_WIDTH
    v_all = _dot(h, win_ref[:, base:base + RET_V_WIDTH])
    g_all = _dot(h, win_ref[:, base + RET_V_WIDTH:base + 2 * RET_V_WIDTH])
    cos = cos_ref[...]
    sin = sin_ref[...]
    for hd in range(RET_HEADS):
        qk_cols = slice(hd * RET_QK_DIM, (hd + 1) * RET_QK_DIM)
        v_cols = slice(hd * RET_V_DIM, (hd + 1) * RET_V_DIM)
        q = q_all[:, qk_cols]
        k = k_all[:, qk_cols]
        q = q * cos + pltpu.roll(q, RET_QK_DIM // 2, 1) * sin
        k = (k * cos + pltpu.roll(k, RET_QK_DIM // 2, 1) * sin) * (RET_QK_DIM ** -0.5)
        for c in range(chunks):
            rows = slice(c * RET_CHUNK, (c + 1) * RET_CHUNK)
            qc, kc, vc = q[rows], k[rows], v_all[rows, v_cols]
            scores = _dot_nt(qc.astype(BF16), kc.astype(BF16)) * dmask_ref[hd]
            intra = _dot(scores.astype(BF16), vc.astype(BF16))
            state = state_ref[hd]
            cross = _dot((qc * qdec_ref[hd]).astype(BF16), state.astype(BF16))
            state_ref[hd] = state * cdec_ref[hd] + _dot_tn(kc.astype(BF16), (vc * kdec_ref[hd]).astype(BF16))
            gate = g_all[rows, v_cols]
            yb = _rms(intra + cross, rnorm_ref[:, v_cols]) * (gate * jax.nn.sigmoid(gate))
            y_ref[rows, GM_WIDTH + hd * RET_V_DIM:GM_WIDTH + (hd + 1) * RET_V_DIM] = yb.astype(BF16)

    o_ref[0] = x + _dot(y_ref[...], wout_ref[...])


def _rope_angles(seq, half):
    inv = 1.0 / (ROPE_THETA ** (jnp.arange(half, dtype=F32) / half))
    ang = jnp.arange(seq, dtype=F32)[:, None] * inv[None, :]
    return jnp.cos(ang), jnp.sin(ang)


def _hy_mixer(x, gmix, w_in, v_norm, w_s, b_s, ret_norm, w_out):
    b, s, d = x.shape
    ts = HY_TOKENS
    cos, sin = _rope_angles(s, RET_QK_DIM // 2)
    cos2 = jnp.concatenate([cos, cos], axis=1)
    sin2 = jnp.concatenate([-sin, sin], axis=1)
    log_gamma = jnp.log(1.0 - 2.0 ** (-5.0 - jnp.arange(RET_HEADS, dtype=F32)))
    pos = jnp.arange(RET_CHUNK, dtype=F32)
    diff = pos[:, None] - pos[None, :]
    dmask = jnp.where(diff[None] >= 0, jnp.exp(diff[None] * log_gamma[:, None, None]), 0.0)
    qdec = jnp.exp((pos[None, :] + 1.0) * log_gamma[:, None])[:, :, None]
    kdec = jnp.exp((RET_CHUNK - 1.0 - pos[None, :]) * log_gamma[:, None])[:, :, None]
    cdec = jnp.exp(RET_CHUNK * log_gamma)[:, None, None]
    in_width = w_in.shape[1]
    return pl.pallas_call(
        _hy_body,
        out_shape=jax.ShapeDtypeStruct((b, s, d), F32),
        grid=(b, s // ts),
        in_specs=[
            pl.BlockSpec((1, ts, d), lambda i, j: (i, j, 0)),
            _const_spec((1, d)),
            _const_spec((d, in_width)),
            _const_spec((1, GM_WIDTH)),
            _const_spec((GM_GROUPS, GM_CHUNK, GM_CHUNK)),
            _const_spec((GM_GROUPS, GM_CHUNK, 1)),
            pl.BlockSpec((ts, RET_QK_DIM), lambda i, j: (j, 0)),
            pl.BlockSpec((ts, RET_QK_DIM), lambda i, j: (j, 0)),
            _const_spec((RET_HEADS, RET_CHUNK, RET_CHUNK)),
            _const_spec((RET_HEADS, RET_CHUNK, 1)),
            _const_spec((RET_HEADS, RET_CHUNK, 1)),
            _const_spec((RET_HEADS, 1, 1)),
            _const_spec((1, RET_V_WIDTH)),
            _const_spec((HY_OUT_WIDTH, d)),
        ],
        out_specs=pl.BlockSpec((1, ts, d), lambda i, j: (i, j, 0)),
        scratch_shapes=[
            pltpu.VMEM((RET_HEADS, RET_QK_DIM, RET_V_DIM), F32),
            pltpu.VMEM((ts, HY_OUT_WIDTH), BF16),
        ],
        compiler_params=_cparams("parallel", "arbitrary"),
        name="hy_mixer",
    )(x, gmix[None, :], w_in.astype(BF16), v_norm[None, :], w_s, b_s[:, :, None], cos2, sin2, dmask, qdec, kdec, cdec,
      ret_norm[None, :], w_out.astype(BF16))


def _mla_proj_body(x_ref, gmix_ref, wdown_ref, qn_ref, kvn_ref, wq_ref, wqrot_ref, wk_ref, wv_ref, cq_ref, sq_ref,
                   ck_ref, sk_ref, q_ref, k_ref, v_ref):
    h = _rms(x_ref[0], gmix_ref[...]).astype(BF16)
    down = _dot(h, wdown_ref[...])
    kv_end = MLA_Q_RANK + MLA_KV_RANK
    cq = _rms(down[:, :MLA_Q_RANK], qn_ref[...]).astype(BF16)
    ckv = _rms(down[:, MLA_Q_RANK:kv_end], kvn_ref[...]).astype(BF16)
    k_rope = (down[:, kv_end:kv_end + LANES] * ck_ref[...] + down[:, kv_end + LANES:kv_end + 2 * LANES] * sk_ref[...])
    k_rope = k_rope.astype(BF16)
    qa = _dot(cq, wq_ref[...])
    qb = _dot(cq, wqrot_ref[...])
    kn = _dot(ckv, wk_ref[...])
    scale = MLA_QK_DIM ** -0.5
    cq_tab = cq_ref[...] * scale
    sq_tab = sq_ref[...] * scale
    for hd in range(MLA_HEADS):
        seg = slice(hd * MLA_QK_PAD, (hd + 1) * MLA_QK_PAD)
        q_ref[0, :, seg] = (qa[:, seg] * cq_tab + qb[:, seg] * sq_tab).astype(BF16)
        k_ref[0, :, hd * MLA_QK_PAD:hd * MLA_QK_PAD + MLA_NOPE_DIM] = (
            kn[:, hd * MLA_NOPE_DIM:(hd + 1) * MLA_NOPE_DIM].astype(BF16))
        k_ref[0, :, hd * MLA_QK_PAD + MLA_NOPE_DIM:(hd + 1) * MLA_QK_PAD] = k_rope
    v_ref[0] = _dot(ckv, wv_ref[...]).astype(BF16)


def _rot_half_cols(w):
    half = w.shape[-1] // 2
    return jnp.concatenate([-w[..., half:], w[..., :half]], axis=-1)


def _mla_proj(x, gmix, w_down, q_norm, w_q_up, kv_norm, w_kv_up):
    b, s, d = x.shape
    tm = MLA_TOKENS
    kv_end = MLA_Q_RANK + MLA_KV_RANK
    pad = MLA_QK_PAD - MLA_QK_DIM
    w_kr = w_down[:, kv_end:]
    zk = jnp.zeros((d, LANES - MLA_ROPE_DIM), F32)
    w_down_x = jnp.concatenate([w_down[:, :kv_end], w_kr, zk, _rot_half_cols(w_kr), zk], axis=1).astype(BF16)
    wq = w_q_up.reshape(MLA_Q_RANK, MLA_HEADS, MLA_QK_DIM)
    zq = jnp.zeros((MLA_Q_RANK, MLA_HEADS, pad), F32)
    wq_a = jnp.concatenate([wq, zq], axis=2).reshape(MLA_Q_RANK, MLA_HEADS * MLA_QK_PAD).astype(BF16)
    wq_b = jnp.concatenate([jnp.zeros((MLA_Q_RANK, MLA_HEADS, MLA_NOPE_DIM), F32),
                            _rot_half_cols(wq[:, :, MLA_NOPE_DIM:]), zq], axis=2)
    wq_b = wq_b.reshape(MLA_Q_RANK, MLA_HEADS * MLA_QK_PAD).astype(BF16)
    wkv = w_kv_up.reshape(MLA_KV_RANK, MLA_HEADS, MLA_NOPE_DIM + MLA_V_DIM)
    wk = wkv[:, :, :MLA_NOPE_DIM].reshape(MLA_KV_RANK, MLA_HEADS * MLA_NOPE_DIM).astype(BF16)
    wv = wkv[:, :, MLA_NOPE_DIM:].reshape(MLA_KV_RANK, MLA_HEADS * MLA_V_DIM).astype(BF16)
    cos, sin = _rope_angles(s, MLA_ROPE_DIM // 2)
    zt = jnp.zeros((s, pad), F32)
    cq_tab = jnp.concatenate([jnp.ones((s, MLA_NOPE_DIM), F32), cos, cos, zt], axis=1)
    sq_tab = jnp.concatenate([jnp.zeros((s, MLA_NOPE_DIM), F32), sin, sin, zt], axis=1)
    ck_tab = jnp.concatenate([cos, cos, zt], axis=1)
    sk_tab = jnp.concatenate([sin, sin, zt], axis=1)
    qk_width = MLA_HEADS * MLA_QK_PAD
    v_width = MLA_HEADS * MLA_V_DIM
    tab = lambda w: pl.BlockSpec((tm, w), lambda i, j: (j, 0))
    out = lambda w: pl.BlockSpec((1, tm, w), lambda i, j: (i, j, 0))
    return pl.pallas_call(
        _mla_proj_body,
        out_shape=(jax.ShapeDtypeStruct((b, s, qk_width), BF16), jax.ShapeDtypeStruct((b, s, qk_width), BF16),
                   jax.ShapeDtypeStruct((b, s, v_width), BF16)),
        grid=(b, s // tm),
        in_specs=[
            out(d),
            _const_spec((1, d)),
            _const_spec(w_down_x.shape),
            _const_spec((1, MLA_Q_RANK)),
            _const_spec((1, MLA_KV_RANK)),
            _const_spec(wq_a.shape),
            _const_spec(wq_b.shape),
            _const_spec(wk.shape),
            _const_spec(wv.shape),
            tab(MLA_QK_PAD), tab(MLA_QK_PAD), tab(LANES), tab(LANES),
        ],
        out_specs=(out(qk_width), out(qk_width), out(v_width)),
        compiler_params=_cparams("parallel", "parallel"),
        name="mla_proj",
    )(x, gmix[None, :], w_down_x, q_norm[None, :], kv_norm[None, :], wq_a, wq_b, wk, wv, cq_tab, sq_tab, ck_tab, sk_tab)


def _flash_body(q_ref, k_ref, v_ref, o_ref):
    blk = q_ref.shape[1]
    kvb = ATT_KV_BLOCK
    i = pl.program_id(2)
    q = q_ref[0]
    full = (i * blk) // kvb

    def step(start, width, carry, masked):
        m, l, acc = carry
        s = _dot_nt(q, k_ref[0, pl.ds(start, width), :])
        if masked:
            row = lax.broadcasted_iota(jnp.int32, s.shape, 0) + i * blk
            col = lax.broadcasted_iota(jnp.int32, s.shape, 1) + start
            s = jnp.where(row >= col, s, -jnp.inf)
        m_new = jnp.maximum(m, jnp.max(s, axis=1, keepdims=True))
        p = jnp.exp(s - m_new)
        alpha = jnp.exp(m - m_new)
        l = alpha * l + jnp.sum(p, axis=1, keepdims=True)
        acc = alpha * acc + _dot(p.astype(BF16), v_ref[0, pl.ds(start, width), :])
        return m_new, l, acc

    init = (jnp.full((blk, 1), -jnp.inf, F32), jnp.zeros((blk, 1), F32), jnp.zeros((blk, v_ref.shape[2]), F32))
    carry = lax.fori_loop(0, full, lambda j, c: step(pl.multiple_of(j * kvb, kvb), kvb, c, False), init)
    for rest in range(1, kvb // blk + 1):
        @pl.when((i + 1) * blk - full * kvb == rest * blk)
        def _():
            _, l, acc = step(pl.multiple_of(full * kvb, kvb), rest * blk, carry, True)
            o_ref[0] = (acc / l).astype(BF16)


def _flash(q, k, v):
    b, s, _ = q.shape
    blk = ATT_BLOCK
    return pl.pallas_call(
        _flash_body,
        out_shape=jax.ShapeDtypeStruct((b, s, MLA_HEADS * MLA_V_DIM), BF16),
        grid=(b, MLA_HEADS, s // blk),
        in_specs=[
            pl.BlockSpec((1, blk, MLA_QK_PAD), lambda bi, h, i: (bi, i, h)),
            pl.BlockSpec((1, s, MLA_QK_PAD), lambda bi, h, i: (bi, 0, h)),
            pl.BlockSpec((1, s, MLA_V_DIM), lambda bi, h, i: (bi, 0, h)),
        ],
        out_specs=pl.BlockSpec((1, blk, MLA_V_DIM), lambda bi, h, i: (bi, i, h)),
        compiler_params=_cparams("parallel", "parallel", "arbitrary"),
        name="mla_flash",
    )(q, k, v)


def _out_proj_body(x_ref, o_ref, w_ref, y_ref):
    y_ref[...] = x_ref[...] + _dot(o_ref[...], w_ref[...])


def _out_proj(x2, o2, w_out):
    t, d = x2.shape
    tm = MLA_TOKENS
    return pl.pallas_call(
        _out_proj_body,
        out_shape=jax.ShapeDtypeStruct((t, d), F32),
        grid=(t // tm,),
        in_specs=[pl.BlockSpec((tm, d), lambda i: (i, 0)), pl.BlockSpec((tm, o2.shape[1]), lambda i: (i, 0)),
                  _const_spec(w_out.shape)],
        out_specs=pl.BlockSpec((tm, d), lambda i: (i, 0)),
        compiler_params=_cparams("parallel"),
        name="mla_out",
    )(x2, o2, w_out.astype(BF16))


def _sort_network(n):
    def merge(lo, hi, r):
        step = 2 * r
        if step < hi - lo:
            yield from merge(lo, hi, step)
            yield from merge(lo + r, hi, step)
            yield from ((i, i + r) for i in range(lo + r, hi - r, step))
        else:
            yield (lo, lo + r)

    def sort(lo, hi):
        if hi - lo >= 1:
            mid = lo + (hi - lo) // 2
            yield from sort(lo, mid)
            yield from sort(mid + 1, hi)
            yield from merge(lo, hi, 1)

    return tuple(sort(0, n - 1))


_SORT_TOPK = _sort_network(PEER_TOPK)


def _top_sorted(slabs):
    groups = []
    for g0 in range(0, len(slabs), PEER_TOPK):
        v = list(slabs[g0:g0 + PEER_TOPK])
        for i, j in _SORT_TOPK:
            v[i], v[j] = jnp.maximum(v[i], v[j]), jnp.minimum(v[i], v[j])
        groups.append(v)
    while len(groups) > 1:
        merged = []
        for a, b in zip(groups[0::2], groups[1::2]):
            v = [jnp.maximum(a[i], b[PEER_TOPK - 1 - i]) for i in range(PEER_TOPK)]
            d = PEER_TOPK // 2
            while d >= 1:
                for i in range(PEER_TOPK):
                    if not i & d:
                        v[i], v[i + d] = jnp.maximum(v[i], v[i + d]), jnp.minimum(v[i], v[i + d])
                d //= 2
            merged.append(v)
        groups = merged
    return groups[0]


def _merge_counts(s1, s2):
    n = [jnp.zeros_like(s1[0]) for _ in range(PEER_TOPK)]
    front = [s1[a] + s2[0] for a in range(PEER_TOPK)]
    for _ in range(PEER_TOPK):
        m = functools.reduce(jnp.maximum, front)
        first = jnp.full_like(m, float(PEER_TOPK))
        for a in reversed(range(PEER_TOPK)):
            first = jnp.where(front[a] == m, float(a), first)
        hits = [first == float(a) for a in range(PEER_TOPK)]
        n_hit = jnp.zeros_like(m)
        for a in range(PEER_TOPK):
            n_hit = jnp.where(hits[a], n[a], n_hit)
        nxt = jnp.full_like(m, -jnp.inf)
        for b in range(1, PEER_TOPK):
            nxt = jnp.where(n_hit == float(b - 1), s2[b], nxt)
        for a in range(PEER_TOPK):
            n[a] = jnp.where(hits[a], n[a] + 1.0, n[a])
            front[a] = jnp.where(hits[a], s1[a] + nxt, front[a])
    return n


_RANK_CHECKSUM = float(sum(range(PEER_TOPK)) + (PEER_N_KEYS - PEER_TOPK) * PEER_TOPK)


def _sel_tile(lt, s_ref, r_ref, t_ref, e1_ref, nk_ref, e2_ref, rb_ref):
    keys = range(PEER_N_KEYS)
    top = [_top_sorted([s_ref[p, lt, k] for k in keys]) for p in range(2)]
    n = _merge_counts(top[0], top[1])

    e1s = [jnp.exp(v - top[0][0]) for v in top[0]]
    e2s = [jnp.exp(v - top[1][0]) for v in top[1]]
    z = jnp.zeros_like(e1s[0])
    for b in range(PEER_TOPK):
        row = jnp.zeros_like(z)
        for a in range(PEER_TOPK):
            row = row + jnp.where(n[a] > float(b), e1s[a], 0.0)
        z = z + row * e2s[b]
    inv_z = 1.0 / z

    def rank(s, sorted_vals):
        r = jnp.full_like(s, float(PEER_TOPK))
        for a in reversed(range(PEER_TOPK)):
            r = jnp.where(s >= sorted_vals[a], float(a), r)
        return r

    def shifted(r, seen):
        seen = list(seen)
        back = jnp.zeros_like(r)
        for a in range(PEER_TOPK):
            hit = r == float(a)
            back = jnp.where(hit, seen[a], back)
            seen[a] = jnp.where(hit, seen[a] + 1.0, seen[a])
        return jnp.minimum(r + back, float(PEER_TOPK)), tuple(seen)

    none_seen = tuple(jnp.zeros_like(z) for _ in range(PEER_TOPK))

    def any_lane(mask):
        return jnp.max(jnp.where(mask, 1.0, 0.0)) > 0.0

    total = jnp.zeros_like(z)
    for k in keys:
        r = rank(s_ref[1, lt, k], top[1])
        r_ref[k] = r
        total = total + r
        t_ref[k] = jnp.exp(s_ref[1, lt, k] - top[1][0])

    @pl.when(any_lane(total != _RANK_CHECKSUM))
    def _():
        def fix(k, seen):
            r_ref[k], seen = shifted(r_ref[k], seen)
            return seen

        lax.fori_loop(0, PEER_N_KEYS, fix, none_seen)

    count = jnp.zeros_like(z)
    for k in keys:
        s = s_ref[0, lt, k]
        nk = jnp.zeros_like(s)
        for a in reversed(range(PEER_TOPK)):
            nk = jnp.where(s >= top[0][a], n[a], nk)
        nk_ref[lt, k] = nk
        e1_ref[lt, k] = jnp.exp(s - top[0][0]) * inv_z
        count = count + jnp.where(s >= top[0][PEER_TOPK - 1], 1.0, 0.0)
    tied = count != float(PEER_TOPK)
    for a in range(PEER_TOPK - 1):
        tied = tied | (top[0][a] == top[0][a + 1])

    @pl.when(any_lane(tied))
    def _():
        def fix(k, seen):
            r, seen = shifted(rank(s_ref[0, lt, k], top[0]), seen)
            nk = jnp.zeros_like(r)
            for a in range(PEER_TOPK):
                nk = jnp.where(r == float(a), n[a], nk)
            nk_ref[lt, k] = nk
            return seen

        lax.fori_loop(0, PEER_N_KEYS, fix, none_seen)

    e2p = pltpu.bitcast(jnp.swapaxes(t_ref[...], 0, 1).astype(BF16), F32)
    rbp = pltpu.bitcast(jnp.swapaxes(r_ref[...], 0, 1).astype(BF16), F32)
    for h in range(PEER_HEADS):
        e2_ref[h, lt] = e2p[h]
        rb_ref[h, lt] = rbp[h]


def _peer_sel_body(x_ref, g_ref, wq_ref, wkey_ref, hb_ref, e1_ref, nk_ref, e2_ref, rb_ref, s_ref, r_ref, t_ref):
    hb = _rms(x_ref[...], g_ref[...]).astype(BF16)
    hb_ref[...] = pltpu.bitcast(hb, F32)
    qry = _dot(hb, wq_ref[...]).astype(BF16)
    tiles = x_ref.shape[0] // LANES
    width = PEER_HEADS * PEER_HALF
    for p in range(2):
        sc = _dot_nt(wkey_ref[p], qry[:, p * width:(p + 1) * width])
        sc = sc.reshape(PEER_N_KEYS, PEER_HEADS, tiles * LANES)
        for lt in range(tiles):
            s_ref[p, lt] = sc[:, :, lt * LANES:(lt + 1) * LANES]

    def tile(lt, _):
        _sel_tile(lt, s_ref, r_ref, t_ref, e1_ref, nk_ref, e2_ref, rb_ref)
        return 0

    lax.fori_loop(0, tiles, tile, 0)


def _peer_select(x2, gain, w_q, sub_keys):
    t, d = x2.shape
    tb = SEL_TOKENS
    tiles = tb // LANES
    width = PEER_HEADS * PEER_HALF
    wq = w_q.reshape(d, PEER_HEADS, 2, PEER_HALF).transpose(0, 2, 1, 3).reshape(d, 2 * width).astype(BF16)
    eye = jnp.eye(PEER_HEADS, dtype=F32)
    wkey = (sub_keys[:, :, None, None, :] * eye[None, None, :, :, None]).reshape(2, PEER_N_KEYS * PEER_HEADS, width)
    key_shape = jax.ShapeDtypeStruct((t // LANES, PEER_N_KEYS, PEER_HEADS, LANES), F32)
    key_spec = pl.BlockSpec((tiles, PEER_N_KEYS, PEER_HEADS, LANES), lambda i: (i, 0, 0, 0))
    pair_shape = jax.ShapeDtypeStruct((PEER_HEADS, t // LANES, PEER_N_KEYS // 2, LANES), F32)
    pair_spec = pl.BlockSpec((PEER_HEADS, tiles, PEER_N_KEYS // 2, LANES), lambda i: (0, i, 0, 0))
    slab = (PEER_N_KEYS, PEER_HEADS, LANES)
    return pl.pallas_call(
        _peer_sel_body,
        out_shape=(jax.ShapeDtypeStruct((t // 2, d), F32), key_shape, key_shape, pair_shape, pair_shape),
        grid=(t // tb,),
        in_specs=[pl.BlockSpec((tb, d), lambda i: (i, 0)), _const_spec((1, d)), _const_spec(wq.shape),
                  _const_spec(wkey.shape)],
        out_specs=(pl.BlockSpec((tb // 2, d), lambda i: (i, 0)), key_spec, key_spec, pair_spec, pair_spec),
        scratch_shapes=[pltpu.VMEM((2, tiles) + slab, F32), pltpu.VMEM(slab, F32), pltpu.VMEM(slab, F32)],
        compiler_params=_cparams("parallel"),
        name="peer_select",
    )(x2, gain[None, :], wq, wkey.astype(BF16))


def _key_rows_bf16(row):
    packed = jnp.broadcast_to(row, (BF16_SUBLANES, row.shape[1])).astype(BF16)
    return jnp.concatenate([packed] * (PEER_N_KEYS // BF16_SUBLANES), axis=0)


def _peer_main_body(final_norm, x_ref, hb_ref, u_ref, vt_ref, e1_ref, n_ref, e2_ref, rb_ref, fg_ref, o_ref, acc_ref,
                    a_ref):
    s = pl.program_id(1)

    @pl.when(s == 0)
    def _():
        acc_ref[...] = jnp.zeros_like(acc_ref)
        a_ref[...] = jnp.zeros_like(a_ref)

    piece_tiles = PEER_PIECE // LANES
    for p in range(acc_ref.shape[0]):
        g_tiles = []
        for tb in range(p * piece_tiles, (p + 1) * piece_tiles):
            g_keys = []
            for k0 in range(0, PEER_KEY_GROUP, PEER_KEY_REUSE):
                ks = range(k0, k0 + PEER_KEY_REUSE)
                w = {}
                for h in range(PEER_HEADS):
                    rb_t = pltpu.bitcast(rb_ref[h, tb], BF16)
                    e2_t = pltpu.bitcast(e2_ref[h, tb], BF16)
                    for k in ks:
                        n_b = _key_rows_bf16(n_ref[tb, k, h:h + 1, :])
                        e1_b = _key_rows_bf16(e1_ref[tb, k, h:h + 1, :])
                        term = jnp.where(rb_t < n_b, e2_t * e1_b, 0.0)
                        w[k] = term if h == 0 else w[k] + term
                for k in ks:
                    g_keys.append(w[k] * _gelu(a_ref[tb, k * PEER_N_KEYS:(k + 1) * PEER_N_KEYS, :].astype(BF16)))
            g_tiles.append(jnp.concatenate(g_keys, axis=0))
        acc_ref[p] += _dot(pltpu.bitcast(vt_ref[0], BF16), jnp.concatenate(g_tiles, axis=1))
        hb_piece = pltpu.bitcast(hb_ref[p * (PEER_PIECE // 2):(p + 1) * (PEER_PIECE // 2), :], BF16)
        a_piece = _dot_nt(pltpu.bitcast(u_ref[...], BF16), hb_piece)
        for t in range(piece_tiles):
            a_ref[p * piece_tiles + t] = a_piece[:, t * LANES:(t + 1) * LANES]

    @pl.when(s == pl.num_programs(1) - 1)
    def _():
        for p in range(acc_ref.shape[0]):
            rows = slice(p * PEER_PIECE, (p + 1) * PEER_PIECE)
            y = x_ref[rows, :] + acc_ref[p].T
            if final_norm:
                y = _rms(y, fg_ref[...])
            o_ref[rows, :] = y


def _pack_rows_body(x_ref, o_ref):
    o_ref[...] = pltpu.bitcast(x_ref[0].astype(BF16), F32)


def _pack_rows(tables, layer):
    _, rows, d = tables.shape
    blk = PACK_ROWS
    return pl.pallas_call(
        _pack_rows_body,
        out_shape=jax.ShapeDtypeStruct((rows // 2, d), F32),
        grid=(rows // blk,),
        in_specs=[pl.BlockSpec((1, blk, d), lambda i: (layer, i, 0))],
        out_specs=pl.BlockSpec((blk // 2, d), lambda i: (i, 0)),
        compiler_params=_cparams("parallel"),
        name="pack_rows",
    )(tables)


def _pack_transposed_body(x_ref, o_ref):
    o_ref[0] = pltpu.bitcast(x_ref[0].T.astype(BF16), F32)


def _pack_transposed_chunks(tables, layer, chunk):
    _, rows, d = tables.shape
    return pl.pallas_call(
        _pack_transposed_body,
        out_shape=jax.ShapeDtypeStruct((rows // chunk, d // 2, chunk), F32),
        grid=(rows // chunk,),
        in_specs=[pl.BlockSpec((1, chunk, d), lambda i: (layer, i, 0))],
        out_specs=pl.BlockSpec((1, d // 2, chunk), lambda i: (i, 0, 0)),
        compiler_params=_cparams("parallel"),
        name="pack_transposed",
    )(tables)


def _peer_experts(x2, hb, e1, n, e2, rb, u_tabs, v_tabs, layer, final_gain=None):
    t, d = x2.shape
    n_experts = u_tabs.shape[1]
    tb, eb = PEER_TOKENS, PEER_EXPERTS
    chunks = n_experts // eb
    final_norm = final_gain is not None
    fg = (final_gain if final_norm else jnp.ones((d,), F32))[None, :]
    once = dict(pipeline_mode=pl.Buffered(1))
    tok_spec = pl.BlockSpec((tb, d), lambda i, s: (i, 0), **once)
    hb_spec = pl.BlockSpec((tb // 2, d), lambda i, s: (i, 0), **once)
    sel_spec = pl.BlockSpec((PEER_HEADS, tb // LANES, PEER_N_KEYS // 2, LANES), lambda i, s: (0, i, 0, 0), **once)
    key_spec = pl.BlockSpec((tb // LANES, PEER_KEY_GROUP, PEER_HEADS, LANES),
                            lambda i, s: (i, jnp.clip(s - 1, 0, chunks - 1), 0, 0))
    return pl.pallas_call(
        functools.partial(_peer_main_body, final_norm),
        out_shape=jax.ShapeDtypeStruct((t, d), F32),
        grid=(t // tb, chunks + 1),
        in_specs=[
            tok_spec,
            hb_spec,
            pl.BlockSpec((eb // 2, d), lambda i, s: (jnp.minimum(s, chunks - 1), 0)),
            pl.BlockSpec((1, d // 2, eb), lambda i, s: (jnp.maximum(s - 1, 0), 0, 0)),
            key_spec, key_spec, sel_spec, sel_spec,
            _const_spec((1, d)),
        ],
        out_specs=pl.BlockSpec((tb, d), lambda i, s: (i, 0)),
        scratch_shapes=[pltpu.VMEM((tb // PEER_PIECE, d, PEER_PIECE), F32), pltpu.VMEM((tb // LANES, eb, LANES), F32)],
        compiler_params=_cparams("parallel", "arbitrary"),
        name="peer_experts",
    )(x2, hb, _pack_rows(u_tabs, layer), _pack_transposed_chunks(v_tabs, layer, eb), e1, n, e2, rb, fg)


def _peer(x2, gain, w_q, sub_keys, u_tabs, v_tabs, layer, final_gain=None):
    hb, e1, n, e2, rb = _peer_select(x2, gain, w_q, sub_keys)
    return _peer_experts(x2, hb, e1, n, e2, rb, u_tabs, v_tabs, layer, final_gain)


def kernel(x, norm_mix, norm_ffn, hy_w_in, gm_v_norm, gm_w_s, gm_b_s, ret_norm, hy_w_out, mla_w_down, mla_q_norm,
           mla_w_q_up, mla_kv_norm, mla_w_kv_up, mla_w_out, peer_w_q, peer_sub_keys, peer_u, peer_v, final_norm):
    b, s, d = x.shape
    depth = norm_mix.shape[0]
    for layer in range(depth):
        j = layer // 2
        if layer % 2 == 0:
            x = _hy_mixer(x, norm_mix[layer], hy_w_in[j], gm_v_norm[j], gm_w_s[j], gm_b_s[j], ret_norm[j], hy_w_out[j])
            x2 = x.reshape(b * s, d)
        else:
            q, k, v = _mla_proj(x, norm_mix[layer], mla_w_down[j], mla_q_norm[j], mla_w_q_up[j], mla_kv_norm[j],
                                mla_w_kv_up[j])
            o = _flash(q, k, v)
            x2 = _out_proj(x.reshape(b * s, d), o.reshape(b * s, -1), mla_w_out[j])
        last = layer == depth - 1
        x2 = _peer(x2, norm_ffn[layer], peer_w_q[layer], peer_sub_keys[layer], peer_u, peer_v, layer,
                   final_norm if last else None)
        x = x2.reshape(b, s, d)
    return x
```

```python
import functools
import math

import jax
import jax.numpy as jnp
from jax import lax
from jax.experimental import pallas as pl
from jax.experimental.pallas import tpu as pltpu

F32 = jnp.float32
BF16 = jnp.bfloat16

LANES = 128
BF16_SUBLANES = 16
MXU_WIDTH = 256
V7X_VMEM_BYTES = 64 * 1024 * 1024
VMEM_LIMIT_BYTES = 56 * 1024 * 1024

NORM_EPS = 1e-6
ROPE_THETA = 10000.0

GM_GROUPS = 4
GM_DIM = 256
GM_CHUNK = 128
GM_WIDTH = GM_GROUPS * GM_DIM
RET_HEADS = 4
RET_QK_DIM = 128
RET_V_DIM = 256
RET_CHUNK = 128
RET_QK_WIDTH = RET_HEADS * RET_QK_DIM
RET_V_WIDTH = RET_HEADS * RET_V_DIM
HY_OUT_WIDTH = GM_WIDTH + RET_V_WIDTH
MLA_HEADS = 8
MLA_Q_RANK = 384
MLA_KV_RANK = 256
MLA_NOPE_DIM = 128
MLA_ROPE_DIM = 64
MLA_V_DIM = 128
MLA_QK_DIM = MLA_NOPE_DIM + MLA_ROPE_DIM
MLA_QK_PAD = 2 * LANES
PEER_HEADS = 8
PEER_N_KEYS = 128
PEER_HALF = 128
PEER_TOPK = 16

HY_TOKENS = 512
MLA_TOKENS = 512
ATT_BLOCK = 512
ATT_KV_BLOCK = 2048
ATT_SUB_BLOCK = 1024
SEL_TOKENS = 256
PEER_TOKENS = 1024
PEER_EXPERTS = 1024
PEER_KEY_GROUP = PEER_EXPERTS // PEER_N_KEYS
PEER_KEY_REUSE = 2
PEER_PIECE = MXU_WIDTH
PACK_ROWS = 1024


def _cparams(*semantics):
    return pltpu.CompilerParams(dimension_semantics=semantics, vmem_limit_bytes=VMEM_LIMIT_BYTES)


def _const_spec(shape):
    return pl.BlockSpec(shape, lambda *_: (0,) * len(shape))


def _rms(x, g):
    return x * lax.rsqrt(jnp.mean(x * x, axis=-1, keepdims=True) + NORM_EPS) * g


def _gelu(x):
    return 0.5 * x * (1.0 + lax.erf(x * (2.0 ** -0.5)))


def _dot(a, b):
    return jnp.dot(a, b, preferred_element_type=F32)


def _dot_nt(a, b):
    return lax.dot_general(a, b, (((1,), (1,)), ((), ())), preferred_element_type=F32)


def _dot_tn(a, b):
    return lax.dot_general(a, b, (((0,), (0,)), ((), ())), preferred_element_type=F32)


def _hy_body(x_ref, gmix_ref, win_ref, vnorm_ref, ws_ref, bs_ref, cos_ref, sin_ref, dmask_ref, qdec_ref, kdec_ref,
             cdec_ref, rnorm_ref, wout_ref, o_ref, state_ref, y_ref):
    @pl.when(pl.program_id(1) == 0)
    def _():
        state_ref[...] = jnp.zeros_like(state_ref)

    x = x_ref[0]
    h = _rms(x, gmix_ref[...]).astype(BF16)
    tokens = x.shape[0]
    chunks = tokens // GM_CHUNK

    u_all = _gelu(_dot(h, win_ref[:, 0:GM_WIDTH]))
    v_all = _gelu(_dot(h, win_ref[:, GM_WIDTH:2 * GM_WIDTH]))
    row = lax.broadcasted_iota(jnp.int32, (GM_CHUNK, GM_CHUNK), 0)
    col = lax.broadcasted_iota(jnp.int32, (GM_CHUNK, GM_CHUNK), 1)
    causal = row >= col
    for g in range(GM_GROUPS):
        cols = slice(g * GM_DIM, (g + 1) * GM_DIM)
        vg = _rms(v_all[:, cols], vnorm_ref[:, cols]).astype(BF16)
        wg = jnp.where(causal, ws_ref[g], 0.0).astype(BF16)
        for c in range(chunks):
            rows = slice(c * GM_CHUNK, (c + 1) * GM_CHUNK)
            mixed = _dot(wg, vg[rows]) + bs_ref[g]
            y_ref[rows, cols] = (u_all[rows, cols] * mixed).astype(BF16)

    base = 2 * GM_WIDTH
    q_all = _dot(h, win_ref[:, base:base + RET_QK_WIDTH])
    k_all = _dot(h, win_ref[:, base + RET_QK_WIDTH:base + 2 * RET_QK_WIDTH])
    base += 2 * RET_QK_WIDTH
    v_all = _dot(h, win_ref[:, base:base + RET_V_WIDTH])
    g_all = _dot(h, win_ref[:, base + RET_V_WIDTH:base + 2 * RET_V_WIDTH])
    cos = cos_ref[...]
    sin = sin_ref[...]
    for hd in range(RET_HEADS):
        qk_cols = slice(hd * RET_QK_DIM, (hd + 1) * RET_QK_DIM)
        v_cols = slice(hd * RET_V_DIM, (hd + 1) * RET_V_DIM)
        q = q_all[:, qk_cols]
        k = k_all[:, qk_cols]
        q = q * cos + pltpu.roll(q, RET_QK_DIM // 2, 1) * sin
        k = (k * cos + pltpu.roll(k, RET_QK_DIM // 2, 1) * sin) * (RET_QK_DIM ** -0.5)
        for c in range(chunks):
            rows = slice(c * RET_CHUNK, (c + 1) * RET_CHUNK)
            qc, kc, vc = q[rows], k[rows], v_all[rows, v_cols]
            scores = _dot_nt(qc.astype(BF16), kc.astype(BF16)) * dmask_ref[hd]
            intra = _dot(scores.astype(BF16), vc.astype(BF16))
            state = state_ref[hd]
            cross = _dot((qc * qdec_ref[hd]).astype(BF16), state.astype(BF16))
            state_ref[hd] = state * cdec_ref[hd] + _dot_tn(kc.astype(BF16), (vc * kdec_ref[hd]).astype(BF16))
            gate = g_all[rows, v_cols]
            yb = _rms(intra + cross, rnorm_ref[:, v_cols]) * (gate * jax.nn.sigmoid(gate))
            y_ref[rows, GM_WIDTH + hd * RET_V_DIM:GM_WIDTH + (hd + 1) * RET_V_DIM] = yb.astype(BF16)

    o_ref[0] = x + _dot(y_ref[...], wout_ref[...])


def _rope_angles(seq, half):
    inv = 1.0 / (ROPE_THETA ** (jnp.arange(half, dtype=F32) / half))
    ang = jnp.arange(seq, dtype=F32)[:, None] * inv[None, :]
    return jnp.cos(ang), jnp.sin(ang)


def _hy_mixer(x, gmix, w_in, v_norm, w_s, b_s, ret_norm, w_out):
    b, s, d = x.shape
    ts = HY_TOKENS
    cos, sin = _rope_angles(s, RET_QK_DIM // 2)
    cos2 = jnp.concatenate([cos, cos], axis=1)
    sin2 = jnp.concatenate([-sin, sin], axis=1)
    log_gamma = jnp.log(1.0 - 2.0 ** (-5.0 - jnp.arange(RET_HEADS, dtype=F32)))
    pos = jnp.arange(RET_CHUNK, dtype=F32)
    diff = pos[:, None] - pos[None, :]
    dmask = jnp.where(diff[None] >= 0, jnp.exp(diff[None] * log_gamma[:, None, None]), 0.0)
    qdec = jnp.exp((pos[None, :] + 1.0) * log_gamma[:, None])[:, :, None]
    kdec = jnp.exp((RET_CHUNK - 1.0 - pos[None, :]) * log_gamma[:, None])[:, :, None]
    cdec = jnp.exp(RET_CHUNK * log_gamma)[:, None, None]
    in_width = w_in.shape[1]
    return pl.pallas_call(
        _hy_body,
        out_shape=jax.ShapeDtypeStruct((b, s, d), F32),
        grid=(b, s // ts),
        in_specs=[
            pl.BlockSpec((1, ts, d), lambda i, j: (i, j, 0)),
            _const_spec((1, d)),
            _const_spec((d, in_width)),
            _const_spec((1, GM_WIDTH)),
            _const_spec((GM_GROUPS, GM_CHUNK, GM_CHUNK)),
            _const_spec((GM_GROUPS, GM_CHUNK, 1)),
            pl.BlockSpec((ts, RET_QK_DIM), lambda i, j: (j, 0)),
            pl.BlockSpec((ts, RET_QK_DIM), lambda i, j: (j, 0)),
            _const_spec((RET_HEADS, RET_CHUNK, RET_CHUNK)),
            _const_spec((RET_HEADS, RET_CHUNK, 1)),
            _const_spec((RET_HEADS, RET_CHUNK, 1)),
            _const_spec((RET_HEADS, 1, 1)),
            _const_spec((1, RET_V_WIDTH)),
            _const_spec((HY_OUT_WIDTH, d)),
        ],
        out_specs=pl.BlockSpec((1, ts, d), lambda i, j: (i, j, 0)),
        scratch_shapes=[
            pltpu.VMEM((RET_HEADS, RET_QK_DIM, RET_V_DIM), F32),
            pltpu.VMEM((ts, HY_OUT_WIDTH), BF16),
        ],
        compiler_params=_cparams("parallel", "arbitrary"),
        name="hy_mixer",
    )(x, gmix[None, :], w_in.astype(BF16), v_norm[None, :], w_s, b_s[:, :, None], cos2, sin2, dmask, qdec, kdec, cdec,
      ret_norm[None, :], w_out.astype(BF16))


def _mla_proj_body(x_ref, gmix_ref, wdown_ref, qn_ref, kvn_ref, wq_ref, wqrot_ref, wk_ref, wv_ref, cq_ref, sq_ref,
                   ck_ref, sk_ref, q_ref, k_ref, v_ref):
    h = _rms(x_ref[0], gmix_ref[...]).astype(BF16)
    down = _dot(h, wdown_ref[...])
    kv_end = MLA_Q_RANK + MLA_KV_RANK
    cq = _rms(down[:, :MLA_Q_RANK], qn_ref[...]).astype(BF16)
    ckv = _rms(down[:, MLA_Q_RANK:kv_end], kvn_ref[...]).astype(BF16)
    k_rope = (down[:, kv_end:kv_end + LANES] * ck_ref[...] + down[:, kv_end + LANES:kv_end + 2 * LANES] * sk_ref[...])
    k_rope = k_rope.astype(BF16)
    qa = _dot(cq, wq_ref[...])
    qb = _dot(cq, wqrot_ref[...])
    kn = _dot(ckv, wk_ref[...])
    scale = MLA_QK_DIM ** -0.5
    cq_tab = cq_ref[...] * scale
    sq_tab = sq_ref[...] * scale
    for hd in range(MLA_HEADS):
        seg = slice(hd * MLA_QK_PAD, (hd + 1) * MLA_QK_PAD)
        q_ref[0, :, seg] = (qa[:, seg] * cq_tab + qb[:, seg] * sq_tab).astype(BF16)
        k_ref[0, :, hd * MLA_QK_PAD:hd * MLA_QK_PAD + MLA_NOPE_DIM] = (
            kn[:, hd * MLA_NOPE_DIM:(hd + 1) * MLA_NOPE_DIM].astype(BF16))
        k_ref[0, :, hd * MLA_QK_PAD + MLA_NOPE_DIM:(hd + 1) * MLA_QK_PAD] = k_rope
    v_ref[0] = _dot(ckv, wv_ref[...]).astype(BF16)


def _rot_half_cols(w):
    half = w.shape[-1] // 2
    return jnp.concatenate([-w[..., half:], w[..., :half]], axis=-1)


def _mla_proj(x, gmix, w_down, q_norm, w_q_up, kv_norm, w_kv_up):
    b, s, d = x.shape
    tm = MLA_TOKENS
    kv_end = MLA_Q_RANK + MLA_KV_RANK
    pad = MLA_QK_PAD - MLA_QK_DIM
    w_kr = w_down[:, kv_end:]
    zk = jnp.zeros((d, LANES - MLA_ROPE_DIM), F32)
    w_down_x = jnp.concatenate([w_down[:, :kv_end], w_kr, zk, _rot_half_cols(w_kr), zk], axis=1).astype(BF16)
    wq = w_q_up.reshape(MLA_Q_RANK, MLA_HEADS, MLA_QK_DIM)
    zq = jnp.zeros((MLA_Q_RANK, MLA_HEADS, pad), F32)
    wq_a = jnp.concatenate([wq, zq], axis=2).reshape(MLA_Q_RANK, MLA_HEADS * MLA_QK_PAD).astype(BF16)
    wq_b = jnp.concatenate([jnp.zeros((MLA_Q_RANK, MLA_HEADS, MLA_NOPE_DIM), F32),
                            _rot_half_cols(wq[:, :, MLA_NOPE_DIM:]), zq], axis=2)
    wq_b = wq_b.reshape(MLA_Q_RANK, MLA_HEADS * MLA_QK_PAD).astype(BF16)
    wkv = w_kv_up.reshape(MLA_KV_RANK, MLA_HEADS, MLA_NOPE_DIM + MLA_V_DIM)
    wk = wkv[:, :, :MLA_NOPE_DIM].reshape(MLA_KV_RANK, MLA_HEADS * MLA_NOPE_DIM).astype(BF16)
    wv = wkv[:, :, MLA_NOPE_DIM:].reshape(MLA_KV_RANK, MLA_HEADS * MLA_V_DIM).astype(BF16)
    cos, sin = _rope_angles(s, MLA_ROPE_DIM // 2)
    zt = jnp.zeros((s, pad), F32)
    cq_tab = jnp.concatenate([jnp.ones((s, MLA_NOPE_DIM), F32), cos, cos, zt], axis=1)
    sq_tab = jnp.concatenate([jnp.zeros((s, MLA_NOPE_DIM), F32), sin, sin, zt], axis=1)
    ck_tab = jnp.concatenate([cos, cos, zt], axis=1)
    sk_tab = jnp.concatenate([sin, sin, zt], axis=1)
    qk_width = MLA_HEADS * MLA_QK_PAD
    v_width = MLA_HEADS * MLA_V_DIM
    tab = lambda w: pl.BlockSpec((tm, w), lambda i, j: (j, 0))
    out = lambda w: pl.BlockSpec((1, tm, w), lambda i, j: (i, j, 0))
    return pl.pallas_call(
        _mla_proj_body,
        out_shape=(jax.ShapeDtypeStruct((b, s, qk_width), BF16), jax.ShapeDtypeStruct((b, s, qk_width), BF16),
                   jax.ShapeDtypeStruct((b, s, v_width), BF16)),
        grid=(b, s // tm),
        in_specs=[
            out(d),
            _const_spec((1, d)),
            _const_spec(w_down_x.shape),
            _const_spec((1, MLA_Q_RANK)),
            _const_spec((1, MLA_KV_RANK)),
            _const_spec(wq_a.shape),
            _const_spec(wq_b.shape),
            _const_spec(wk.shape),
            _const_spec(wv.shape),
            tab(MLA_QK_PAD), tab(MLA_QK_PAD), tab(LANES), tab(LANES),
        ],
        out_specs=(out(qk_width), out(qk_width), out(v_width)),
        compiler_params=_cparams("parallel", "parallel"),
        name="mla_proj",
    )(x, gmix[None, :], w_down_x, q_norm[None, :], kv_norm[None, :], wq_a, wq_b, wk, wv, cq_tab, sq_tab, ck_tab, sk_tab)


def _flash_body(q_ref, k_ref, v_ref, o_ref):
    blk = q_ref.shape[1]
    kvb = ATT_KV_BLOCK
    i = pl.program_id(2)
    q = q_ref[0]
    full = (i * blk) // kvb

    def step(start, width, carry, masked):
        m, l, acc = carry
        for off in range(0, width, ATT_SUB_BLOCK):
            w = min(ATT_SUB_BLOCK, width - off)
            s = _dot_nt(q, k_ref[0, pl.ds(start + off, w), :])
            if masked:
                row = lax.broadcasted_iota(jnp.int32, s.shape, 0) + i * blk
                col = lax.broadcasted_iota(jnp.int32, s.shape, 1) + (start + off)
                s = jnp.where(row >= col, s, -jnp.inf)
            m_new = jnp.maximum(m, jnp.max(s, axis=1, keepdims=True))
            p = jnp.exp(s - m_new)
            alpha = jnp.exp(m - m_new)
            l = alpha * l + jnp.sum(p, axis=1, keepdims=True)
            acc = alpha * acc + _dot(p.astype(BF16), v_ref[0, pl.ds(start + off, w), :])
            m = m_new
        return m, l, acc

    init = (jnp.full((blk, 1), -jnp.inf, F32), jnp.zeros((blk, 1), F32), jnp.zeros((blk, v_ref.shape[2]), F32))
    carry = lax.fori_loop(0, full, lambda j, c: step(pl.multiple_of(j * kvb, kvb), kvb, c, False), init)
    for rest in range(1, kvb // blk + 1):
        @pl.when((i + 1) * blk - full * kvb == rest * blk)
        def _():
            _, l, acc = step(pl.multiple_of(full * kvb, kvb), rest * blk, carry, True)
            o_ref[0] = (acc / l).astype(BF16)


def _flash(q, k, v):
    b, s, _ = q.shape
    blk = ATT_BLOCK
    return pl.pallas_call(
        _flash_body,
        out_shape=jax.ShapeDtypeStruct((b, s, MLA_HEADS * MLA_V_DIM), BF16),
        grid=(b, MLA_HEADS, s // blk),
        in_specs=[
            pl.BlockSpec((1, blk, MLA_QK_PAD), lambda bi, h, i: (bi, i, h)),
            pl.BlockSpec((1, s, MLA_QK_PAD), lambda bi, h, i: (bi, 0, h)),
            pl.BlockSpec((1, s, MLA_V_DIM), lambda bi, h, i: (bi, 0, h)),
        ],
        out_specs=pl.BlockSpec((1, blk, MLA_V_DIM), lambda bi, h, i: (bi, i, h)),
        compiler_params=_cparams("parallel", "parallel", "arbitrary"),
        name="mla_flash",
    )(q, k, v)


def _out_proj_body(x_ref, o_ref, w_ref, y_ref):
    y_ref[...] = x_ref[...] + _dot(o_ref[...], w_ref[...])


def _out_proj(x2, o2, w_out):
    t, d = x2.shape
    tm = MLA_TOKENS
    return pl.pallas_call(
        _out_proj_body,
        out_shape=jax.ShapeDtypeStruct((t, d), F32),
        grid=(t // tm,),
        in_specs=[pl.BlockSpec((tm, d), lambda i: (i, 0)), pl.BlockSpec((tm, o2.shape[1]), lambda i: (i, 0)),
                  _const_spec(w_out.shape)],
        out_specs=pl.BlockSpec((tm, d), lambda i: (i, 0)),
        compiler_params=_cparams("parallel"),
        name="mla_out",
    )(x2, o2, w_out.astype(BF16))


def _sort_network(n):
    def merge(lo, hi, r):
        step = 2 * r
        if step < hi - lo:
            yield from merge(lo, hi, step)
            yield from merge(lo + r, hi, step)
            yield from ((i, i + r) for i in range(lo + r, hi - r, step))
        else:
            yield (lo, lo + r)

    def sort(lo, hi):
        if hi - lo >= 1:
            mid = lo + (hi - lo) // 2
            yield from sort(lo, mid)
            yield from sort(mid + 1, hi)
            yield from merge(lo, hi, 1)

    return tuple(sort(0, n - 1))


_SORT_TOPK = _sort_network(PEER_TOPK)


def _top_sorted(slabs):
    groups = []
    for g0 in range(0, len(slabs), PEER_TOPK):
        v = list(slabs[g0:g0 + PEER_TOPK])
        for i, j in _SORT_TOPK:
            v[i], v[j] = jnp.maximum(v[i], v[j]), jnp.minimum(v[i], v[j])
        groups.append(v)
    while len(groups) > 1:
        merged = []
        for a, b in zip(groups[0::2], groups[1::2]):
            v = [jnp.maximum(a[i], b[PEER_TOPK - 1 - i]) for i in range(PEER_TOPK)]
            d = PEER_TOPK // 2
            while d >= 1:
                for i in range(PEER_TOPK):
                    if not i & d:
                        v[i], v[i + d] = jnp.maximum(v[i], v[i + d]), jnp.minimum(v[i], v[i + d])
                d //= 2
            merged.append(v)
        groups = merged
    return groups[0]


def _merge_counts(s1, s2):
    n = [jnp.zeros_like(s1[0]) for _ in range(PEER_TOPK)]
    front = [s1[a] + s2[0] for a in range(PEER_TOPK)]
    for _ in range(PEER_TOPK):
        m = functools.reduce(jnp.maximum, front)
        first = jnp.full_like(m, float(PEER_TOPK))
        for a in reversed(range(PEER_TOPK)):
            first = jnp.where(front[a] == m, float(a), first)
        hits = [first == float(a) for a in range(PEER_TOPK)]
        n_hit = jnp.zeros_like(m)
        for a in range(PEER_TOPK):
            n_hit = jnp.where(hits[a], n[a], n_hit)
        nxt = jnp.full_like(m, -jnp.inf)
        for b in range(1, PEER_TOPK):
            nxt = jnp.where(n_hit == float(b - 1), s2[b], nxt)
        for a in range(PEER_TOPK):
            n[a] = jnp.where(hits[a], n[a] + 1.0, n[a])
            front[a] = jnp.where(hits[a], s1[a] + nxt, front[a])
    return n


_RANK_CHECKSUM = float(sum(range(PEER_TOPK)) + (PEER_N_KEYS - PEER_TOPK) * PEER_TOPK)


def _sel_tile(lt, s_ref, r_ref, t_ref, e1_ref, nk_ref, e2_ref, rb_ref):
    keys = range(PEER_N_KEYS)
    top = [_top_sorted([s_ref[p, lt, k] for k in keys]) for p in range(2)]
    n = _merge_counts(top[0], top[1])

    e1s = [jnp.exp(v - top[0][0]) for v in top[0]]
    e2s = [jnp.exp(v - top[1][0]) for v in top[1]]
    z = jnp.zeros_like(e1s[0])
    for b in range(PEER_TOPK):
        row = jnp.zeros_like(z)
        for a in range(PEER_TOPK):
            row = row + jnp.where(n[a] > float(b), e1s[a], 0.0)
        z = z + row * e2s[b]
    inv_z = 1.0 / z

    def rank(s, sorted_vals):
        r = jnp.full_like(s, float(PEER_TOPK))
        for a in reversed(range(PEER_TOPK)):
            r = jnp.where(s >= sorted_vals[a], float(a), r)
        return r

    def shifted(r, seen):
        seen = list(seen)
        back = jnp.zeros_like(r)
        for a in range(PEER_TOPK):
            hit = r == float(a)
            back = jnp.where(hit, seen[a], back)
            seen[a] = jnp.where(hit, seen[a] + 1.0, seen[a])
        return jnp.minimum(r + back, float(PEER_TOPK)), tuple(seen)

    none_seen = tuple(jnp.zeros_like(z) for _ in range(PEER_TOPK))

    def any_lane(mask):
        return jnp.max(jnp.where(mask, 1.0, 0.0)) > 0.0

    total = jnp.zeros_like(z)
    for k in keys:
        r = rank(s_ref[1, lt, k], top[1])
        r_ref[k] = r
        total = total + r
        t_ref[k] = jnp.exp(s_ref[1, lt, k] - top[1][0])

    @pl.when(any_lane(total != _RANK_CHECKSUM))
    def _():
        def fix(k, seen):
            r_ref[k], seen = shifted(r_ref[k], seen)
            return seen

        lax.fori_loop(0, PEER_N_KEYS, fix, none_seen)

    count = jnp.zeros_like(z)
    for k in keys:
        s = s_ref[0, lt, k]
        nk = jnp.zeros_like(s)
        for a in reversed(range(PEER_TOPK)):
            nk = jnp.where(s >= top[0][a], n[a], nk)
        nk_ref[lt, k] = nk
        e1_ref[lt, k] = jnp.exp(s - top[0][0]) * inv_z
        count = count + jnp.where(s >= top[0][PEER_TOPK - 1], 1.0, 0.0)
    tied = count != float(PEER_TOPK)
    for a in range(PEER_TOPK - 1):
        tied = tied | (top[0][a] == top[0][a + 1])

    @pl.when(any_lane(tied))
    def _():
        def fix(k, seen):
            r, seen = shifted(rank(s_ref[0, lt, k], top[0]), seen)
            nk = jnp.zeros_like(r)
            for a in range(PEER_TOPK):
                nk = jnp.where(r == float(a), n[a], nk)
            nk_ref[lt, k] = nk
            return seen

        lax.fori_loop(0, PEER_N_KEYS, fix, none_seen)

    e2p = pltpu.bitcast(jnp.swapaxes(t_ref[...], 0, 1).astype(BF16), F32)
    rbp = pltpu.bitcast(jnp.swapaxes(r_ref[...], 0, 1).astype(BF16), F32)
    for h in range(PEER_HEADS):
        e2_ref[h, lt] = e2p[h]
        rb_ref[h, lt] = rbp[h]


def _fold_keys_body(keys_ref, wq_ref, o_ref):
    o_ref[0, 0] = jnp.dot(keys_ref[0], wq_ref[0, 0], preferred_element_type=F32, precision=lax.Precision.HIGHEST)


def _fold_keys(sub_keys, w_q):
    d = w_q.shape[0]
    wq_t = w_q.reshape(d, PEER_HEADS, 2, PEER_HALF).transpose(2, 1, 3, 0)
    return pl.pallas_call(
        _fold_keys_body,
        out_shape=jax.ShapeDtypeStruct((2, PEER_HEADS, PEER_N_KEYS, d), F32),
        grid=(2, PEER_HEADS),
        in_specs=[pl.BlockSpec((1, PEER_N_KEYS, PEER_HALF), lambda p, h: (p, 0, 0)),
                  pl.BlockSpec((1, 1, PEER_HALF, d), lambda p, h: (p, h, 0, 0))],
        out_specs=pl.BlockSpec((1, 1, PEER_N_KEYS, d), lambda p, h: (p, h, 0, 0)),
        compiler_params=_cparams("parallel", "parallel"),
        name="fold_keys",
    )(sub_keys, wq_t)


def _peer_sel_body(x_ref, g_ref, ws_ref, hb_ref, e1_ref, nk_ref, e2_ref, rb_ref, s_ref, r_ref, t_ref):
    hb = _rms(x_ref[...], g_ref[...]).astype(BF16)
    hb_ref[...] = pltpu.bitcast(hb, F32)
    tiles = x_ref.shape[0] // LANES
    for p in range(2):
        sc = _dot_nt(ws_ref[p], hb)
        sc = sc.reshape(PEER_N_KEYS, PEER_HEADS, tiles * LANES)
        for lt in range(tiles):
            s_ref[p, lt] = sc[:, :, lt * LANES:(lt + 1) * LANES]

    def tile(lt, _):
        _sel_tile(lt, s_ref, r_ref, t_ref, e1_ref, nk_ref, e2_ref, rb_ref)
        return 0

    lax.fori_loop(0, tiles, tile, 0)


def _peer_select(x2, gain, w_q, sub_keys):
    t, d = x2.shape
    tb = SEL_TOKENS
    tiles = tb // LANES
    ws = _fold_keys(sub_keys, w_q).transpose(0, 2, 1, 3).reshape(2, PEER_N_KEYS * PEER_HEADS, d).astype(BF16)
    key_shape = jax.ShapeDtypeStruct((t // LANES, PEER_N_KEYS, PEER_HEADS, LANES), F32)
    key_spec = pl.BlockSpec((tiles, PEER_N_KEYS, PEER_HEADS, LANES), lambda i: (i, 0, 0, 0))
    pair_shape = jax.ShapeDtypeStruct((PEER_HEADS, t // LANES, PEER_N_KEYS // 2, LANES), F32)
    pair_spec = pl.BlockSpec((PEER_HEADS, tiles, PEER_N_KEYS // 2, LANES), lambda i: (0, i, 0, 0))
    slab = (PEER_N_KEYS, PEER_HEADS, LANES)
    return pl.pallas_call(
        _peer_sel_body,
        out_shape=(jax.ShapeDtypeStruct((t // 2, d), F32), key_shape, key_shape, pair_shape, pair_shape),
        grid=(t // tb,),
        in_specs=[pl.BlockSpec((tb, d), lambda i: (i, 0)), _const_spec((1, d)), _const_spec(ws.shape)],
        out_specs=(pl.BlockSpec((tb // 2, d), lambda i: (i, 0)), key_spec, key_spec, pair_spec, pair_spec),
        scratch_shapes=[pltpu.VMEM((2, tiles) + slab, F32), pltpu.VMEM(slab, F32), pltpu.VMEM(slab, F32)],
        compiler_params=_cparams("parallel"),
        name="peer_select",
    )(x2, gain[None, :], ws)


def _key_rows_bf16(row):
    packed = jnp.broadcast_to(row, (BF16_SUBLANES, row.shape[1])).astype(BF16)
    return jnp.concatenate([packed] * (PEER_N_KEYS // BF16_SUBLANES), axis=0)


def _peer_main_body(final_norm, x_ref, hb_ref, u_ref, vt_ref, e1_ref, n_ref, e2_ref, rb_ref, fg_ref, o_ref, acc_ref,
                    a_ref):
    s = pl.program_id(1)

    @pl.when(s == 0)
    def _():
        acc_ref[...] = jnp.zeros_like(acc_ref)
        a_ref[...] = jnp.zeros_like(a_ref)

    piece_tiles = PEER_PIECE // LANES
    for p in range(acc_ref.shape[0]):
        g_tiles = []
        for tb in range(p * piece_tiles, (p + 1) * piece_tiles):
            g_keys = []
            for k0 in range(0, PEER_KEY_GROUP, PEER_KEY_REUSE):
                ks = range(k0, k0 + PEER_KEY_REUSE)
                w = {}
                for h in range(PEER_HEADS):
                    rb_t = pltpu.bitcast(rb_ref[h, tb], BF16)
                    e2_t = pltpu.bitcast(e2_ref[h, tb], BF16)
                    for k in ks:
                        n_b = _key_rows_bf16(n_ref[tb, k, h:h + 1, :])
                        e1_b = _key_rows_bf16(e1_ref[tb, k, h:h + 1, :])
                        term = jnp.where(rb_t < n_b, e2_t * e1_b, 0.0)
                        w[k] = term if h == 0 else w[k] + term
                for k in ks:
                    a_t = pltpu.bitcast(a_ref[tb, k * (PEER_N_KEYS // 2):(k + 1) * (PEER_N_KEYS // 2), :], BF16)
                    g_keys.append(w[k] * _gelu(a_t))
            g_tiles.append(jnp.concatenate(g_keys, axis=0))
        acc_ref[p] += _dot(pltpu.bitcast(vt_ref[0], BF16), jnp.concatenate(g_tiles, axis=1))
        hb_piece = pltpu.bitcast(hb_ref[p * (PEER_PIECE // 2):(p + 1) * (PEER_PIECE // 2), :], BF16)
        a_piece = pltpu.bitcast(_dot_nt(pltpu.bitcast(u_ref[...], BF16), hb_piece).astype(BF16), F32)
        for t in range(piece_tiles):
            a_ref[p * piece_tiles + t] = a_piece[:, t * LANES:(t + 1) * LANES]

    @pl.when(s == pl.num_programs(1) - 1)
    def _():
        for p in range(acc_ref.shape[0]):
            rows = slice(p * PEER_PIECE, (p + 1) * PEER_PIECE)
            y = x_ref[rows, :] + acc_ref[p].T
            if final_norm:
                y = _rms(y, fg_ref[...])
            o_ref[rows, :] = y


def _pack_rows_body(x_ref, o_ref):
    o_ref[...] = pltpu.bitcast(x_ref[0].astype(BF16), F32)


def _pack_rows(tables, layer):
    _, rows, d = tables.shape
    blk = PACK_ROWS
    return pl.pallas_call(
        _pack_rows_body,
        out_shape=jax.ShapeDtypeStruct((rows // 2, d), F32),
        grid=(rows // blk,),
        in_specs=[pl.BlockSpec((1, blk, d), lambda i: (layer, i, 0))],
        out_specs=pl.BlockSpec((blk // 2, d), lambda i: (i, 0)),
        compiler_params=_cparams("parallel"),
        name="pack_rows",
    )(tables)


def _pack_transposed_body(x_ref, o_ref):
    o_ref[0] = pltpu.bitcast(x_ref[0].T.astype(BF16), F32)


def _pack_transposed_chunks(tables, layer, chunk):
    _, rows, d = tables.shape
    return pl.pallas_call(
        _pack_transposed_body,
        out_shape=jax.ShapeDtypeStruct((rows // chunk, d // 2, chunk), F32),
        grid=(rows // chunk,),
        in_specs=[pl.BlockSpec((1, chunk, d), lambda i: (layer, i, 0))],
        out_specs=pl.BlockSpec((1, d // 2, chunk), lambda i: (i, 0, 0)),
        compiler_params=_cparams("parallel"),
        name="pack_transposed",
    )(tables)


def _peer_experts(x2, hb, e1, n, e2, rb, u_tabs, v_tabs, layer, final_gain=None):
    t, d = x2.shape
    n_experts = u_tabs.shape[1]
    tb, eb = PEER_TOKENS, PEER_EXPERTS
    chunks = n_experts // eb
    final_norm = final_gain is not None
    fg = (final_gain if final_norm else jnp.ones((d,), F32))[None, :]
    once = dict(pipeline_mode=pl.Buffered(1))
    tok_spec = pl.BlockSpec((tb, d), lambda i, s: (i, 0), **once)
    hb_spec = pl.BlockSpec((tb // 2, d), lambda i, s: (i, 0), **once)
    sel_spec = pl.BlockSpec((PEER_HEADS, tb // LANES, PEER_N_KEYS // 2, LANES), lambda i, s: (0, i, 0, 0), **once)
    key_spec = pl.BlockSpec((tb // LANES, PEER_KEY_GROUP, PEER_HEADS, LANES),
                            lambda i, s: (i, jnp.clip(s - 1, 0, chunks - 1), 0, 0))
    return pl.pallas_call(
        functools.partial(_peer_main_body, final_norm),
        out_shape=jax.ShapeDtypeStruct((t, d), F32),
        grid=(t // tb, chunks + 1),
        in_specs=[
            tok_spec,
            hb_spec,
            pl.BlockSpec((eb // 2, d), lambda i, s: (jnp.minimum(s, chunks - 1), 0)),
            pl.BlockSpec((1, d // 2, eb), lambda i, s: (jnp.maximum(s - 1, 0), 0, 0)),
            key_spec, key_spec, sel_spec, sel_spec,
            _const_spec((1, d)),
        ],
        out_specs=pl.BlockSpec((tb, d), lambda i, s: (i, 0)),
        scratch_shapes=[pltpu.VMEM((tb // PEER_PIECE, d, PEER_PIECE), F32),
                        pltpu.VMEM((tb // LANES, eb // 2, LANES), F32)],
        compiler_params=_cparams("parallel", "arbitrary"),
        name="peer_experts",
    )(x2, hb, _pack_rows(u_tabs, layer), _pack_transposed_chunks(v_tabs, layer, eb), e1, n, e2, rb, fg)


def _peer(x2, gain, w_q, sub_keys, u_tabs, v_tabs, layer, final_gain=None):
    hb, e1, n, e2, rb = _peer_select(x2, gain, w_q, sub_keys)
    return _peer_experts(x2, hb, e1, n, e2, rb, u_tabs, v_tabs, layer, final_gain)


def kernel(x, norm_mix, norm_ffn, hy_w_in, gm_v_norm, gm_w_s, gm_b_s, ret_norm, hy_w_out, mla_w_down, mla_q_norm,
           mla_w_q_up, mla_kv_norm, mla_w_kv_up, mla_w_out, peer_w_q, peer_sub_keys, peer_u, peer_v, final_norm):
    b, s, d = x.shape
    depth = norm_mix.shape[0]
    for layer in range(depth):
        j = layer // 2
        if layer % 2 == 0:
            x = _hy_mixer(x, norm_mix[layer], hy_w_in[j], gm_v_norm[j], gm_w_s[j], gm_b_s[j], ret_norm[j], hy_w_out[j])
            x2 = x.reshape(b * s, d)
        else:
            q, k, v = _mla_proj(x, norm_mix[layer], mla_w_down[j], mla_q_norm[j], mla_w_q_up[j], mla_kv_norm[j],
                                mla_w_kv_up[j])
            o = _flash(q, k, v)
            x2 = _out_proj(x.reshape(b * s, d), o.reshape(b * s, -1), mla_w_out[j])
        last = layer == depth - 1
        x2 = _peer(x2, norm_ffn[layer], peer_w_q[layer], peer_sub_keys[layer], peer_u, peer_v, layer,
                   final_norm if last else None)
        x = x2.reshape(b, s, d)
    return x
```

```python
import functools
import math

import jax
import jax.numpy as jnp
from jax import lax
from jax.experimental import pallas as pl
from jax.experimental.pallas import tpu as pltpu

F32 = jnp.float32
BF16 = jnp.bfloat16

LANES = 128
BF16_SUBLANES = 16
MXU_WIDTH = 256
V7X_VMEM_BYTES = 64 * 1024 * 1024
VMEM_LIMIT_BYTES = 56 * 1024 * 1024

NORM_EPS = 1e-6
ROPE_THETA = 10000.0

GM_GROUPS = 4
GM_DIM = 256
GM_CHUNK = 128
GM_WIDTH = GM_GROUPS * GM_DIM
RET_HEADS = 4
RET_QK_DIM = 128
RET_V_DIM = 256
RET_CHUNK = 128
RET_QK_WIDTH = RET_HEADS * RET_QK_DIM
RET_V_WIDTH = RET_HEADS * RET_V_DIM
HY_OUT_WIDTH = GM_WIDTH + RET_V_WIDTH
MLA_HEADS = 8
MLA_Q_RANK = 384
MLA_KV_RANK = 256
MLA_NOPE_DIM = 128
MLA_ROPE_DIM = 64
MLA_V_DIM = 128
MLA_QK_DIM = MLA_NOPE_DIM + MLA_ROPE_DIM
MLA_QK_PAD = 2 * LANES
PEER_HEADS = 8
PEER_N_KEYS = 128
PEER_HALF = 128
PEER_TOPK = 16

HY_TOKENS = 512
MLA_TOKENS = 512
ATT_BLOCK = 512
ATT_KV_BLOCK = 2048
ATT_SUB_BLOCK = 1024
SEL_TOKENS = 512
PEER_TOKENS = 1024
PEER_EXPERTS = 1024
PEER_KEY_GROUP = PEER_EXPERTS // PEER_N_KEYS
PEER_KEY_REUSE = 2
PEER_PIECE = MXU_WIDTH
PACK_ROWS = 1024


def _cparams(*semantics):
    return pltpu.CompilerParams(dimension_semantics=semantics, vmem_limit_bytes=VMEM_LIMIT_BYTES)


def _const_spec(shape):
    return pl.BlockSpec(shape, lambda *_: (0,) * len(shape))


def _rms(x, g):
    return x * lax.rsqrt(jnp.mean(x * x, axis=-1, keepdims=True) + NORM_EPS) * g


def _gelu(x):
    return 0.5 * x * (1.0 + lax.erf(x * (2.0 ** -0.5)))


def _dot(a, b):
    return jnp.dot(a, b, preferred_element_type=F32)


def _dot_nt(a, b):
    return lax.dot_general(a, b, (((1,), (1,)), ((), ())), preferred_element_type=F32)


def _dot_tn(a, b):
    return lax.dot_general(a, b, (((0,), (0,)), ((), ())), preferred_element_type=F32)


def _hy_body(x_ref, gmix_ref, win_ref, vnorm_ref, ws_ref, bs_ref, cos_ref, sin_ref, dmask_ref, qdec_ref, kdec_ref,
             cdec_ref, rnorm_ref, wout_ref, o_ref, state_ref, y_ref):
    @pl.when(pl.program_id(1) == 0)
    def _():
        state_ref[...] = jnp.zeros_like(state_ref)

    x = x_ref[0]
    h = _rms(x, gmix_ref[...]).astype(BF16)
    tokens = x.shape[0]
    chunks = tokens // GM_CHUNK

    u_all = _gelu(_dot(h, win_ref[:, 0:GM_WIDTH]))
    v_all = _gelu(_dot(h, win_ref[:, GM_WIDTH:2 * GM_WIDTH]))
    row = lax.broadcasted_iota(jnp.int32, (GM_CHUNK, GM_CHUNK), 0)
    col = lax.broadcasted_iota(jnp.int32, (GM_CHUNK, GM_CHUNK), 1)
    causal = row >= col
    for g in range(GM_GROUPS):
        cols = slice(g * GM_DIM, (g + 1) * GM_DIM)
        vg = _rms(v_all[:, cols], vnorm_ref[:, cols]).astype(BF16)
        wg = jnp.where(causal, ws_ref[g], 0.0).astype(BF16)
        for c in range(chunks):
            rows = slice(c * GM_CHUNK, (c + 1) * GM_CHUNK)
            mixed = _dot(wg, vg[rows]) + bs_ref[g]
            y_ref[rows, cols] = (u_all[rows, cols] * mixed).astype(BF16)

    base = 2 * GM_WIDTH
    q_all = _dot(h, win_ref[:, base:base + RET_QK_WIDTH])
    k_all = _dot(h, win_ref[:, base + RET_QK_WIDTH:base + 2 * RET_QK_WIDTH])
    base += 2 * RET_QK_WIDTH
    v_all = _dot(h, win_ref[:, base:base + RET_V_WIDTH])
    g_all = _dot(h, win_ref[:, base + RET_V_WIDTH:base + 2 * RET_V_WIDTH])
    cos = cos_ref[...]
    sin = sin_ref[...]
    for hd in range(RET_HEADS):
        qk_cols = slice(hd * RET_QK_DIM, (hd + 1) * RET_QK_DIM)
        v_cols = slice(hd * RET_V_DIM, (hd + 1) * RET_V_DIM)
        q = q_all[:, qk_cols]
        k = k_all[:, qk_cols]
        q = q * cos + pltpu.roll(q, RET_QK_DIM // 2, 1) * sin
        k = (k * cos + pltpu.roll(k, RET_QK_DIM // 2, 1) * sin) * (RET_QK_DIM ** -0.5)
        for c in range(chunks):
            rows = slice(c * RET_CHUNK, (c + 1) * RET_CHUNK)
            qc, kc, vc = q[rows], k[rows], v_all[rows, v_cols]
            scores = _dot_nt(qc.astype(BF16), kc.astype(BF16)) * dmask_ref[hd]
            intra = _dot(scores.astype(BF16), vc.astype(BF16))
            state = state_ref[hd]
            cross = _dot((qc * qdec_ref[hd]).astype(BF16), state.astype(BF16))
            state_ref[hd] = state * cdec_ref[hd] + _dot_tn(kc.astype(BF16), (vc * kdec_ref[hd]).astype(BF16))
            gate = g_all[rows, v_cols]
            yb = _rms(intra + cross, rnorm_ref[:, v_cols]) * (gate * jax.nn.sigmoid(gate))
            y_ref[rows, GM_WIDTH + hd * RET_V_DIM:GM_WIDTH + (hd + 1) * RET_V_DIM] = yb.astype(BF16)

    o_ref[0] = x + _dot(y_ref[...], wout_ref[...])


def _rope_angles(seq, half):
    inv = 1.0 / (ROPE_THETA ** (jnp.arange(half, dtype=F32) / half))
    ang = jnp.arange(seq, dtype=F32)[:, None] * inv[None, :]
    return jnp.cos(ang), jnp.sin(ang)


def _hy_mixer(x, gmix, w_in, v_norm, w_s, b_s, ret_norm, w_out):
    b, s, d = x.shape
    ts = HY_TOKENS
    cos, sin = _rope_angles(s, RET_QK_DIM // 2)
    cos2 = jnp.concatenate([cos, cos], axis=1)
    sin2 = jnp.concatenate([-sin, sin], axis=1)
    log_gamma = jnp.log(1.0 - 2.0 ** (-5.0 - jnp.arange(RET_HEADS, dtype=F32)))
    pos = jnp.arange(RET_CHUNK, dtype=F32)
    diff = pos[:, None] - pos[None, :]
    dmask = jnp.where(diff[None] >= 0, jnp.exp(diff[None] * log_gamma[:, None, None]), 0.0)
    qdec = jnp.exp((pos[None, :] + 1.0) * log_gamma[:, None])[:, :, None]
    kdec = jnp.exp((RET_CHUNK - 1.0 - pos[None, :]) * log_gamma[:, None])[:, :, None]
    cdec = jnp.exp(RET_CHUNK * log_gamma)[:, None, None]
    in_width = w_in.shape[1]
    return pl.pallas_call(
        _hy_body,
        out_shape=jax.ShapeDtypeStruct((b, s, d), F32),
        grid=(b, s // ts),
        in_specs=[
            pl.BlockSpec((1, ts, d), lambda i, j: (i, j, 0)),
            _const_spec((1, d)),
            _const_spec((d, in_width)),
            _const_spec((1, GM_WIDTH)),
            _const_spec((GM_GROUPS, GM_CHUNK, GM_CHUNK)),
            _const_spec((GM_GROUPS, GM_CHUNK, 1)),
            pl.BlockSpec((ts, RET_QK_DIM), lambda i, j: (j, 0)),
            pl.BlockSpec((ts, RET_QK_DIM), lambda i, j: (j, 0)),
            _const_spec((RET_HEADS, RET_CHUNK, RET_CHUNK)),
            _const_spec((RET_HEADS, RET_CHUNK, 1)),
            _const_spec((RET_HEADS, RET_CHUNK, 1)),
            _const_spec((RET_HEADS, 1, 1)),
            _const_spec((1, RET_V_WIDTH)),
            _const_spec((HY_OUT_WIDTH, d)),
        ],
        out_specs=pl.BlockSpec((1, ts, d), lambda i, j: (i, j, 0)),
        scratch_shapes=[
            pltpu.VMEM((RET_HEADS, RET_QK_DIM, RET_V_DIM), F32),
            pltpu.VMEM((ts, HY_OUT_WIDTH), BF16),
        ],
        compiler_params=_cparams("parallel", "arbitrary"),
        name="hy_mixer",
    )(x, gmix[None, :], w_in.astype(BF16), v_norm[None, :], w_s, b_s[:, :, None], cos2, sin2, dmask, qdec, kdec, cdec,
      ret_norm[None, :], w_out.astype(BF16))


def _mla_proj_body(x_ref, gmix_ref, wdown_ref, qn_ref, kvn_ref, wq_ref, wqrot_ref, wk_ref, wv_ref, cq_ref, sq_ref,
                   ck_ref, sk_ref, q_ref, k_ref, v_ref):
    h = _rms(x_ref[0], gmix_ref[...]).astype(BF16)
    down = _dot(h, wdown_ref[...])
    kv_end = MLA_Q_RANK + MLA_KV_RANK
    cq = _rms(down[:, :MLA_Q_RANK], qn_ref[...]).astype(BF16)
    ckv = _rms(down[:, MLA_Q_RANK:kv_end], kvn_ref[...]).astype(BF16)
    k_rope = (down[:, kv_end:kv_end + LANES] * ck_ref[...] + down[:, kv_end + LANES:kv_end + 2 * LANES] * sk_ref[...])
    k_rope = k_rope.astype(BF16)
    qa = _dot(cq, wq_ref[...])
    qb = _dot(cq, wqrot_ref[...])
    kn = _dot(ckv, wk_ref[...])
    scale = MLA_QK_DIM ** -0.5
    cq_tab = cq_ref[...] * scale
    sq_tab = sq_ref[...] * scale
    for hd in range(MLA_HEADS):
        seg = slice(hd * MLA_QK_PAD, (hd + 1) * MLA_QK_PAD)
        q_ref[0, :, seg] = (qa[:, seg] * cq_tab + qb[:, seg] * sq_tab).astype(BF16)
        k_ref[0, :, hd * MLA_QK_PAD:hd * MLA_QK_PAD + MLA_NOPE_DIM] = (
            kn[:, hd * MLA_NOPE_DIM:(hd + 1) * MLA_NOPE_DIM].astype(BF16))
        k_ref[0, :, hd * MLA_QK_PAD + MLA_NOPE_DIM:(hd + 1) * MLA_QK_PAD] = k_rope
    v_ref[0] = _dot(ckv, wv_ref[...]).astype(BF16)


def _rot_half_cols(w):
    half = w.shape[-1] // 2
    return jnp.concatenate([-w[..., half:], w[..., :half]], axis=-1)


def _mla_proj(x, gmix, w_down, q_norm, w_q_up, kv_norm, w_kv_up):
    b, s, d = x.shape
    tm = MLA_TOKENS
    kv_end = MLA_Q_RANK + MLA_KV_RANK
    pad = MLA_QK_PAD - MLA_QK_DIM
    w_kr = w_down[:, kv_end:]
    zk = jnp.zeros((d, LANES - MLA_ROPE_DIM), F32)
    w_down_x = jnp.concatenate([w_down[:, :kv_end], w_kr, zk, _rot_half_cols(w_kr), zk], axis=1).astype(BF16)
    wq = w_q_up.reshape(MLA_Q_RANK, MLA_HEADS, MLA_QK_DIM)
    zq = jnp.zeros((MLA_Q_RANK, MLA_HEADS, pad), F32)
    wq_a = jnp.concatenate([wq, zq], axis=2).reshape(MLA_Q_RANK, MLA_HEADS * MLA_QK_PAD).astype(BF16)
    wq_b = jnp.concatenate([jnp.zeros((MLA_Q_RANK, MLA_HEADS, MLA_NOPE_DIM), F32),
                            _rot_half_cols(wq[:, :, MLA_NOPE_DIM:]), zq], axis=2)
    wq_b = wq_b.reshape(MLA_Q_RANK, MLA_HEADS * MLA_QK_PAD).astype(BF16)
    wkv = w_kv_up.reshape(MLA_KV_RANK, MLA_HEADS, MLA_NOPE_DIM + MLA_V_DIM)
    wk = wkv[:, :, :MLA_NOPE_DIM].reshape(MLA_KV_RANK, MLA_HEADS * MLA_NOPE_DIM).astype(BF16)
    wv = wkv[:, :, MLA_NOPE_DIM:].reshape(MLA_KV_RANK, MLA_HEADS * MLA_V_DIM).astype(BF16)
    cos, sin = _rope_angles(s, MLA_ROPE_DIM // 2)
    zt = jnp.zeros((s, pad), F32)
    cq_tab = jnp.concatenate([jnp.ones((s, MLA_NOPE_DIM), F32), cos, cos, zt], axis=1)
    sq_tab = jnp.concatenate([jnp.zeros((s, MLA_NOPE_DIM), F32), sin, sin, zt], axis=1)
    ck_tab = jnp.concatenate([cos, cos, zt], axis=1)
    sk_tab = jnp.concatenate([sin, sin, zt], axis=1)
    qk_width = MLA_HEADS * MLA_QK_PAD
    v_width = MLA_HEADS * MLA_V_DIM
    tab = lambda w: pl.BlockSpec((tm, w), lambda i, j: (j, 0))
    out = lambda w: pl.BlockSpec((1, tm, w), lambda i, j: (i, j, 0))
    return pl.pallas_call(
        _mla_proj_body,
        out_shape=(jax.ShapeDtypeStruct((b, s, qk_width), BF16), jax.ShapeDtypeStruct((b, s, qk_width), BF16),
                   jax.ShapeDtypeStruct((b, s, v_width), BF16)),
        grid=(b, s // tm),
        in_specs=[
            out(d),
            _const_spec((1, d)),
            _const_spec(w_down_x.shape),
            _const_spec((1, MLA_Q_RANK)),
            _const_spec((1, MLA_KV_RANK)),
            _const_spec(wq_a.shape),
            _const_spec(wq_b.shape),
            _const_spec(wk.shape),
            _const_spec(wv.shape),
            tab(MLA_QK_PAD), tab(MLA_QK_PAD), tab(LANES), tab(LANES),
        ],
        out_specs=(out(qk_width), out(qk_width), out(v_width)),
        compiler_params=_cparams("parallel", "parallel"),
        name="mla_proj",
    )(x, gmix[None, :], w_down_x, q_norm[None, :], kv_norm[None, :], wq_a, wq_b, wk, wv, cq_tab, sq_tab, ck_tab, sk_tab)


def _flash_body(q_ref, k_ref, v_ref, o_ref):
    blk = q_ref.shape[1]
    kvb = ATT_KV_BLOCK
    i = pl.program_id(2)
    q = q_ref[0]
    full = (i * blk) // kvb

    def step(start, width, carry, masked):
        m, l, acc = carry
        for off in range(0, width, ATT_SUB_BLOCK):
            w = min(ATT_SUB_BLOCK, width - off)
            s = _dot_nt(q, k_ref[0, pl.ds(start + off, w), :])
            if masked:
                row = lax.broadcasted_iota(jnp.int32, s.shape, 0) + i * blk
                col = lax.broadcasted_iota(jnp.int32, s.shape, 1) + (start + off)
                s = jnp.where(row >= col, s, -jnp.inf)
            m_new = jnp.maximum(m, jnp.max(s, axis=1, keepdims=True))
            p = jnp.exp(s - m_new)
            alpha = jnp.exp(m - m_new)
            l = alpha * l + jnp.sum(p, axis=1, keepdims=True)
            acc = alpha * acc + _dot(p.astype(BF16), v_ref[0, pl.ds(start + off, w), :])
            m = m_new
        return m, l, acc

    init = (jnp.full((blk, 1), -jnp.inf, F32), jnp.zeros((blk, 1), F32), jnp.zeros((blk, v_ref.shape[2]), F32))
    carry = lax.fori_loop(0, full, lambda j, c: step(pl.multiple_of(j * kvb, kvb), kvb, c, False), init)
    for rest in range(1, kvb // blk + 1):
        @pl.when((i + 1) * blk - full * kvb == rest * blk)
        def _():
            _, l, acc = step(pl.multiple_of(full * kvb, kvb), rest * blk, carry, True)
            o_ref[0] = (acc / l).astype(BF16)


def _flash(q, k, v):
    b, s, _ = q.shape
    blk = ATT_BLOCK
    return pl.pallas_call(
        _flash_body,
        out_shape=jax.ShapeDtypeStruct((b, s, MLA_HEADS * MLA_V_DIM), BF16),
        grid=(b, MLA_HEADS, s // blk),
        in_specs=[
            pl.BlockSpec((1, blk, MLA_QK_PAD), lambda bi, h, i: (bi, i, h)),
            pl.BlockSpec((1, s, MLA_QK_PAD), lambda bi, h, i: (bi, 0, h)),
            pl.BlockSpec((1, s, MLA_V_DIM), lambda bi, h, i: (bi, 0, h)),
        ],
        out_specs=pl.BlockSpec((1, blk, MLA_V_DIM), lambda bi, h, i: (bi, i, h)),
        compiler_params=_cparams("parallel", "parallel", "arbitrary"),
        name="mla_flash",
    )(q, k, v)


def _out_proj_body(x_ref, o_ref, w_ref, y_ref):
    y_ref[...] = x_ref[...] + _dot(o_ref[...], w_ref[...])


def _out_proj(x2, o2, w_out):
    t, d = x2.shape
    tm = MLA_TOKENS
    return pl.pallas_call(
        _out_proj_body,
        out_shape=jax.ShapeDtypeStruct((t, d), F32),
        grid=(t // tm,),
        in_specs=[pl.BlockSpec((tm, d), lambda i: (i, 0)), pl.BlockSpec((tm, o2.shape[1]), lambda i: (i, 0)),
                  _const_spec(w_out.shape)],
        out_specs=pl.BlockSpec((tm, d), lambda i: (i, 0)),
        compiler_params=_cparams("parallel"),
        name="mla_out",
    )(x2, o2, w_out.astype(BF16))


def _sort_network(n):
    def merge(lo, hi, r):
        step = 2 * r
        if step < hi - lo:
            yield from merge(lo, hi, step)
            yield from merge(lo + r, hi, step)
            yield from ((i, i + r) for i in range(lo + r, hi - r, step))
        else:
            yield (lo, lo + r)

    def sort(lo, hi):
        if hi - lo >= 1:
            mid = lo + (hi - lo) // 2
            yield from sort(lo, mid)
            yield from sort(mid + 1, hi)
            yield from merge(lo, hi, 1)

    return tuple(sort(0, n - 1))


_SORT_TOPK = _sort_network(PEER_TOPK)


def _top_sorted(slabs):
    groups = []
    for g0 in range(0, len(slabs), PEER_TOPK):
        v = list(slabs[g0:g0 + PEER_TOPK])
        for i, j in _SORT_TOPK:
            v[i], v[j] = jnp.maximum(v[i], v[j]), jnp.minimum(v[i], v[j])
        groups.append(v)
    while len(groups) > 1:
        merged = []
        for a, b in zip(groups[0::2], groups[1::2]):
            v = [jnp.maximum(a[i], b[PEER_TOPK - 1 - i]) for i in range(PEER_TOPK)]
            d = PEER_TOPK // 2
            while d >= 1:
                for i in range(PEER_TOPK):
                    if not i & d:
                        v[i], v[i + d] = jnp.maximum(v[i], v[i + d]), jnp.minimum(v[i], v[i + d])
                d //= 2
            merged.append(v)
        groups = merged
    return groups[0]


def _merge_counts(s1, s2):
    n = [jnp.zeros_like(s1[0]) for _ in range(PEER_TOPK)]
    front = [s1[a] + s2[0] for a in range(PEER_TOPK)]
    for _ in range(PEER_TOPK):
        m = functools.reduce(jnp.maximum, front)
        first = jnp.full_like(m, float(PEER_TOPK))
        for a in reversed(range(PEER_TOPK)):
            first = jnp.where(front[a] == m, float(a), first)
        hits = [first == float(a) for a in range(PEER_TOPK)]
        n_hit = jnp.zeros_like(m)
        for a in range(PEER_TOPK):
            n_hit = jnp.where(hits[a], n[a], n_hit)
        nxt = jnp.full_like(m, -jnp.inf)
        for b in range(1, PEER_TOPK):
            nxt = jnp.where(n_hit == float(b - 1), s2[b], nxt)
        for a in range(PEER_TOPK):
            n[a] = jnp.where(hits[a], n[a] + 1.0, n[a])
            front[a] = jnp.where(hits[a], s1[a] + nxt, front[a])
    return n


_RANK_CHECKSUM = float(sum(range(PEER_TOPK)) + (PEER_N_KEYS - PEER_TOPK) * PEER_TOPK)


def _sel_tile(lt, s_ref, r_ref, t_ref, e1_ref, nk_ref, e2_ref, rb_ref):
    keys = range(PEER_N_KEYS)
    top = [_top_sorted([s_ref[p, lt, k] for k in keys]) for p in range(2)]
    n = _merge_counts(top[0], top[1])

    e1s = [jnp.exp(v - top[0][0]) for v in top[0]]
    e2s = [jnp.exp(v - top[1][0]) for v in top[1]]
    z = jnp.zeros_like(e1s[0])
    for b in range(PEER_TOPK):
        row = jnp.zeros_like(z)
        for a in range(PEER_TOPK):
            row = row + jnp.where(n[a] > float(b), e1s[a], 0.0)
        z = z + row * e2s[b]
    inv_z = 1.0 / z

    def rank(s, sorted_vals):
        r = jnp.full_like(s, float(PEER_TOPK))
        for a in reversed(range(PEER_TOPK)):
            r = jnp.where(s >= sorted_vals[a], float(a), r)
        return r

    def shifted(r, seen):
        seen = list(seen)
        back = jnp.zeros_like(r)
        for a in range(PEER_TOPK):
            hit = r == float(a)
            back = jnp.where(hit, seen[a], back)
            seen[a] = jnp.where(hit, seen[a] + 1.0, seen[a])
        return jnp.minimum(r + back, float(PEER_TOPK)), tuple(seen)

    none_seen = tuple(jnp.zeros_like(z) for _ in range(PEER_TOPK))

    def any_lane(mask):
        return jnp.max(jnp.where(mask, 1.0, 0.0)) > 0.0

    total = jnp.zeros_like(z)
    for k in keys:
        r = rank(s_ref[1, lt, k], top[1])
        r_ref[k] = r
        total = total + r
        t_ref[k] = jnp.exp(s_ref[1, lt, k] - top[1][0])

    @pl.when(any_lane(total != _RANK_CHECKSUM))
    def _():
        def fix(k, seen):
            r_ref[k], seen = shifted(r_ref[k], seen)
            return seen

        lax.fori_loop(0, PEER_N_KEYS, fix, none_seen)

    count = jnp.zeros_like(z)
    for k in keys:
        s = s_ref[0, lt, k]
        nk = jnp.zeros_like(s)
        for a in reversed(range(PEER_TOPK)):
            nk = jnp.where(s >= top[0][a], n[a], nk)
        nk_ref[lt, k] = nk
        e1_ref[lt, k] = jnp.exp(s - top[0][0]) * inv_z
        count = count + jnp.where(s >= top[0][PEER_TOPK - 1], 1.0, 0.0)
    tied = count != float(PEER_TOPK)
    for a in range(PEER_TOPK - 1):
        tied = tied | (top[0][a] == top[0][a + 1])

    @pl.when(any_lane(tied))
    def _():
        def fix(k, seen):
            r, seen = shifted(rank(s_ref[0, lt, k], top[0]), seen)
            nk = jnp.zeros_like(r)
            for a in range(PEER_TOPK):
                nk = jnp.where(r == float(a), n[a], nk)
            nk_ref[lt, k] = nk
            return seen

        lax.fori_loop(0, PEER_N_KEYS, fix, none_seen)

    e2p = pltpu.bitcast(jnp.swapaxes(t_ref[...], 0, 1).astype(BF16), F32)
    rbp = pltpu.bitcast(jnp.swapaxes(r_ref[...], 0, 1).astype(BF16), F32)
    for h in range(PEER_HEADS):
        e2_ref[h, lt] = e2p[h]
        rb_ref[h, lt] = rbp[h]


def _fold_keys_body(keys_ref, wq_ref, o_ref):
    o_ref[0, 0] = jnp.dot(keys_ref[0], wq_ref[0, 0], preferred_element_type=F32, precision=lax.Precision.HIGHEST)


def _fold_keys(sub_keys, w_q):
    d = w_q.shape[0]
    wq_t = w_q.reshape(d, PEER_HEADS, 2, PEER_HALF).transpose(2, 1, 3, 0)
    return pl.pallas_call(
        _fold_keys_body,
        out_shape=jax.ShapeDtypeStruct((2, PEER_HEADS, PEER_N_KEYS, d), F32),
        grid=(2, PEER_HEADS),
        in_specs=[pl.BlockSpec((1, PEER_N_KEYS, PEER_HALF), lambda p, h: (p, 0, 0)),
                  pl.BlockSpec((1, 1, PEER_HALF, d), lambda p, h: (p, h, 0, 0))],
        out_specs=pl.BlockSpec((1, 1, PEER_N_KEYS, d), lambda p, h: (p, h, 0, 0)),
        compiler_params=_cparams("parallel", "parallel"),
        name="fold_keys",
    )(sub_keys, wq_t)


def _peer_sel_body(x_ref, g_ref, ws_ref, hb_ref, e1_ref, nk_ref, e2_ref, rb_ref, s_ref, r_ref, t_ref):
    hb = _rms(x_ref[...], g_ref[...]).astype(BF16)
    hb_ref[...] = pltpu.bitcast(hb, F32)
    tiles = x_ref.shape[0] // LANES
    for p in range(2):
        sc = _dot_nt(ws_ref[p], hb)
        sc = sc.reshape(PEER_N_KEYS, PEER_HEADS, tiles * LANES)
        for lt in range(tiles):
            s_ref[p, lt] = sc[:, :, lt * LANES:(lt + 1) * LANES]

    def tile(lt, _):
        _sel_tile(lt, s_ref, r_ref, t_ref, e1_ref, nk_ref, e2_ref, rb_ref)
        return 0

    lax.fori_loop(0, tiles, tile, 0)


def _peer_select(x2, gain, w_q, sub_keys):
    t, d = x2.shape
    tb = SEL_TOKENS
    tiles = tb // LANES
    ws = _fold_keys(sub_keys, w_q).transpose(0, 2, 1, 3).reshape(2, PEER_N_KEYS * PEER_HEADS, d).astype(BF16)
    key_shape = jax.ShapeDtypeStruct((t // LANES, PEER_N_KEYS, PEER_HEADS, LANES), F32)
    key_spec = pl.BlockSpec((tiles, PEER_N_KEYS, PEER_HEADS, LANES), lambda i: (i, 0, 0, 0))
    pair_shape = jax.ShapeDtypeStruct((PEER_HEADS, t // LANES, PEER_N_KEYS // 2, LANES), F32)
    pair_spec = pl.BlockSpec((PEER_HEADS, tiles, PEER_N_KEYS // 2, LANES), lambda i: (0, i, 0, 0))
    slab = (PEER_N_KEYS, PEER_HEADS, LANES)
    return pl.pallas_call(
        _peer_sel_body,
        out_shape=(jax.ShapeDtypeStruct((t // 2, d), F32), key_shape, key_shape, pair_shape, pair_shape),
        grid=(t // tb,),
        in_specs=[pl.BlockSpec((tb, d), lambda i: (i, 0)), _const_spec((1, d)), _const_spec(ws.shape)],
        out_specs=(pl.BlockSpec((tb // 2, d), lambda i: (i, 0)), key_spec, key_spec, pair_spec, pair_spec),
        scratch_shapes=[pltpu.VMEM((2, tiles) + slab, F32), pltpu.VMEM(slab, F32), pltpu.VMEM(slab, F32)],
        compiler_params=_cparams("parallel"),
        name="peer_select",
    )(x2, gain[None, :], ws)


def _key_rows_bf16(row):
    packed = jnp.broadcast_to(row, (BF16_SUBLANES, row.shape[1])).astype(BF16)
    return jnp.concatenate([packed] * (PEER_N_KEYS // BF16_SUBLANES), axis=0)


def _peer_main_body(final_norm, x_ref, hb_ref, u_ref, vt_ref, e1_ref, n_ref, e2_ref, rb_ref, fg_ref, o_ref, acc_ref,
                    a_ref):
    s = pl.program_id(1)

    @pl.when(s == 0)
    def _():
        acc_ref[...] = jnp.zeros_like(acc_ref)
        a_ref[...] = jnp.zeros_like(a_ref)

    piece_tiles = PEER_PIECE // LANES
    for p in range(acc_ref.shape[0]):
        g_tiles = []
        for tb in range(p * piece_tiles, (p + 1) * piece_tiles):
            g_keys = []
            for k0 in range(0, PEER_KEY_GROUP, PEER_KEY_REUSE):
                ks = range(k0, k0 + PEER_KEY_REUSE)
                w = {}
                for h in range(PEER_HEADS):
                    rb_t = pltpu.bitcast(rb_ref[h, tb], BF16)
                    e2_t = pltpu.bitcast(e2_ref[h, tb], BF16)
                    for k in ks:
                        n_b = _key_rows_bf16(n_ref[tb, k, h:h + 1, :])
                        e1_b = _key_rows_bf16(e1_ref[tb, k, h:h + 1, :])
                        term = jnp.where(rb_t < n_b, e2_t * e1_b, 0.0)
                        w[k] = term if h == 0 else w[k] + term
                for k in ks:
                    g_keys.append(w[k] * _gelu(a_ref[tb, k * PEER_N_KEYS:(k + 1) * PEER_N_KEYS, :].astype(BF16)))
            g_tiles.append(jnp.concatenate(g_keys, axis=0))
        acc_ref[p] += _dot(pltpu.bitcast(vt_ref[0], BF16), jnp.concatenate(g_tiles, axis=1))
        hb_piece = pltpu.bitcast(hb_ref[p * (PEER_PIECE // 2):(p + 1) * (PEER_PIECE // 2), :], BF16)
        a_piece = _dot_nt(pltpu.bitcast(u_ref[...], BF16), hb_piece)
        for t in range(piece_tiles):
            a_ref[p * piece_tiles + t] = a_piece[:, t * LANES:(t + 1) * LANES]

    @pl.when(s == pl.num_programs(1) - 1)
    def _():
        for p in range(acc_ref.shape[0]):
            rows = slice(p * PEER_PIECE, (p + 1) * PEER_PIECE)
            y = x_ref[rows, :] + acc_ref[p].T
            if final_norm:
                y = _rms(y, fg_ref[...])
            o_ref[rows, :] = y


def _pack_rows_body(x_ref, o_ref):
    o_ref[...] = pltpu.bitcast(x_ref[0].astype(BF16), F32)


def _pack_rows(tables, layer):
    _, rows, d = tables.shape
    blk = PACK_ROWS
    return pl.pallas_call(
        _pack_rows_body,
        out_shape=jax.ShapeDtypeStruct((rows // 2, d), F32),
        grid=(rows // blk,),
        in_specs=[pl.BlockSpec((1, blk, d), lambda i: (layer, i, 0))],
        out_specs=pl.BlockSpec((blk // 2, d), lambda i: (i, 0)),
        compiler_params=_cparams("parallel"),
        name="pack_rows",
    )(tables)


def _pack_transposed_body(x_ref, o_ref):
    o_ref[0] = pltpu.bitcast(x_ref[0].T.astype(BF16), F32)


def _pack_transposed_chunks(tables, layer, chunk):
    _, rows, d = tables.shape
    return pl.pallas_call(
        _pack_transposed_body,
        out_shape=jax.ShapeDtypeStruct((rows // chunk, d // 2, chunk), F32),
        grid=(rows // chunk,),
        in_specs=[pl.BlockSpec((1, chunk, d), lambda i: (layer, i, 0))],
        out_specs=pl.BlockSpec((1, d // 2, chunk), lambda i: (i, 0, 0)),
        compiler_params=_cparams("parallel"),
        name="pack_transposed",
    )(tables)


def _peer_experts(x2, hb, e1, n, e2, rb, u_tabs, v_tabs, layer, final_gain=None):
    t, d = x2.shape
    n_experts = u_tabs.shape[1]
    tb, eb = PEER_TOKENS, PEER_EXPERTS
    chunks = n_experts // eb
    final_norm = final_gain is not None
    fg = (final_gain if final_norm else jnp.ones((d,), F32))[None, :]
    once = dict(pipeline_mode=pl.Buffered(1))
    tok_spec = pl.BlockSpec((tb, d), lambda i, s: (i, 0), **once)
    hb_spec = pl.BlockSpec((tb // 2, d), lambda i, s: (i, 0), **once)
    sel_spec = pl.BlockSpec((PEER_HEADS, tb // LANES, PEER_N_KEYS // 2, LANES), lambda i, s: (0, i, 0, 0), **once)
    key_spec = pl.BlockSpec((tb // LANES, PEER_KEY_GROUP, PEER_HEADS, LANES),
                            lambda i, s: (i, jnp.clip(s - 1, 0, chunks - 1), 0, 0))
    return pl.pallas_call(
        functools.partial(_peer_main_body, final_norm),
        out_shape=jax.ShapeDtypeStruct((t, d), F32),
        grid=(t // tb, chunks + 1),
        in_specs=[
            tok_spec,
            hb_spec,
            pl.BlockSpec((eb // 2, d), lambda i, s: (jnp.minimum(s, chunks - 1), 0)),
            pl.BlockSpec((1, d // 2, eb), lambda i, s: (jnp.maximum(s - 1, 0), 0, 0)),
            key_spec, key_spec, sel_spec, sel_spec,
            _const_spec((1, d)),
        ],
        out_specs=pl.BlockSpec((tb, d), lambda i, s: (i, 0)),
        scratch_shapes=[pltpu.VMEM((tb // PEER_PIECE, d, PEER_PIECE), F32), pltpu.VMEM((tb // LANES, eb, LANES), F32)],
        compiler_params=_cparams("parallel", "arbitrary"),
        name="peer_experts",
    )(x2, hb, _pack_rows(u_tabs, layer), _pack_transposed_chunks(v_tabs, layer, eb), e1, n, e2, rb, fg)


def _peer(x2, gain, w_q, sub_keys, u_tabs, v_tabs, layer, final_gain=None):
    hb, e1, n, e2, rb = _peer_select(x2, gain, w_q, sub_keys)
    return _peer_experts(x2, hb, e1, n, e2, rb, u_tabs, v_tabs, layer, final_gain)


def kernel(x, norm_mix, norm_ffn, hy_w_in, gm_v_norm, gm_w_s, gm_b_s, ret_norm, hy_w_out, mla_w_down, mla_q_norm,
           mla_w_q_up, mla_kv_norm, mla_w_kv_up, mla_w_out, peer_w_q, peer_sub_keys, peer_u, peer_v, final_norm):
    b, s, d = x.shape
    depth = norm_mix.shape[0]
    for layer in range(depth):
        j = layer // 2
        if layer % 2 == 0:
            x = _hy_mixer(x, norm_mix[layer], hy_w_in[j], gm_v_norm[j], gm_w_s[j], gm_b_s[j], ret_norm[j], hy_w_out[j])
            x2 = x.reshape(b * s, d)
        else:
            q, k, v = _mla_proj(x, norm_mix[layer], mla_w_down[j], mla_q_norm[j], mla_w_q_up[j], mla_kv_norm[j],
                                mla_w_kv_up[j])
            o = _flash(q, k, v)
            x2 = _out_proj(x.reshape(b * s, d), o.reshape(b * s, -1), mla_w_out[j])
        last = layer == depth - 1
        x2 = _peer(x2, norm_ffn[layer], peer_w_q[layer], peer_sub_keys[layer], peer_u, peer_v, layer,
                   final_norm if last else None)
        x = x2.reshape(b, s, d)
    return x
```

```python
import functools
import math

import jax
import jax.numpy as jnp
from jax import lax
from jax.experimental import pallas as pl
from jax.experimental.pallas import tpu as pltpu

F32 = jnp.float32
BF16 = jnp.bfloat16

LANES = 128
BF16_SUBLANES = 16
MXU_WIDTH = 256
V7X_VMEM_BYTES = 64 * 1024 * 1024
VMEM_LIMIT_BYTES = 56 * 1024 * 1024

NORM_EPS = 1e-6
ROPE_THETA = 10000.0

GM_GROUPS = 4
GM_DIM = 256
GM_CHUNK = 128
GM_WIDTH = GM_GROUPS * GM_DIM
RET_HEADS = 4
RET_QK_DIM = 128
RET_V_DIM = 256
RET_CHUNK = 128
RET_QK_WIDTH = RET_HEADS * RET_QK_DIM
RET_V_WIDTH = RET_HEADS * RET_V_DIM
HY_OUT_WIDTH = GM_WIDTH + RET_V_WIDTH
MLA_HEADS = 8
MLA_Q_RANK = 384
MLA_KV_RANK = 256
MLA_NOPE_DIM = 128
MLA_ROPE_DIM = 64
MLA_V_DIM = 128
MLA_QK_DIM = MLA_NOPE_DIM + MLA_ROPE_DIM
MLA_QK_PAD = 2 * LANES
PEER_HEADS = 8
PEER_N_KEYS = 128
PEER_HALF = 128
PEER_TOPK = 16

HY_TOKENS = 512
MLA_TOKENS = 512
ATT_BLOCK = 512
ATT_KV_BLOCK = 2048
ATT_SUB_BLOCK = 1024
SEL_TOKENS = 512
PEER_TOKENS = 1024
PEER_EXPERTS = 1024
PEER_KEY_GROUP = PEER_EXPERTS // PEER_N_KEYS
PEER_KEY_REUSE = 2
PEER_PIECE = MXU_WIDTH
PACK_ROWS = 1024


def _cparams(*semantics):
    return pltpu.CompilerParams(dimension_semantics=semantics, vmem_limit_bytes=VMEM_LIMIT_BYTES)


def _const_spec(shape):
    return pl.BlockSpec(shape, lambda *_: (0,) * len(shape))


def _rms(x, g):
    return x * lax.rsqrt(jnp.mean(x * x, axis=-1, keepdims=True) + NORM_EPS) * g


def _gelu(x):
    return 0.5 * x * (1.0 + lax.erf(x * (2.0 ** -0.5)))


def _dot(a, b):
    return jnp.dot(a, b, preferred_element_type=F32)


def _dot_nt(a, b):
    return lax.dot_general(a, b, (((1,), (1,)), ((), ())), preferred_element_type=F32)


def _dot_tn(a, b):
    return lax.dot_general(a, b, (((0,), (0,)), ((), ())), preferred_element_type=F32)


def _hy_body(x_ref, gmix_ref, win_ref, vnorm_ref, ws_ref, bs_ref, cos_ref, sin_ref, dmask_ref, qdec_ref, kdec_ref,
             cdec_ref, rnorm_ref, wout_ref, o_ref, state_ref, y_ref):
    @pl.when(pl.program_id(1) == 0)
    def _():
        state_ref[...] = jnp.zeros_like(state_ref)

    x = x_ref[0]
    h = _rms(x, gmix_ref[...]).astype(BF16)
    tokens = x.shape[0]
    chunks = tokens // GM_CHUNK

    u_all = _gelu(_dot(h, win_ref[:, 0:GM_WIDTH]))
    v_all = _gelu(_dot(h, win_ref[:, GM_WIDTH:2 * GM_WIDTH]))
    row = lax.broadcasted_iota(jnp.int32, (GM_CHUNK, GM_CHUNK), 0)
    col = lax.broadcasted_iota(jnp.int32, (GM_CHUNK, GM_CHUNK), 1)
    causal = row >= col
    for g in range(GM_GROUPS):
        cols = slice(g * GM_DIM, (g + 1) * GM_DIM)
        vg = _rms(v_all[:, cols], vnorm_ref[:, cols]).astype(BF16)
        wg = jnp.where(causal, ws_ref[g], 0.0).astype(BF16)
        for c in range(chunks):
            rows = slice(c * GM_CHUNK, (c + 1) * GM_CHUNK)
            mixed = _dot(wg, vg[rows]) + bs_ref[g]
            y_ref[rows, cols] = (u_all[rows, cols] * mixed).astype(BF16)

    base = 2 * GM_WIDTH
    q_all = _dot(h, win_ref[:, base:base + RET_QK_WIDTH])
    k_all = _dot(h, win_ref[:, base + RET_QK_WIDTH:base + 2 * RET_QK_WIDTH])
    base += 2 * RET_QK_WIDTH
    v_all = _dot(h, win_ref[:, base:base + RET_V_WIDTH])
    g_all = _dot(h, win_ref[:, base + RET_V_WIDTH:base + 2 * RET_V_WIDTH])
    cos = cos_ref[...]
    sin = sin_ref[...]
    for hd in range(RET_HEADS):
        qk_cols = slice(hd * RET_QK_DIM, (hd + 1) * RET_QK_DIM)
        v_cols = slice(hd * RET_V_DIM, (hd + 1) * RET_V_DIM)
        q = q_all[:, qk_cols]
        k = k_all[:, qk_cols]
        q = q * cos + pltpu.roll(q, RET_QK_DIM // 2, 1) * sin
        k = (k * cos + pltpu.roll(k, RET_QK_DIM // 2, 1) * sin) * (RET_QK_DIM ** -0.5)
        for c in range(chunks):
            rows = slice(c * RET_CHUNK, (c + 1) * RET_CHUNK)
            qc, kc, vc = q[rows], k[rows], v_all[rows, v_cols]
            scores = _dot_nt(qc.astype(BF16), kc.astype(BF16)) * dmask_ref[hd]
            intra = _dot(scores.astype(BF16), vc.astype(BF16))
            state = state_ref[hd]
            cross = _dot((qc * qdec_ref[hd]).astype(BF16), state.astype(BF16))
            state_ref[hd] = state * cdec_ref[hd] + _dot_tn(kc.astype(BF16), (vc * kdec_ref[hd]).astype(BF16))
            gate = g_all[rows, v_cols]
            yb = _rms(intra + cross, rnorm_ref[:, v_cols]) * (gate * jax.nn.sigmoid(gate))
            y_ref[rows, GM_WIDTH + hd * RET_V_DIM:GM_WIDTH + (hd + 1) * RET_V_DIM] = yb.astype(BF16)

    o_ref[0] = x + _dot(y_ref[...], wout_ref[...])


def _rope_angles(seq, half):
    inv = 1.0 / (ROPE_THETA ** (jnp.arange(half, dtype=F32) / half))
    ang = jnp.arange(seq, dtype=F32)[:, None] * inv[None, :]
    return jnp.cos(ang), jnp.sin(ang)


def _hy_mixer(x, gmix, w_in, v_norm, w_s, b_s, ret_norm, w_out):
    b, s, d = x.shape
    ts = HY_TOKENS
    cos, sin = _rope_angles(s, RET_QK_DIM // 2)
    cos2 = jnp.concatenate([cos, cos], axis=1)
    sin2 = jnp.concatenate([-sin, sin], axis=1)
    log_gamma = jnp.log(1.0 - 2.0 ** (-5.0 - jnp.arange(RET_HEADS, dtype=F32)))
    pos = jnp.arange(RET_CHUNK, dtype=F32)
    diff = pos[:, None] - pos[None, :]
    dmask = jnp.where(diff[None] >= 0, jnp.exp(diff[None] * log_gamma[:, None, None]), 0.0)
    qdec = jnp.exp((pos[None, :] + 1.0) * log_gamma[:, None])[:, :, None]
    kdec = jnp.exp((RET_CHUNK - 1.0 - pos[None, :]) * log_gamma[:, None])[:, :, None]
    cdec = jnp.exp(RET_CHUNK * log_gamma)[:, None, None]
    in_width = w_in.shape[1]
    return pl.pallas_call(
        _hy_body,
        out_shape=jax.ShapeDtypeStruct((b, s, d), F32),
        grid=(b, s // ts),
        in_specs=[
            pl.BlockSpec((1, ts, d), lambda i, j: (i, j, 0)),
            _const_spec((1, d)),
            _const_spec((d, in_width)),
            _const_spec((1, GM_WIDTH)),
            _const_spec((GM_GROUPS, GM_CHUNK, GM_CHUNK)),
            _const_spec((GM_GROUPS, GM_CHUNK, 1)),
            pl.BlockSpec((ts, RET_QK_DIM), lambda i, j: (j, 0)),
            pl.BlockSpec((ts, RET_QK_DIM), lambda i, j: (j, 0)),
            _const_spec((RET_HEADS, RET_CHUNK, RET_CHUNK)),
            _const_spec((RET_HEADS, RET_CHUNK, 1)),
            _const_spec((RET_HEADS, RET_CHUNK, 1)),
            _const_spec((RET_HEADS, 1, 1)),
            _const_spec((1, RET_V_WIDTH)),
            _const_spec((HY_OUT_WIDTH, d)),
        ],
        out_specs=pl.BlockSpec((1, ts, d), lambda i, j: (i, j, 0)),
        scratch_shapes=[
            pltpu.VMEM((RET_HEADS, RET_QK_DIM, RET_V_DIM), F32),
            pltpu.VMEM((ts, HY_OUT_WIDTH), BF16),
        ],
        compiler_params=_cparams("parallel", "arbitrary"),
        name="hy_mixer",
    )(x, gmix[None, :], w_in.astype(BF16), v_norm[None, :], w_s, b_s[:, :, None], cos2, sin2, dmask, qdec, kdec, cdec,
      ret_norm[None, :], w_out.astype(BF16))


def _mla_proj_body(x_ref, gmix_ref, wdown_ref, qn_ref, kvn_ref, wq_ref, wqrot_ref, wk_ref, wv_ref, cq_ref, sq_ref,
                   ck_ref, sk_ref, q_ref, k_ref, v_ref):
    h = _rms(x_ref[0], gmix_ref[...]).astype(BF16)
    down = _dot(h, wdown_ref[...])
    kv_end = MLA_Q_RANK + MLA_KV_RANK
    cq = _rms(down[:, :MLA_Q_RANK], qn_ref[...]).astype(BF16)
    ckv = _rms(down[:, MLA_Q_RANK:kv_end], kvn_ref[...]).astype(BF16)
    k_rope = (down[:, kv_end:kv_end + LANES] * ck_ref[...] + down[:, kv_end + LANES:kv_end + 2 * LANES] * sk_ref[...])
    k_rope = k_rope.astype(BF16)
    qa = _dot(cq, wq_ref[...])
    qb = _dot(cq, wqrot_ref[...])
    kn = _dot(ckv, wk_ref[...])
    scale = MLA_QK_DIM ** -0.5
    cq_tab = cq_ref[...] * scale
    sq_tab = sq_ref[...] * scale
    for hd in range(MLA_HEADS):
        seg = slice(hd * MLA_QK_PAD, (hd + 1) * MLA_QK_PAD)
        q_ref[0, :, seg] = (qa[:, seg] * cq_tab + qb[:, seg] * sq_tab).astype(BF16)
        k_ref[0, :, hd * MLA_QK_PAD:hd * MLA_QK_PAD + MLA_NOPE_DIM] = (
            kn[:, hd * MLA_NOPE_DIM:(hd + 1) * MLA_NOPE_DIM].astype(BF16))
        k_ref[0, :, hd * MLA_QK_PAD + MLA_NOPE_DIM:(hd + 1) * MLA_QK_PAD] = k_rope
    v_ref[0] = _dot(ckv, wv_ref[...]).astype(BF16)


def _rot_half_cols(w):
    half = w.shape[-1] // 2
    return jnp.concatenate([-w[..., half:], w[..., :half]], axis=-1)


def _mla_proj(x, gmix, w_down, q_norm, w_q_up, kv_norm, w_kv_up):
    b, s, d = x.shape
    tm = MLA_TOKENS
    kv_end = MLA_Q_RANK + MLA_KV_RANK
    pad = MLA_QK_PAD - MLA_QK_DIM
    w_kr = w_down[:, kv_end:]
    zk = jnp.zeros((d, LANES - MLA_ROPE_DIM), F32)
    w_down_x = jnp.concatenate([w_down[:, :kv_end], w_kr, zk, _rot_half_cols(w_kr), zk], axis=1).astype(BF16)
    wq = w_q_up.reshape(MLA_Q_RANK, MLA_HEADS, MLA_QK_DIM)
    zq = jnp.zeros((MLA_Q_RANK, MLA_HEADS, pad), F32)
    wq_a = jnp.concatenate([wq, zq], axis=2).reshape(MLA_Q_RANK, MLA_HEADS * MLA_QK_PAD).astype(BF16)
    wq_b = jnp.concatenate([jnp.zeros((MLA_Q_RANK, MLA_HEADS, MLA_NOPE_DIM), F32),
                            _rot_half_cols(wq[:, :, MLA_NOPE_DIM:]), zq], axis=2)
    wq_b = wq_b.reshape(MLA_Q_RANK, MLA_HEADS * MLA_QK_PAD).astype(BF16)
    wkv = w_kv_up.reshape(MLA_KV_RANK, MLA_HEADS, MLA_NOPE_DIM + MLA_V_DIM)
    wk = wkv[:, :, :MLA_NOPE_DIM].reshape(MLA_KV_RANK, MLA_HEADS * MLA_NOPE_DIM).astype(BF16)
    wv = wkv[:, :, MLA_NOPE_DIM:].reshape(MLA_KV_RANK, MLA_HEADS * MLA_V_DIM).astype(BF16)
    cos, sin = _rope_angles(s, MLA_ROPE_DIM // 2)
    zt = jnp.zeros((s, pad), F32)
    cq_tab = jnp.concatenate([jnp.ones((s, MLA_NOPE_DIM), F32), cos, cos, zt], axis=1)
    sq_tab = jnp.concatenate([jnp.zeros((s, MLA_NOPE_DIM), F32), sin, sin, zt], axis=1)
    ck_tab = jnp.concatenate([cos, cos, zt], axis=1)
    sk_tab = jnp.concatenate([sin, sin, zt], axis=1)
    qk_width = MLA_HEADS * MLA_QK_PAD
    v_width = MLA_HEADS * MLA_V_DIM
    tab = lambda w: pl.BlockSpec((tm, w), lambda i, j: (j, 0))
    out = lambda w: pl.BlockSpec((1, tm, w), lambda i, j: (i, j, 0))
    return pl.pallas_call(
        _mla_proj_body,
        out_shape=(jax.ShapeDtypeStruct((b, s, qk_width), BF16), jax.ShapeDtypeStruct((b, s, qk_width), BF16),
                   jax.ShapeDtypeStruct((b, s, v_width), BF16)),
        grid=(b, s // tm),
        in_specs=[
            out(d),
            _const_spec((1, d)),
            _const_spec(w_down_x.shape),
            _const_spec((1, MLA_Q_RANK)),
            _const_spec((1, MLA_KV_RANK)),
            _const_spec(wq_a.shape),
            _const_spec(wq_b.shape),
            _const_spec(wk.shape),
            _const_spec(wv.shape),
            tab(MLA_QK_PAD), tab(MLA_QK_PAD), tab(LANES), tab(LANES),
        ],
        out_specs=(out(qk_width), out(qk_width), out(v_width)),
        compiler_params=_cparams("parallel", "parallel"),
        name="mla_proj",
    )(x, gmix[None, :], w_down_x, q_norm[None, :], kv_norm[None, :], wq_a, wq_b, wk, wv, cq_tab, sq_tab, ck_tab, sk_tab)


def _flash_body(q_ref, k_ref, v_ref, o_ref):
    blk = q_ref.shape[1]
    kvb = ATT_KV_BLOCK
    i = pl.program_id(2)
    q = q_ref[0]
    full = (i * blk) // kvb

    def step(start, width, carry, masked):
        m, l, acc = carry
        offs = range(0, width, ATT_SUB_BLOCK)
        widths = [min(ATT_SUB_BLOCK, width - off) for off in offs]
        scores = [_dot_nt(q, k_ref[0, pl.ds(start + off, w), :]) for off, w in zip(offs, widths)]
        for off, w, s in zip(offs, widths, scores):
            if masked:
                row = lax.broadcasted_iota(jnp.int32, s.shape, 0) + i * blk
                col = lax.broadcasted_iota(jnp.int32, s.shape, 1) + (start + off)
                s = jnp.where(row >= col, s, -jnp.inf)
            m_new = jnp.maximum(m, jnp.max(s, axis=1, keepdims=True))
            p = jnp.exp(s - m_new)
            alpha = jnp.exp(m - m_new)
            l = alpha * l + jnp.sum(p, axis=1, keepdims=True)
            acc = alpha * acc + _dot(p.astype(BF16), v_ref[0, pl.ds(start + off, w), :])
            m = m_new
        return m, l, acc

    init = (jnp.full((blk, 1), -jnp.inf, F32), jnp.zeros((blk, 1), F32), jnp.zeros((blk, v_ref.shape[2]), F32))
    carry = lax.fori_loop(0, full, lambda j, c: step(pl.multiple_of(j * kvb, kvb), kvb, c, False), init)
    for rest in range(1, kvb // blk + 1):
        @pl.when((i + 1) * blk - full * kvb == rest * blk)
        def _():
            _, l, acc = step(pl.multiple_of(full * kvb, kvb), rest * blk, carry, True)
            o_ref[0] = (acc / l).astype(BF16)


def _flash(q, k, v):
    b, s, _ = q.shape
    blk = ATT_BLOCK
    return pl.pallas_call(
        _flash_body,
        out_shape=jax.ShapeDtypeStruct((b, s, MLA_HEADS * MLA_V_DIM), BF16),
        grid=(b, MLA_HEADS, s // blk),
        in_specs=[
            pl.BlockSpec((1, blk, MLA_QK_PAD), lambda bi, h, i: (bi, i, h)),
            pl.BlockSpec((1, s, MLA_QK_PAD), lambda bi, h, i: (bi, 0, h)),
            pl.BlockSpec((1, s, MLA_V_DIM), lambda bi, h, i: (bi, 0, h)),
        ],
        out_specs=pl.BlockSpec((1, blk, MLA_V_DIM), lambda bi, h, i: (bi, i, h)),
        compiler_params=_cparams("parallel", "parallel", "arbitrary"),
        name="mla_flash",
    )(q, k, v)


def _out_proj_body(x_ref, o_ref, w_ref, y_ref):
    y_ref[...] = x_ref[...] + _dot(o_ref[...], w_ref[...])


def _out_proj(x2, o2, w_out):
    t, d = x2.shape
    tm = MLA_TOKENS
    return pl.pallas_call(
        _out_proj_body,
        out_shape=jax.ShapeDtypeStruct((t, d), F32),
        grid=(t // tm,),
        in_specs=[pl.BlockSpec((tm, d), lambda i: (i, 0)), pl.BlockSpec((tm, o2.shape[1]), lambda i: (i, 0)),
                  _const_spec(w_out.shape)],
        out_specs=pl.BlockSpec((tm, d), lambda i: (i, 0)),
        compiler_params=_cparams("parallel"),
        name="mla_out",
    )(x2, o2, w_out.astype(BF16))


def _sort_network(n):
    def merge(lo, hi, r):
        step = 2 * r
        if step < hi - lo:
            yield from merge(lo, hi, step)
            yield from merge(lo + r, hi, step)
            yield from ((i, i + r) for i in range(lo + r, hi - r, step))
        else:
            yield (lo, lo + r)

    def sort(lo, hi):
        if hi - lo >= 1:
            mid = lo + (hi - lo) // 2
            yield from sort(lo, mid)
            yield from sort(mid + 1, hi)
            yield from merge(lo, hi, 1)

    return tuple(sort(0, n - 1))


_SORT_TOPK = _sort_network(PEER_TOPK)


def _top_sorted(slabs):
    groups = []
    for g0 in range(0, len(slabs), PEER_TOPK):
        v = list(slabs[g0:g0 + PEER_TOPK])
        for i, j in _SORT_TOPK:
            v[i], v[j] = jnp.maximum(v[i], v[j]), jnp.minimum(v[i], v[j])
        groups.append(v)
    while len(groups) > 1:
        merged = []
        for a, b in zip(groups[0::2], groups[1::2]):
            v = [jnp.maximum(a[i], b[PEER_TOPK - 1 - i]) for i in range(PEER_TOPK)]
            d = PEER_TOPK // 2
            while d >= 1:
                for i in range(PEER_TOPK):
                    if not i & d:
                        v[i], v[i + d] = jnp.maximum(v[i], v[i + d]), jnp.minimum(v[i], v[i + d])
                d //= 2
            merged.append(v)
        groups = merged
    return groups[0]


def _merge_counts(s1, s2):
    n = [jnp.zeros_like(s1[0]) for _ in range(PEER_TOPK)]
    front = [s1[a] + s2[0] for a in range(PEER_TOPK)]
    for _ in range(PEER_TOPK):
        m = functools.reduce(jnp.maximum, front)
        first = jnp.full_like(m, float(PEER_TOPK))
        for a in reversed(range(PEER_TOPK)):
            first = jnp.where(front[a] == m, float(a), first)
        hits = [first == float(a) for a in range(PEER_TOPK)]
        n_hit = jnp.zeros_like(m)
        for a in range(PEER_TOPK):
            n_hit = jnp.where(hits[a], n[a], n_hit)
        nxt = jnp.full_like(m, -jnp.inf)
        for b in range(1, PEER_TOPK):
            nxt = jnp.where(n_hit == float(b - 1), s2[b], nxt)
        for a in range(PEER_TOPK):
            n[a] = jnp.where(hits[a], n[a] + 1.0, n[a])
            front[a] = jnp.where(hits[a], s1[a] + nxt, front[a])
    return n


_RANK_CHECKSUM = float(sum(range(PEER_TOPK)) + (PEER_N_KEYS - PEER_TOPK) * PEER_TOPK)


def _sel_tile(lt, s_ref, r_ref, t_ref, e1_ref, nk_ref, e2_ref, rb_ref):
    keys = range(PEER_N_KEYS)
    top = [_top_sorted([s_ref[p, lt, k] for k in keys]) for p in range(2)]
    n = _merge_counts(top[0], top[1])

    e1s = [jnp.exp(v - top[0][0]) for v in top[0]]
    e2s = [jnp.exp(v - top[1][0]) for v in top[1]]
    z = jnp.zeros_like(e1s[0])
    for b in range(PEER_TOPK):
        row = jnp.zeros_like(z)
        for a in range(PEER_TOPK):
            row = row + jnp.where(n[a] > float(b), e1s[a], 0.0)
        z = z + row * e2s[b]
    inv_z = 1.0 / z

    def rank(s, sorted_vals):
        r = jnp.full_like(s, float(PEER_TOPK))
        for a in reversed(range(PEER_TOPK)):
            r = jnp.where(s >= sorted_vals[a], float(a), r)
        return r

    def shifted(r, seen):
        seen = list(seen)
        back = jnp.zeros_like(r)
        for a in range(PEER_TOPK):
            hit = r == float(a)
            back = jnp.where(hit, seen[a], back)
            seen[a] = jnp.where(hit, seen[a] + 1.0, seen[a])
        return jnp.minimum(r + back, float(PEER_TOPK)), tuple(seen)

    none_seen = tuple(jnp.zeros_like(z) for _ in range(PEER_TOPK))

    def any_lane(mask):
        return jnp.max(jnp.where(mask, 1.0, 0.0)) > 0.0

    total = jnp.zeros_like(z)
    for k in keys:
        r = rank(s_ref[1, lt, k], top[1])
        r_ref[k] = r
        total = total + r
        t_ref[k] = jnp.exp(s_ref[1, lt, k] - top[1][0])

    @pl.when(any_lane(total != _RANK_CHECKSUM))
    def _():
        def fix(k, seen):
            r_ref[k], seen = shifted(r_ref[k], seen)
            return seen

        lax.fori_loop(0, PEER_N_KEYS, fix, none_seen)

    count = jnp.zeros_like(z)
    for k in keys:
        s = s_ref[0, lt, k]
        nk = jnp.zeros_like(s)
        for a in reversed(range(PEER_TOPK)):
            nk = jnp.where(s >= top[0][a], n[a], nk)
        nk_ref[lt, k] = nk
        e1_ref[lt, k] = jnp.exp(s - top[0][0]) * inv_z
        count = count + jnp.where(s >= top[0][PEER_TOPK - 1], 1.0, 0.0)
    tied = count != float(PEER_TOPK)
    for a in range(PEER_TOPK - 1):
        tied = tied | (top[0][a] == top[0][a + 1])

    @pl.when(any_lane(tied))
    def _():
        def fix(k, seen):
            r, seen = shifted(rank(s_ref[0, lt, k], top[0]), seen)
            nk = jnp.zeros_like(r)
            for a in range(PEER_TOPK):
                nk = jnp.where(r == float(a), n[a], nk)
            nk_ref[lt, k] = nk
            return seen

        lax.fori_loop(0, PEER_N_KEYS, fix, none_seen)

    e2p = pltpu.bitcast(jnp.swapaxes(t_ref[...], 0, 1).astype(BF16), F32)
    rbp = pltpu.bitcast(jnp.swapaxes(r_ref[...], 0, 1).astype(BF16), F32)
    for h in range(PEER_HEADS):
        e2_ref[h, lt] = e2p[h]
        rb_ref[h, lt] = rbp[h]


def _fold_keys_body(keys_ref, wq_ref, o_ref):
    o_ref[0, 0] = jnp.dot(keys_ref[0], wq_ref[0, 0], preferred_element_type=F32, precision=lax.Precision.HIGHEST)


def _fold_keys(sub_keys, w_q):
    d = w_q.shape[0]
    wq_t = w_q.reshape(d, PEER_HEADS, 2, PEER_HALF).transpose(2, 1, 3, 0)
    return pl.pallas_call(
        _fold_keys_body,
        out_shape=jax.ShapeDtypeStruct((2, PEER_HEADS, PEER_N_KEYS, d), F32),
        grid=(2, PEER_HEADS),
        in_specs=[pl.BlockSpec((1, PEER_N_KEYS, PEER_HALF), lambda p, h: (p, 0, 0)),
                  pl.BlockSpec((1, 1, PEER_HALF, d), lambda p, h: (p, h, 0, 0))],
        out_specs=pl.BlockSpec((1, 1, PEER_N_KEYS, d), lambda p, h: (p, h, 0, 0)),
        compiler_params=_cparams("parallel", "parallel"),
        name="fold_keys",
    )(sub_keys, wq_t)


def _peer_sel_body(x_ref, g_ref, ws_ref, hb_ref, e1_ref, nk_ref, e2_ref, rb_ref, s_ref, r_ref, t_ref):
    hb = _rms(x_ref[...], g_ref[...]).astype(BF16)
    hb_ref[...] = pltpu.bitcast(hb, F32)
    tiles = x_ref.shape[0] // LANES
    for p in range(2):
        sc = _dot_nt(ws_ref[p], hb)
        sc = sc.reshape(PEER_N_KEYS, PEER_HEADS, tiles * LANES)
        for lt in range(tiles):
            s_ref[p, lt] = sc[:, :, lt * LANES:(lt + 1) * LANES]

    def tile(lt, _):
        _sel_tile(lt, s_ref, r_ref, t_ref, e1_ref, nk_ref, e2_ref, rb_ref)
        return 0

    lax.fori_loop(0, tiles, tile, 0)


def _peer_select(x2, gain, w_q, sub_keys):
    t, d = x2.shape
    tb = SEL_TOKENS
    tiles = tb // LANES
    ws = _fold_keys(sub_keys, w_q).transpose(0, 2, 1, 3).reshape(2, PEER_N_KEYS * PEER_HEADS, d).astype(BF16)
    key_shape = jax.ShapeDtypeStruct((t // LANES, PEER_N_KEYS, PEER_HEADS, LANES), F32)
    key_spec = pl.BlockSpec((tiles, PEER_N_KEYS, PEER_HEADS, LANES), lambda i: (i, 0, 0, 0))
    pair_shape = jax.ShapeDtypeStruct((PEER_HEADS, t // LANES, PEER_N_KEYS // 2, LANES), F32)
    pair_spec = pl.BlockSpec((PEER_HEADS, tiles, PEER_N_KEYS // 2, LANES), lambda i: (0, i, 0, 0))
    slab = (PEER_N_KEYS, PEER_HEADS, LANES)
    return pl.pallas_call(
        _peer_sel_body,
        out_shape=(jax.ShapeDtypeStruct((t // 2, d), F32), key_shape, key_shape, pair_shape, pair_shape),
        grid=(t // tb,),
        in_specs=[pl.BlockSpec((tb, d), lambda i: (i, 0)), _const_spec((1, d)), _const_spec(ws.shape)],
        out_specs=(pl.BlockSpec((tb // 2, d), lambda i: (i, 0)), key_spec, key_spec, pair_spec, pair_spec),
        scratch_shapes=[pltpu.VMEM((2, tiles) + slab, F32), pltpu.VMEM(slab, F32), pltpu.VMEM(slab, F32)],
        compiler_params=_cparams("parallel"),
        name="peer_select",
    )(x2, gain[None, :], ws)


def _key_rows_bf16(row):
    packed = jnp.broadcast_to(row, (BF16_SUBLANES, row.shape[1])).astype(BF16)
    return jnp.concatenate([packed] * (PEER_N_KEYS // BF16_SUBLANES), axis=0)


def _peer_main_body(final_norm, x_ref, hb_ref, u_ref, vt_ref, e1_ref, n_ref, e2_ref, rb_ref, fg_ref, o_ref, acc_ref,
                    a_ref):
    s = pl.program_id(1)
    last = pl.num_programs(1) - 1
    piece_tiles = PEER_PIECE // LANES
    pieces = range(acc_ref.shape[0])

    def gates_and_values(p):
        g_tiles = []
        for tb in range(p * piece_tiles, (p + 1) * piece_tiles):
            g_keys = []
            for k0 in range(0, PEER_KEY_GROUP, PEER_KEY_REUSE):
                ks = range(k0, k0 + PEER_KEY_REUSE)
                w = {}
                for h in range(PEER_HEADS):
                    rb_t = pltpu.bitcast(rb_ref[h, tb], BF16)
                    e2_t = pltpu.bitcast(e2_ref[h, tb], BF16)
                    for k in ks:
                        n_b = _key_rows_bf16(n_ref[tb, k, h:h + 1, :])
                        e1_b = _key_rows_bf16(e1_ref[tb, k, h:h + 1, :])
                        term = jnp.where(rb_t < n_b, e2_t * e1_b, 0.0)
                        w[k] = term if h == 0 else w[k] + term
                for k in ks:
                    g_keys.append(w[k] * _gelu(a_ref[tb, k * PEER_N_KEYS:(k + 1) * PEER_N_KEYS, :].astype(BF16)))
            g_tiles.append(jnp.concatenate(g_keys, axis=0))
        acc_ref[p] += _dot(pltpu.bitcast(vt_ref[0], BF16), jnp.concatenate(g_tiles, axis=1))

    def activations(p):
        hb_piece = pltpu.bitcast(hb_ref[p * (PEER_PIECE // 2):(p + 1) * (PEER_PIECE // 2), :], BF16)
        a_piece = _dot_nt(pltpu.bitcast(u_ref[...], BF16), hb_piece)
        for t in range(piece_tiles):
            a_ref[p * piece_tiles + t] = a_piece[:, t * LANES:(t + 1) * LANES]

    @pl.when(s == 0)
    def _():
        acc_ref[...] = jnp.zeros_like(acc_ref)
        for p in pieces:
            activations(p)

    @pl.when((s > 0) & (s < last))
    def _():
        for p in pieces:
            gates_and_values(p)
            activations(p)

    @pl.when(s == last)
    def _():
        for p in pieces:
            gates_and_values(p)
        for p in pieces:
            rows = slice(p * PEER_PIECE, (p + 1) * PEER_PIECE)
            y = x_ref[rows, :] + acc_ref[p].T
            if final_norm:
                y = _rms(y, fg_ref[...])
            o_ref[rows, :] = y


def _pack_rows_body(x_ref, o_ref):
    o_ref[...] = pltpu.bitcast(x_ref[0].astype(BF16), F32)


def _pack_rows(tables, layer):
    _, rows, d = tables.shape
    blk = PACK_ROWS
    return pl.pallas_call(
        _pack_rows_body,
        out_shape=jax.ShapeDtypeStruct((rows // 2, d), F32),
        grid=(rows // blk,),
        in_specs=[pl.BlockSpec((1, blk, d), lambda i: (layer, i, 0))],
        out_specs=pl.BlockSpec((blk // 2, d), lambda i: (i, 0)),
        compiler_params=_cparams("parallel"),
        name="pack_rows",
    )(tables)


def _pack_transposed_body(x_ref, o_ref):
    o_ref[0] = pltpu.bitcast(x_ref[0].T.astype(BF16), F32)


def _pack_transposed_chunks(tables, layer, chunk):
    _, rows, d = tables.shape
    return pl.pallas_call(
        _pack_transposed_body,
        out_shape=jax.ShapeDtypeStruct((rows // chunk, d // 2, chunk), F32),
        grid=(rows // chunk,),
        in_specs=[pl.BlockSpec((1, chunk, d), lambda i: (layer, i, 0))],
        out_specs=pl.BlockSpec((1, d // 2, chunk), lambda i: (i, 0, 0)),
        compiler_params=_cparams("parallel"),
        name="pack_transposed",
    )(tables)


def _peer_experts(x2, hb, e1, n, e2, rb, u_tabs, v_tabs, layer, final_gain=None):
    t, d = x2.shape
    n_experts = u_tabs.shape[1]
    tb, eb = PEER_TOKENS, PEER_EXPERTS
    chunks = n_experts // eb
    final_norm = final_gain is not None
    fg = (final_gain if final_norm else jnp.ones((d,), F32))[None, :]
    tok_spec = pl.BlockSpec((tb, d), lambda i, s: (i, 0))
    hb_spec = pl.BlockSpec((tb // 2, d), lambda i, s: (i, 0))
    sel_spec = pl.BlockSpec((PEER_HEADS, tb // LANES, PEER_N_KEYS // 2, LANES), lambda i, s: (0, i, 0, 0))
    key_spec = pl.BlockSpec((tb // LANES, PEER_KEY_GROUP, PEER_HEADS, LANES),
                            lambda i, s: (i, jnp.clip(s - 1, 0, chunks - 1), 0, 0))
    return pl.pallas_call(
        functools.partial(_peer_main_body, final_norm),
        out_shape=jax.ShapeDtypeStruct((t, d), F32),
        grid=(t // tb, chunks + 1),
        in_specs=[
            tok_spec,
            hb_spec,
            pl.BlockSpec((eb // 2, d), lambda i, s: (jnp.minimum(s, chunks - 1), 0)),
            pl.BlockSpec((1, d // 2, eb), lambda i, s: (jnp.maximum(s - 1, 0), 0, 0)),
            key_spec, key_spec, sel_spec, sel_spec,
            _const_spec((1, d)),
        ],
        out_specs=pl.BlockSpec((tb, d), lambda i, s: (i, 0)),
        scratch_shapes=[pltpu.VMEM((tb // PEER_PIECE, d, PEER_PIECE), F32), pltpu.VMEM((tb // LANES, eb, LANES), F32)],
        compiler_params=_cparams("parallel", "arbitrary"),
        name="peer_experts",
    )(x2, hb, _pack_rows(u_tabs, layer), _pack_transposed_chunks(v_tabs, layer, eb), e1, n, e2, rb, fg)


def _peer(x2, gain, w_q, sub_keys, u_tabs, v_tabs, layer, final_gain=None):
    hb, e1, n, e2, rb = _peer_select(x2, gain, w_q, sub_keys)
    return _peer_experts(x2, hb, e1, n, e2, rb, u_tabs, v_tabs, layer, final_gain)


def kernel(x, norm_mix, norm_ffn, hy_w_in, gm_v_norm, gm_w_s, gm_b_s, ret_norm, hy_w_out, mla_w_down, mla_q_norm,
           mla_w_q_up, mla_kv_norm, mla_w_kv_up, mla_w_out, peer_w_q, peer_sub_keys, peer_u, peer_v, final_norm):
    b, s, d = x.shape
    depth = norm_mix.shape[0]
    for layer in range(depth):
        j = layer // 2
        if layer % 2 == 0:
            x = _hy_mixer(x, norm_mix[layer], hy_w_in[j], gm_v_norm[j], gm_w_s[j], gm_b_s[j], ret_norm[j], hy_w_out[j])
            x2 = x.reshape(b * s, d)
        else:
            q, k, v = _mla_proj(x, norm_mix[layer], mla_w_down[j], mla_q_norm[j], mla_w_q_up[j], mla_kv_norm[j],
                                mla_w_kv_up[j])
            o = _flash(q, k, v)
            x2 = _out_proj(x.reshape(b * s, d), o.reshape(b * s, -1), mla_w_out[j])
        last = layer == depth - 1
        x2 = _peer(x2, norm_ffn[layer], peer_w_q[layer], peer_sub_keys[layer], peer_u, peer_v, layer,
                   final_norm if last else None)
        x = x2.reshape(b, s, d)
    return x
```

```python
import functools

import jax
import jax.numpy as jnp
from jax import lax
from jax.experimental import pallas as pl
from jax.experimental.pallas import tpu as pltpu

F32 = jnp.float32
BF16 = jnp.bfloat16

LANES = 128
BF16_SUBLANES = 16
MXU_WIDTH = 256
V7X_VMEM_BYTES = 64 * 1024 * 1024
VMEM_LIMIT_BYTES = V7X_VMEM_BYTES - 8 * 1024 * 1024

NORM_EPS = 1e-6
ROPE_THETA = 10000.0

GM_GROUPS = 4
GM_DIM = 256
GM_CHUNK = 128
GM_WIDTH = GM_GROUPS * GM_DIM
RET_HEADS = 4
RET_QK_DIM = 128
RET_V_DIM = 256
RET_CHUNK = 128
RET_QK_WIDTH = RET_HEADS * RET_QK_DIM
RET_V_WIDTH = RET_HEADS * RET_V_DIM
HY_OUT_WIDTH = GM_WIDTH + RET_V_WIDTH
MLA_HEADS = 8
MLA_Q_RANK = 384
MLA_KV_RANK = 256
MLA_NOPE_DIM = 128
MLA_ROPE_DIM = 64
MLA_V_DIM = 128
MLA_QK_DIM = MLA_NOPE_DIM + MLA_ROPE_DIM
MLA_QK_PAD = 2 * LANES
PEER_HEADS = 8
PEER_N_KEYS = 128
PEER_HALF = 128
PEER_TOPK = 16

HY_TOKENS = 512
MLA_TOKENS = 512
ATT_BLOCK = 512
ATT_KV_BLOCK = 2048
ATT_SUB_BLOCK = 1024
SEL_TOKENS = 512
PEER_TOKENS = 1024
PEER_EXPERTS = 1024
PEER_KEY_GROUP = PEER_EXPERTS // PEER_N_KEYS
PEER_KEY_REUSE = 2
PEER_PIECE = MXU_WIDTH
PACK_ELEMS = 1024 * 1024


def _cparams(*semantics):
    return pltpu.CompilerParams(dimension_semantics=semantics, vmem_limit_bytes=VMEM_LIMIT_BYTES)


def _const_spec(shape):
    return pl.BlockSpec(shape, lambda *_: (0,) * len(shape))


def _rms(x, g):
    return x * lax.rsqrt(jnp.mean(x * x, axis=-1, keepdims=True) + NORM_EPS) * g


def _gelu(x):
    return 0.5 * x * (1.0 + lax.erf(x * (2.0 ** -0.5)))


def _dot(a, b):
    return jnp.dot(a, b, preferred_element_type=F32)


def _dot_nt(a, b):
    return lax.dot_general(a, b, (((1,), (1,)), ((), ())), preferred_element_type=F32)


def _dot_tn(a, b):
    return lax.dot_general(a, b, (((0,), (0,)), ((), ())), preferred_element_type=F32)


def _hy_body(x_ref, gmix_ref, win_ref, vnorm_ref, ws_ref, bs_ref, cos_ref, sin_ref, dmask_ref, qdec_ref, kdec_ref,
             cdec_ref, rnorm_ref, wout_ref, o_ref, state_ref, y_ref):
    @pl.when(pl.program_id(1) == 0)
    def _():
        state_ref[...] = jnp.zeros_like(state_ref)

    x = x_ref[0]
    h = _rms(x, gmix_ref[...]).astype(BF16)
    tokens = x.shape[0]
    chunks = tokens // GM_CHUNK

    def w_in(lo, hi):
        return pltpu.bitcast(win_ref[:, lo:hi], BF16)

    u_all = _gelu(_dot(h, w_in(0, GM_WIDTH)))
    v_all = _gelu(_dot(h, w_in(GM_WIDTH, 2 * GM_WIDTH)))
    row = lax.broadcasted_iota(jnp.int32, (GM_CHUNK, GM_CHUNK), 0)
    col = lax.broadcasted_iota(jnp.int32, (GM_CHUNK, GM_CHUNK), 1)
    causal = row >= col
    for g in range(GM_GROUPS):
        cols = slice(g * GM_DIM, (g + 1) * GM_DIM)
        vg = _rms(v_all[:, cols], vnorm_ref[:, cols]).astype(BF16)
        wg = jnp.where(causal, ws_ref[g], 0.0).astype(BF16)
        for c in range(chunks):
            rows = slice(c * GM_CHUNK, (c + 1) * GM_CHUNK)
            mixed = _dot(wg, vg[rows]) + bs_ref[g]
            y_ref[rows, cols] = (u_all[rows, cols] * mixed).astype(BF16)

    base = 2 * GM_WIDTH
    q_all = _dot(h, w_in(base, base + RET_QK_WIDTH))
    k_all = _dot(h, w_in(base + RET_QK_WIDTH, base + 2 * RET_QK_WIDTH))
    base += 2 * RET_QK_WIDTH
    v_all = _dot(h, w_in(base, base + RET_V_WIDTH))
    g_all = _dot(h, w_in(base + RET_V_WIDTH, base + 2 * RET_V_WIDTH))
    cos = cos_ref[...]
    sin = sin_ref[...]
    for hd in range(RET_HEADS):
        qk_cols = slice(hd * RET_QK_DIM, (hd + 1) * RET_QK_DIM)
        v_cols = slice(hd * RET_V_DIM, (hd + 1) * RET_V_DIM)
        q = q_all[:, qk_cols]
        k = k_all[:, qk_cols]
        q = q * cos + pltpu.roll(q, RET_QK_DIM // 2, 1) * sin
        k = (k * cos + pltpu.roll(k, RET_QK_DIM // 2, 1) * sin) * (RET_QK_DIM ** -0.5)
        for c in range(chunks):
            rows = slice(c * RET_CHUNK, (c + 1) * RET_CHUNK)
            qc, kc, vc = q[rows], k[rows], v_all[rows, v_cols]
            scores = _dot_nt(qc.astype(BF16), kc.astype(BF16)) * dmask_ref[hd]
            intra = _dot(scores.astype(BF16), vc.astype(BF16))
            state = state_ref[hd]
            cross = _dot((qc * qdec_ref[hd]).astype(BF16), state.astype(BF16))
            state_ref[hd] = state * cdec_ref[hd] + _dot_tn(kc.astype(BF16), (vc * kdec_ref[hd]).astype(BF16))
            gate = g_all[rows, v_cols]
            yb = _rms(intra + cross, rnorm_ref[:, v_cols]) * (gate * jax.nn.sigmoid(gate))
            y_ref[rows, GM_WIDTH + hd * RET_V_DIM:GM_WIDTH + (hd + 1) * RET_V_DIM] = yb.astype(BF16)

    o_ref[0] = x + _dot(y_ref[...], pltpu.bitcast(wout_ref[...], BF16))


def _rope_angles(seq, half):
    inv = 1.0 / (ROPE_THETA ** (jnp.arange(half, dtype=F32) / half))
    ang = jnp.arange(seq, dtype=F32)[:, None] * inv[None, :]
    return jnp.cos(ang), jnp.sin(ang)


def _hy_mixer(x, gmix, w_in, v_norm, w_s, b_s, ret_norm, w_out):
    b, s, d = x.shape
    ts = HY_TOKENS
    cos, sin = _rope_angles(s, RET_QK_DIM // 2)
    cos2 = jnp.concatenate([cos, cos], axis=1)
    sin2 = jnp.concatenate([-sin, sin], axis=1)
    log_gamma = jnp.log(1.0 - 2.0 ** (-5.0 - jnp.arange(RET_HEADS, dtype=F32)))
    pos = jnp.arange(RET_CHUNK, dtype=F32)
    diff = pos[:, None] - pos[None, :]
    dmask = jnp.where(diff[None] >= 0, jnp.exp(diff[None] * log_gamma[:, None, None]), 0.0)
    qdec = jnp.exp((pos[None, :] + 1.0) * log_gamma[:, None])[:, :, None]
    kdec = jnp.exp((RET_CHUNK - 1.0 - pos[None, :]) * log_gamma[:, None])[:, :, None]
    cdec = jnp.exp(RET_CHUNK * log_gamma)[:, None, None]
    in_width = w_in.shape[1]
    return pl.pallas_call(
        _hy_body,
        out_shape=jax.ShapeDtypeStruct((b, s, d), F32),
        grid=(b, s // ts),
        in_specs=[
            pl.BlockSpec((1, ts, d), lambda i, j: (i, j, 0)),
            _const_spec((1, d)),
            _const_spec((d // 2, in_width)),
            _const_spec((1, GM_WIDTH)),
            _const_spec((GM_GROUPS, GM_CHUNK, GM_CHUNK)),
            _const_spec((GM_GROUPS, GM_CHUNK, 1)),
            pl.BlockSpec((ts, RET_QK_DIM), lambda i, j: (j, 0)),
            pl.BlockSpec((ts, RET_QK_DIM), lambda i, j: (j, 0)),
            _const_spec((RET_HEADS, RET_CHUNK, RET_CHUNK)),
            _const_spec((RET_HEADS, RET_CHUNK, 1)),
            _const_spec((RET_HEADS, RET_CHUNK, 1)),
            _const_spec((RET_HEADS, 1, 1)),
            _const_spec((1, RET_V_WIDTH)),
            _const_spec((HY_OUT_WIDTH // 2, d)),
        ],
        out_specs=pl.BlockSpec((1, ts, d), lambda i, j: (i, j, 0)),
        scratch_shapes=[
            pltpu.VMEM((RET_HEADS, RET_QK_DIM, RET_V_DIM), F32),
            pltpu.VMEM((ts, HY_OUT_WIDTH), BF16),
        ],
        compiler_params=_cparams("parallel", "arbitrary"),
        name="hy_mixer",
    )(x, gmix[None, :], _pack_rows(w_in[None], 0), v_norm[None, :], w_s, b_s[:, :, None], cos2, sin2, dmask, qdec,
      kdec, cdec, ret_norm[None, :], _pack_rows(w_out[None], 0))


def _mla_proj_body(x_ref, gmix_ref, wdown_ref, qn_ref, kvn_ref, wq_ref, wqrot_ref, wk_ref, wv_ref, cq_ref, sq_ref,
                   ck_ref, sk_ref, q_ref, k_ref, v_ref):
    h = _rms(x_ref[0], gmix_ref[...]).astype(BF16)
    down = _dot(h, wdown_ref[...])
    kv_end = MLA_Q_RANK + MLA_KV_RANK
    cq = _rms(down[:, :MLA_Q_RANK], qn_ref[...]).astype(BF16)
    ckv = _rms(down[:, MLA_Q_RANK:kv_end], kvn_ref[...]).astype(BF16)
    k_rope = (down[:, kv_end:kv_end + LANES] * ck_ref[...] + down[:, kv_end + LANES:kv_end + 2 * LANES] * sk_ref[...])
    k_rope = k_rope.astype(BF16)
    qa = _dot(cq, wq_ref[...])
    qb = _dot(cq, wqrot_ref[...])
    kn = _dot(ckv, wk_ref[...])
    scale = MLA_QK_DIM ** -0.5
    cq_tab = cq_ref[...] * scale
    sq_tab = sq_ref[...] * scale
    for hd in range(MLA_HEADS):
        seg = slice(hd * MLA_QK_PAD, (hd + 1) * MLA_QK_PAD)
        q_ref[0, :, seg] = (qa[:, seg] * cq_tab + qb[:, seg] * sq_tab).astype(BF16)
        k_ref[0, :, hd * MLA_QK_PAD:hd * MLA_QK_PAD + MLA_NOPE_DIM] = (
            kn[:, hd * MLA_NOPE_DIM:(hd + 1) * MLA_NOPE_DIM].astype(BF16))
        k_ref[0, :, hd * MLA_QK_PAD + MLA_NOPE_DIM:(hd + 1) * MLA_QK_PAD] = k_rope
    v_ref[0] = _dot(ckv, wv_ref[...]).astype(BF16)


def _rot_half_cols(w):
    half = w.shape[-1] // 2
    return jnp.concatenate([-w[..., half:], w[..., :half]], axis=-1)


def _mla_proj(x, gmix, w_down, q_norm, w_q_up, kv_norm, w_kv_up):
    b, s, d = x.shape
    tm = MLA_TOKENS
    kv_end = MLA_Q_RANK + MLA_KV_RANK
    pad = MLA_QK_PAD - MLA_QK_DIM
    w_kr = w_down[:, kv_end:]
    zk = jnp.zeros((d, LANES - MLA_ROPE_DIM), F32)
    w_down_x = jnp.concatenate([w_down[:, :kv_end], w_kr, zk, _rot_half_cols(w_kr), zk], axis=1).astype(BF16)
    wq = w_q_up.reshape(MLA_Q_RANK, MLA_HEADS, MLA_QK_DIM)
    zq = jnp.zeros((MLA_Q_RANK, MLA_HEADS, pad), F32)
    wq_a = jnp.concatenate([wq, zq], axis=2).reshape(MLA_Q_RANK, MLA_HEADS * MLA_QK_PAD).astype(BF16)
    wq_b = jnp.concatenate([jnp.zeros((MLA_Q_RANK, MLA_HEADS, MLA_NOPE_DIM), F32),
                            _rot_half_cols(wq[:, :, MLA_NOPE_DIM:]), zq], axis=2)
    wq_b = wq_b.reshape(MLA_Q_RANK, MLA_HEADS * MLA_QK_PAD).astype(BF16)
    wkv = w_kv_up.reshape(MLA_KV_RANK, MLA_HEADS, MLA_NOPE_DIM + MLA_V_DIM)
    wk = wkv[:, :, :MLA_NOPE_DIM].reshape(MLA_KV_RANK, MLA_HEADS * MLA_NOPE_DIM).astype(BF16)
    wv = wkv[:, :, MLA_NOPE_DIM:].reshape(MLA_KV_RANK, MLA_HEADS * MLA_V_DIM).astype(BF16)
    cos, sin = _rope_angles(s, MLA_ROPE_DIM // 2)
    zt = jnp.zeros((s, pad), F32)
    cq_tab = jnp.concatenate([jnp.ones((s, MLA_NOPE_DIM), F32), cos, cos, zt], axis=1)
    sq_tab = jnp.concatenate([jnp.zeros((s, MLA_NOPE_DIM), F32), sin, sin, zt], axis=1)
    ck_tab = jnp.concatenate([cos, cos, zt], axis=1)
    sk_tab = jnp.concatenate([sin, sin, zt], axis=1)
    qk_width = MLA_HEADS * MLA_QK_PAD
    v_width = MLA_HEADS * MLA_V_DIM
    tab = lambda w: pl.BlockSpec((tm, w), lambda i, j: (j, 0))
    out = lambda w: pl.BlockSpec((1, tm, w), lambda i, j: (i, j, 0))
    return pl.pallas_call(
        _mla_proj_body,
        out_shape=(jax.ShapeDtypeStruct((b, s, qk_width), BF16), jax.ShapeDtypeStruct((b, s, qk_width), BF16),
                   jax.ShapeDtypeStruct((b, s, v_width), BF16)),
        grid=(b, s // tm),
        in_specs=[
            out(d),
            _const_spec((1, d)),
            _const_spec(w_down_x.shape),
            _const_spec((1, MLA_Q_RANK)),
            _const_spec((1, MLA_KV_RANK)),
            _const_spec(wq_a.shape),
            _const_spec(wq_b.shape),
            _const_spec(wk.shape),
            _const_spec(wv.shape),
            tab(MLA_QK_PAD), tab(MLA_QK_PAD), tab(LANES), tab(LANES),
        ],
        out_specs=(out(qk_width), out(qk_width), out(v_width)),
        compiler_params=_cparams("parallel", "parallel"),
        name="mla_proj",
    )(x, gmix[None, :], w_down_x, q_norm[None, :], kv_norm[None, :], wq_a, wq_b, wk, wv, cq_tab, sq_tab, ck_tab, sk_tab)


def _flash_body(q_ref, k_ref, v_ref, o_ref):
    blk = q_ref.shape[1]
    kvb = ATT_KV_BLOCK
    i = pl.program_id(2)
    q = q_ref[0]
    full = (i * blk) // kvb

    def step(start, width, carry, masked):
        m, l, acc = carry
        offs = range(0, width, ATT_SUB_BLOCK)
        widths = [min(ATT_SUB_BLOCK, width - off) for off in offs]
        scores = [_dot_nt(q, k_ref[0, pl.ds(start + off, w), :]) for off, w in zip(offs, widths)]
        for off, w, s in zip(offs, widths, scores):
            if masked:
                row = lax.broadcasted_iota(jnp.int32, s.shape, 0) + i * blk
                col = lax.broadcasted_iota(jnp.int32, s.shape, 1) + (start + off)
                s = jnp.where(row >= col, s, -jnp.inf)
            m_new = jnp.maximum(m, jnp.max(s, axis=1, keepdims=True))
            p = jnp.exp(s - m_new)
            alpha = jnp.exp(m - m_new)
            l = alpha * l + jnp.sum(p, axis=1, keepdims=True)
            acc = alpha * acc + _dot(p.astype(BF16), v_ref[0, pl.ds(start + off, w), :])
            m = m_new
        return m, l, acc

    init = (jnp.full((blk, 1), -jnp.inf, F32), jnp.zeros((blk, 1), F32), jnp.zeros((blk, v_ref.shape[2]), F32))
    carry = lax.fori_loop(0, full, lambda j, c: step(pl.multiple_of(j * kvb, kvb), kvb, c, False), init)
    for rest in range(1, kvb // blk + 1):
        @pl.when((i + 1) * blk - full * kvb == rest * blk)
        def _():
            _, l, acc = step(pl.multiple_of(full * kvb, kvb), rest * blk, carry, True)
            o_ref[0] = (acc / l).astype(BF16)


def _flash(q, k, v):
    b, s, _ = q.shape
    blk = ATT_BLOCK
    return pl.pallas_call(
        _flash_body,
        out_shape=jax.ShapeDtypeStruct((b, s, MLA_HEADS * MLA_V_DIM), BF16),
        grid=(b, MLA_HEADS, s // blk),
        in_specs=[
            pl.BlockSpec((1, blk, MLA_QK_PAD), lambda bi, h, i: (bi, i, h)),
            pl.BlockSpec((1, s, MLA_QK_PAD), lambda bi, h, i: (bi, 0, h)),
            pl.BlockSpec((1, s, MLA_V_DIM), lambda bi, h, i: (bi, 0, h)),
        ],
        out_specs=pl.BlockSpec((1, blk, MLA_V_DIM), lambda bi, h, i: (bi, i, h)),
        compiler_params=_cparams("parallel", "parallel", "arbitrary"),
        name="mla_flash",
    )(q, k, v)


def _out_proj_body(x_ref, o_ref, w_ref, y_ref):
    y_ref[...] = x_ref[...] + _dot(o_ref[...], w_ref[...])


def _out_proj(x2, o2, w_out):
    t, d = x2.shape
    tm = MLA_TOKENS
    return pl.pallas_call(
        _out_proj_body,
        out_shape=jax.ShapeDtypeStruct((t, d), F32),
        grid=(t // tm,),
        in_specs=[pl.BlockSpec((tm, d), lambda i: (i, 0)), pl.BlockSpec((tm, o2.shape[1]), lambda i: (i, 0)),
                  _const_spec(w_out.shape)],
        out_specs=pl.BlockSpec((tm, d), lambda i: (i, 0)),
        compiler_params=_cparams("parallel"),
        name="mla_out",
    )(x2, o2, w_out.astype(BF16))


def _sort_network(n):
    def merge(lo, hi, r):
        step = 2 * r
        if step < hi - lo:
            yield from merge(lo, hi, step)
            yield from merge(lo + r, hi, step)
            yield from ((i, i + r) for i in range(lo + r, hi - r, step))
        else:
            yield (lo, lo + r)

    def sort(lo, hi):
        if hi - lo >= 1:
            mid = lo + (hi - lo) // 2
            yield from sort(lo, mid)
            yield from sort(mid + 1, hi)
            yield from merge(lo, hi, 1)

    return tuple(sort(0, n - 1))


_SORT_TOPK = _sort_network(PEER_TOPK)


def _top_sorted(slabs):
    groups = []
    for g0 in range(0, len(slabs), PEER_TOPK):
        v = list(slabs[g0:g0 + PEER_TOPK])
        for i, j in _SORT_TOPK:
            v[i], v[j] = jnp.maximum(v[i], v[j]), jnp.minimum(v[i], v[j])
        groups.append(v)
    while len(groups) > 1:
        merged = []
        for a, b in zip(groups[0::2], groups[1::2]):
            v = [jnp.maximum(a[i], b[PEER_TOPK - 1 - i]) for i in range(PEER_TOPK)]
            d = PEER_TOPK // 2
            while d >= 1:
                for i in range(PEER_TOPK):
                    if not i & d:
                        v[i], v[i + d] = jnp.maximum(v[i], v[i + d]), jnp.minimum(v[i], v[i + d])
                d //= 2
            merged.append(v)
        groups = merged
    return groups[0]


def _merge_counts(s1, s2):
    n = [jnp.zeros_like(s1[0]) for _ in range(PEER_TOPK)]
    front = [s1[a] + s2[0] for a in range(PEER_TOPK)]
    for _ in range(PEER_TOPK):
        m = functools.reduce(jnp.maximum, front)
        first = jnp.full_like(m, float(PEER_TOPK))
        for a in reversed(range(PEER_TOPK)):
            first = jnp.where(front[a] == m, float(a), first)
        hits = [first == float(a) for a in range(PEER_TOPK)]
        n_hit = jnp.zeros_like(m)
        for a in range(PEER_TOPK):
            n_hit = jnp.where(hits[a], n[a], n_hit)
        nxt = jnp.full_like(m, -jnp.inf)
        for b in range(1, PEER_TOPK):
            nxt = jnp.where(n_hit == float(b - 1), s2[b], nxt)
        for a in range(PEER_TOPK):
            n[a] = jnp.where(hits[a], n[a] + 1.0, n[a])
            front[a] = jnp.where(hits[a], s1[a] + nxt, front[a])
    return n


_RANK_CHECKSUM = float(sum(range(PEER_TOPK)) + (PEER_N_KEYS - PEER_TOPK) * PEER_TOPK)


def _sel_tile(lt, s_ref, r_ref, t_ref, e1_ref, nk_ref, e2_ref, rb_ref):
    keys = range(PEER_N_KEYS)
    top = [_top_sorted([s_ref[p, lt, k] for k in keys]) for p in range(2)]
    n = _merge_counts(top[0], top[1])

    e1s = [jnp.exp(v - top[0][0]) for v in top[0]]
    e2s = [jnp.exp(v - top[1][0]) for v in top[1]]
    z = jnp.zeros_like(e1s[0])
    for b in range(PEER_TOPK):
        row = jnp.zeros_like(z)
        for a in range(PEER_TOPK):
            row = row + jnp.where(n[a] > float(b), e1s[a], 0.0)
        z = z + row * e2s[b]
    inv_z = 1.0 / z

    def rank(s, sorted_vals):
        r = jnp.full_like(s, float(PEER_TOPK))
        for a in reversed(range(PEER_TOPK)):
            r = jnp.where(s >= sorted_vals[a], float(a), r)
        return r

    def shifted(r, seen):
        seen = list(seen)
        back = jnp.zeros_like(r)
        for a in range(PEER_TOPK):
            hit = r == float(a)
            back = jnp.where(hit, seen[a], back)
            seen[a] = jnp.where(hit, seen[a] + 1.0, seen[a])
        return jnp.minimum(r + back, float(PEER_TOPK)), tuple(seen)

    none_seen = tuple(jnp.zeros_like(z) for _ in range(PEER_TOPK))

    def any_lane(mask):
        return jnp.max(jnp.where(mask, 1.0, 0.0)) > 0.0

    total = jnp.zeros_like(z)
    for k in keys:
        r = rank(s_ref[1, lt, k], top[1])
        r_ref[k] = r
        total = total + r
        t_ref[k] = jnp.exp(s_ref[1, lt, k] - top[1][0])

    @pl.when(any_lane(total != _RANK_CHECKSUM))
    def _():
        def fix(k, seen):
            r_ref[k], seen = shifted(r_ref[k], seen)
            return seen

        lax.fori_loop(0, PEER_N_KEYS, fix, none_seen)

    count = jnp.zeros_like(z)
    for k in keys:
        s = s_ref[0, lt, k]
        nk = jnp.zeros_like(s)
        for a in reversed(range(PEER_TOPK)):
            nk = jnp.where(s >= top[0][a], n[a], nk)
        nk_ref[lt, k] = nk
        e1_ref[lt, k] = jnp.exp(s - top[0][0]) * inv_z
        count = count + jnp.where(s >= top[0][PEER_TOPK - 1], 1.0, 0.0)
    tied = count != float(PEER_TOPK)
    for a in range(PEER_TOPK - 1):
        tied = tied | (top[0][a] == top[0][a + 1])

    @pl.when(any_lane(tied))
    def _():
        def fix(k, seen):
            r, seen = shifted(rank(s_ref[0, lt, k], top[0]), seen)
            nk = jnp.zeros_like(r)
            for a in range(PEER_TOPK):
                nk = jnp.where(r == float(a), n[a], nk)
            nk_ref[lt, k] = nk
            return seen

        lax.fori_loop(0, PEER_N_KEYS, fix, none_seen)

    e2p = pltpu.bitcast(jnp.swapaxes(t_ref[...], 0, 1).astype(BF16), F32)
    rbp = pltpu.bitcast(jnp.swapaxes(r_ref[...], 0, 1).astype(BF16), F32)
    for h in range(PEER_HEADS):
        e2_ref[h, lt] = e2p[h]
        rb_ref[h, lt] = rbp[h]


def _fold_keys_body(keys_ref, wq_ref, o_ref):
    o_ref[0, 0] = jnp.dot(keys_ref[0], wq_ref[0, 0], preferred_element_type=F32, precision=lax.Precision.HIGHEST)


def _fold_keys(sub_keys, w_q):
    d = w_q.shape[0]
    wq_t = w_q.reshape(d, PEER_HEADS, 2, PEER_HALF).transpose(2, 1, 3, 0)
    return pl.pallas_call(
        _fold_keys_body,
        out_shape=jax.ShapeDtypeStruct((2, PEER_HEADS, PEER_N_KEYS, d), F32),
        grid=(2, PEER_HEADS),
        in_specs=[pl.BlockSpec((1, PEER_N_KEYS, PEER_HALF), lambda p, h: (p, 0, 0)),
                  pl.BlockSpec((1, 1, PEER_HALF, d), lambda p, h: (p, h, 0, 0))],
        out_specs=pl.BlockSpec((1, 1, PEER_N_KEYS, d), lambda p, h: (p, h, 0, 0)),
        compiler_params=_cparams("parallel", "parallel"),
        name="fold_keys",
    )(sub_keys, wq_t)


def _peer_sel_body(x_ref, g_ref, ws_ref, hb_ref, e1_ref, nk_ref, e2_ref, rb_ref, s_ref, r_ref, t_ref):
    hb = _rms(x_ref[...], g_ref[...]).astype(BF16)
    hb_ref[...] = pltpu.bitcast(hb, F32)
    tiles = x_ref.shape[0] // LANES
    for p in range(2):
        sc = _dot_nt(ws_ref[p], hb)
        sc = sc.reshape(PEER_N_KEYS, PEER_HEADS, tiles * LANES)
        for lt in range(tiles):
            s_ref[p, lt] = sc[:, :, lt * LANES:(lt + 1) * LANES]

    def tile(lt, _):
        _sel_tile(lt, s_ref, r_ref, t_ref, e1_ref, nk_ref, e2_ref, rb_ref)
        return 0

    lax.fori_loop(0, tiles, tile, 0)


def _peer_select(x2, gain, w_q, sub_keys):
    t, d = x2.shape
    tb = SEL_TOKENS
    tiles = tb // LANES
    ws = _fold_keys(sub_keys, w_q).transpose(0, 2, 1, 3).reshape(2, PEER_N_KEYS * PEER_HEADS, d).astype(BF16)
    key_shape = jax.ShapeDtypeStruct((t // LANES, PEER_N_KEYS, PEER_HEADS, LANES), F32)
    key_spec = pl.BlockSpec((tiles, PEER_N_KEYS, PEER_HEADS, LANES), lambda i: (i, 0, 0, 0))
    pair_shape = jax.ShapeDtypeStruct((PEER_HEADS, t // LANES, PEER_N_KEYS // 2, LANES), F32)
    pair_spec = pl.BlockSpec((PEER_HEADS, tiles, PEER_N_KEYS // 2, LANES), lambda i: (0, i, 0, 0))
    slab = (PEER_N_KEYS, PEER_HEADS, LANES)
    return pl.pallas_call(
        _peer_sel_body,
        out_shape=(jax.ShapeDtypeStruct((t // 2, d), F32), key_shape, key_shape, pair_shape, pair_shape),
        grid=(t // tb,),
        in_specs=[pl.BlockSpec((tb, d), lambda i: (i, 0)), _const_spec((1, d)), _const_spec(ws.shape)],
        out_specs=(pl.BlockSpec((tb // 2, d), lambda i: (i, 0)), key_spec, key_spec, pair_spec, pair_spec),
        scratch_shapes=[pltpu.VMEM((2, tiles) + slab, F32), pltpu.VMEM(slab, F32), pltpu.VMEM(slab, F32)],
        compiler_params=_cparams("parallel"),
        name="peer_select",
    )(x2, gain[None, :], ws)


def _key_rows_bf16(row):
    packed = jnp.broadcast_to(row, (BF16_SUBLANES, row.shape[1])).astype(BF16)
    return jnp.concatenate([packed] * (PEER_N_KEYS // BF16_SUBLANES), axis=0)


def _peer_main_body(final_norm, x_ref, hb_ref, u_ref, vt_ref, e1_ref, n_ref, e2_ref, rb_ref, fg_ref, o_ref, acc_ref,
                    a_ref):
    s = pl.program_id(1)
    last = pl.num_programs(1) - 1
    piece_tiles = PEER_PIECE // LANES
    pieces = range(acc_ref.shape[0])

    def gates_and_values(p):
        g_tiles = []
        for tb in range(p * piece_tiles, (p + 1) * piece_tiles):
            g_keys = []
            for k0 in range(0, PEER_KEY_GROUP, PEER_KEY_REUSE):
                ks = range(k0, k0 + PEER_KEY_REUSE)
                w = {}
                for h in range(PEER_HEADS):
                    rb_t = pltpu.bitcast(rb_ref[h, tb], BF16)
                    e2_t = pltpu.bitcast(e2_ref[h, tb], BF16)
                    for k in ks:
                        n_b = _key_rows_bf16(n_ref[tb, k, h:h + 1, :])
                        e1_b = _key_rows_bf16(e1_ref[tb, k, h:h + 1, :])
                        term = jnp.where(rb_t < n_b, e2_t * e1_b, 0.0)
                        w[k] = term if h == 0 else w[k] + term
                for k in ks:
                    g_keys.append(w[k] * _gelu(a_ref[tb, k * PEER_N_KEYS:(k + 1) * PEER_N_KEYS, :].astype(BF16)))
            g_tiles.append(jnp.concatenate(g_keys, axis=0))
        acc_ref[p] += _dot(pltpu.bitcast(vt_ref[0], BF16), jnp.concatenate(g_tiles, axis=1))

    def activations(p):
        hb_piece = pltpu.bitcast(hb_ref[p * (PEER_PIECE // 2):(p + 1) * (PEER_PIECE // 2), :], BF16)
        a_piece = _dot_nt(pltpu.bitcast(u_ref[...], BF16), hb_piece)
        for t in range(piece_tiles):
            a_ref[p * piece_tiles + t] = a_piece[:, t * LANES:(t + 1) * LANES]

    @pl.when(s == 0)
    def _():
        acc_ref[...] = jnp.zeros_like(acc_ref)
        for p in pieces:
            activations(p)

    @pl.when((s > 0) & (s < last))
    def _():
        for p in pieces:
            gates_and_values(p)
            activations(p)

    @pl.when(s == last)
    def _():
        for p in pieces:
            gates_and_values(p)
        for p in pieces:
            rows = slice(p * PEER_PIECE, (p + 1) * PEER_PIECE)
            y = x_ref[rows, :] + acc_ref[p].T
            if final_norm:
                y = _rms(y, fg_ref[...])
            o_ref[rows, :] = y


def _pack_rows_body(x_ref, o_ref):
    o_ref[...] = pltpu.bitcast(x_ref[0].astype(BF16), F32)


def _pack_rows(tables, layer):
    _, rows, d = tables.shape
    blk = min(rows, 1 << ((PACK_ELEMS // d).bit_length() - 1))
    return pl.pallas_call(
        _pack_rows_body,
        out_shape=jax.ShapeDtypeStruct((rows // 2, d), F32),
        grid=(rows // blk,),
        in_specs=[pl.BlockSpec((1, blk, d), lambda i: (layer, i, 0))],
        out_specs=pl.BlockSpec((blk // 2, d), lambda i: (i, 0)),
        compiler_params=_cparams("parallel"),
        name="pack_rows",
    )(tables)


def _pack_transposed_body(x_ref, o_ref):
    o_ref[0] = pltpu.bitcast(x_ref[0].T.astype(BF16), F32)


def _pack_transposed_chunks(tables, layer, chunk):
    _, rows, d = tables.shape
    return pl.pallas_call(
        _pack_transposed_body,
        out_shape=jax.ShapeDtypeStruct((rows // chunk, d // 2, chunk), F32),
        grid=(rows // chunk,),
        in_specs=[pl.BlockSpec((1, chunk, d), lambda i: (layer, i, 0))],
        out_specs=pl.BlockSpec((1, d // 2, chunk), lambda i: (i, 0, 0)),
        compiler_params=_cparams("parallel"),
        name="pack_transposed",
    )(tables)


def _peer_experts(x2, hb, e1, n, e2, rb, u_tabs, v_tabs, layer, final_gain=None):
    t, d = x2.shape
    n_experts = u_tabs.shape[1]
    tb, eb = PEER_TOKENS, PEER_EXPERTS
    chunks = n_experts // eb
    final_norm = final_gain is not None
    fg = (final_gain if final_norm else jnp.ones((d,), F32))[None, :]
    tok_spec = pl.BlockSpec((tb, d), lambda i, s: (i, 0))
    hb_spec = pl.BlockSpec((tb // 2, d), lambda i, s: (i, 0))
    sel_spec = pl.BlockSpec((PEER_HEADS, tb // LANES, PEER_N_KEYS // 2, LANES), lambda i, s: (0, i, 0, 0))
    key_spec = pl.BlockSpec((tb // LANES, PEER_KEY_GROUP, PEER_HEADS, LANES),
                            lambda i, s: (i, jnp.clip(s - 1, 0, chunks - 1), 0, 0))
    return pl.pallas_call(
        functools.partial(_peer_main_body, final_norm),
        out_shape=jax.ShapeDtypeStruct((t, d), F32),
        grid=(t // tb, chunks + 1),
        in_specs=[
            tok_spec,
            hb_spec,
            pl.BlockSpec((eb // 2, d), lambda i, s: (jnp.minimum(s, chunks - 1), 0)),
            pl.BlockSpec((1, d // 2, eb), lambda i, s: (jnp.maximum(s - 1, 0), 0, 0)),
            key_spec, key_spec, sel_spec, sel_spec,
            _const_spec((1, d)),
        ],
        out_specs=pl.BlockSpec((tb, d), lambda i, s: (i, 0)),
        scratch_shapes=[pltpu.VMEM((tb // PEER_PIECE, d, PEER_PIECE), F32), pltpu.VMEM((tb // LANES, eb, LANES), F32)],
        compiler_params=_cparams("parallel", "arbitrary"),
        name="peer_experts",
    )(x2, hb, _pack_rows(u_tabs, layer), _pack_transposed_chunks(v_tabs, layer, eb), e1, n, e2, rb, fg)


def _peer(x2, gain, w_q, sub_keys, u_tabs, v_tabs, layer, final_gain=None):
    hb, e1, n, e2, rb = _peer_select(x2, gain, w_q, sub_keys)
    return _peer_experts(x2, hb, e1, n, e2, rb, u_tabs, v_tabs, layer, final_gain)


def kernel(x, norm_mix, norm_ffn, hy_w_in, gm_v_norm, gm_w_s, gm_b_s, ret_norm, hy_w_out, mla_w_down, mla_q_norm,
           mla_w_q_up, mla_kv_norm, mla_w_kv_up, mla_w_out, peer_w_q, peer_sub_keys, peer_u, peer_v, final_norm):
    b, s, d = x.shape
    depth = norm_mix.shape[0]
    for layer in range(depth):
        j = layer // 2
        if layer % 2 == 0:
            x = _hy_mixer(x, norm_mix[layer], hy_w_in[j], gm_v_norm[j], gm_w_s[j], gm_b_s[j], ret_norm[j], hy_w_out[j])
            x2 = x.reshape(b * s, d)
        else:
            q, k, v = _mla_proj(x, norm_mix[layer], mla_w_down[j], mla_q_norm[j], mla_w_q_up[j], mla_kv_norm[j],
                                mla_w_kv_up[j])
            o = _flash(q, k, v)
            x2 = _out_proj(x.reshape(b * s, d), o.reshape(b * s, -1), mla_w_out[j])
        last = layer == depth - 1
        x2 = _peer(x2, norm_ffn[layer], peer_w_q[layer], peer_sub_keys[layer], peer_u, peer_v, layer,
                   final_norm if last else None)
        x = x2.reshape(b, s, d)
    return x
```

```python
import functools

import jax
import jax.numpy as jnp
from jax import lax
from jax.experimental import pallas as pl
from jax.experimental.pallas import tpu as pltpu

F32 = jnp.float32
BF16 = jnp.bfloat16
PAIRS = jnp.uint32

LANES = 128
BF16_SUBLANES = 16
MXU_WIDTH = 256
V7X_VMEM_BYTES = 64 * 1024 * 1024
VMEM_LIMIT_BYTES = V7X_VMEM_BYTES - 8 * 1024 * 1024

NORM_EPS = 1e-6
ROPE_THETA = 10000.0

GM_GROUPS = 4
GM_DIM = 256
GM_CHUNK = 128
GM_WIDTH = GM_GROUPS * GM_DIM
RET_HEADS = 4
RET_QK_DIM = 128
RET_V_DIM = 256
RET_CHUNK = 128
RET_QK_WIDTH = RET_HEADS * RET_QK_DIM
RET_V_WIDTH = RET_HEADS * RET_V_DIM
HY_OUT_WIDTH = GM_WIDTH + RET_V_WIDTH
MLA_HEADS = 8
MLA_Q_RANK = 384
MLA_KV_RANK = 256
MLA_NOPE_DIM = 128
MLA_ROPE_DIM = 64
MLA_V_DIM = 128
MLA_QK_DIM = MLA_NOPE_DIM + MLA_ROPE_DIM
MLA_QK_PAD = 2 * LANES
PEER_HEADS = 8
PEER_N_KEYS = 128
PEER_HALF = 128
PEER_TOPK = 16

HY_TOKENS = 512
MLA_TOKENS = 512
ATT_BLOCK = 512
ATT_SUB_BLOCK = 4096
SEL_TOKENS = 512
PEER_TOKENS = 1024
PEER_EXPERTS = 1024
PEER_KEY_GROUP = PEER_EXPERTS // PEER_N_KEYS
PEER_KEY_REUSE = 2
PEER_PIECE = MXU_WIDTH
PACK_ELEMS = 1024 * 1024


def _cparams(*semantics):
    return pltpu.CompilerParams(dimension_semantics=semantics, vmem_limit_bytes=VMEM_LIMIT_BYTES)


def _const_spec(shape):
    return pl.BlockSpec(shape, lambda *_: (0,) * len(shape))


def _rms(x, g):
    return x * lax.rsqrt(jnp.mean(x * x, axis=-1, keepdims=True) + NORM_EPS) * g


def _gelu(x):
    return 0.5 * x * (1.0 + lax.erf(x * (2.0 ** -0.5)))


def _dot(a, b):
    return jnp.dot(a, b, preferred_element_type=F32)


def _dot_nt(a, b):
    return lax.dot_general(a, b, (((1,), (1,)), ((), ())), preferred_element_type=F32)


def _dot_tn(a, b):
    return lax.dot_general(a, b, (((0,), (0,)), ((), ())), preferred_element_type=F32)


def _hy_body(x_ref, gmix_ref, win_ref, vnorm_ref, ws_ref, bs_ref, cos_ref, sin_ref, dmask_ref, qdec_ref, kdec_ref,
             cdec_ref, rnorm_ref, wout_ref, o_ref, state_ref, y_ref):
    @pl.when(pl.program_id(1) == 0)
    def _():
        state_ref[...] = jnp.zeros_like(state_ref)

    x = x_ref[0]
    h = _rms(x, gmix_ref[...]).astype(BF16)
    tokens = x.shape[0]
    chunks = tokens // GM_CHUNK

    u_all = _gelu(_dot(h, win_ref[:, 0:GM_WIDTH]))
    v_all = _gelu(_dot(h, win_ref[:, GM_WIDTH:2 * GM_WIDTH]))
    row = lax.broadcasted_iota(jnp.int32, (GM_CHUNK, GM_CHUNK), 0)
    col = lax.broadcasted_iota(jnp.int32, (GM_CHUNK, GM_CHUNK), 1)
    causal = row >= col
    for g in range(GM_GROUPS):
        cols = slice(g * GM_DIM, (g + 1) * GM_DIM)
        vg = _rms(v_all[:, cols], vnorm_ref[:, cols]).astype(BF16)
        wg = jnp.where(causal, ws_ref[g], 0.0).astype(BF16)
        for c in range(chunks):
            rows = slice(c * GM_CHUNK, (c + 1) * GM_CHUNK)
            mixed = _dot(wg, vg[rows]) + bs_ref[g]
            y_ref[rows, cols] = (u_all[rows, cols] * mixed).astype(BF16)

    base = 2 * GM_WIDTH
    q_all = _dot(h, win_ref[:, base:base + RET_QK_WIDTH])
    k_all = _dot(h, win_ref[:, base + RET_QK_WIDTH:base + 2 * RET_QK_WIDTH])
    base += 2 * RET_QK_WIDTH
    v_all = _dot(h, win_ref[:, base:base + RET_V_WIDTH])
    g_all = _dot(h, win_ref[:, base + RET_V_WIDTH:base + 2 * RET_V_WIDTH])
    cos = cos_ref[...]
    sin = sin_ref[...]
    for hd in range(RET_HEADS):
        qk_cols = slice(hd * RET_QK_DIM, (hd + 1) * RET_QK_DIM)
        v_cols = slice(hd * RET_V_DIM, (hd + 1) * RET_V_DIM)
        q = q_all[:, qk_cols]
        k = k_all[:, qk_cols]
        q = q * cos + pltpu.roll(q, RET_QK_DIM // 2, 1) * sin
        k = (k * cos + pltpu.roll(k, RET_QK_DIM // 2, 1) * sin) * (RET_QK_DIM ** -0.5)
        for c in range(chunks):
            rows = slice(c * RET_CHUNK, (c + 1) * RET_CHUNK)
            qc, kc, vc = q[rows], k[rows], v_all[rows, v_cols]
            scores = _dot_nt(qc.astype(BF16), kc.astype(BF16)) * dmask_ref[hd]
            intra = _dot(scores.astype(BF16), vc.astype(BF16))
            state = state_ref[hd]
            cross = _dot((qc * qdec_ref[hd]).astype(BF16), state.astype(BF16))
            state_ref[hd] = state * cdec_ref[hd] + _dot_tn(kc.astype(BF16), (vc * kdec_ref[hd]).astype(BF16))
            gate = g_all[rows, v_cols]
            yb = _rms(intra + cross, rnorm_ref[:, v_cols]) * (gate * jax.nn.sigmoid(gate))
            y_ref[rows, GM_WIDTH + hd * RET_V_DIM:GM_WIDTH + (hd + 1) * RET_V_DIM] = yb.astype(BF16)

    o_ref[0] = x + _dot(y_ref[...], wout_ref[...])


def _rope_angles(seq, half):
    inv = 1.0 / (ROPE_THETA ** (jnp.arange(half, dtype=F32) / half))
    ang = jnp.arange(seq, dtype=F32)[:, None] * inv[None, :]
    return jnp.cos(ang), jnp.sin(ang)


def _hy_mixer(x, gmix, w_in, v_norm, w_s, b_s, ret_norm, w_out):
    b, s, d = x.shape
    ts = HY_TOKENS
    cos, sin = _rope_angles(s, RET_QK_DIM // 2)
    cos2 = jnp.concatenate([cos, cos], axis=1)
    sin2 = jnp.concatenate([-sin, sin], axis=1)
    log_gamma = jnp.log(1.0 - 2.0 ** (-5.0 - jnp.arange(RET_HEADS, dtype=F32)))
    pos = jnp.arange(RET_CHUNK, dtype=F32)
    diff = pos[:, None] - pos[None, :]
    dmask = jnp.where(diff[None] >= 0, jnp.exp(diff[None] * log_gamma[:, None, None]), 0.0)
    qdec = jnp.exp((pos[None, :] + 1.0) * log_gamma[:, None])[:, :, None]
    kdec = jnp.exp((RET_CHUNK - 1.0 - pos[None, :]) * log_gamma[:, None])[:, :, None]
    cdec = jnp.exp(RET_CHUNK * log_gamma)[:, None, None]
    in_width = w_in.shape[1]
    return pl.pallas_call(
        _hy_body,
        out_shape=jax.ShapeDtypeStruct((b, s, d), F32),
        grid=(b, s // ts),
        in_specs=[
            pl.BlockSpec((1, ts, d), lambda i, j: (i, j, 0)),
            _const_spec((1, d)),
            _const_spec((d, in_width)),
            _const_spec((1, GM_WIDTH)),
            _const_spec((GM_GROUPS, GM_CHUNK, GM_CHUNK)),
            _const_spec((GM_GROUPS, GM_CHUNK, 1)),
            pl.BlockSpec((ts, RET_QK_DIM), lambda i, j: (j, 0)),
            pl.BlockSpec((ts, RET_QK_DIM), lambda i, j: (j, 0)),
            _const_spec((RET_HEADS, RET_CHUNK, RET_CHUNK)),
            _const_spec((RET_HEADS, RET_CHUNK, 1)),
            _const_spec((RET_HEADS, RET_CHUNK, 1)),
            _const_spec((RET_HEADS, 1, 1)),
            _const_spec((1, RET_V_WIDTH)),
            _const_spec((HY_OUT_WIDTH, d)),
        ],
        out_specs=pl.BlockSpec((1, ts, d), lambda i, j: (i, j, 0)),
        scratch_shapes=[
            pltpu.VMEM((RET_HEADS, RET_QK_DIM, RET_V_DIM), F32),
            pltpu.VMEM((ts, HY_OUT_WIDTH), BF16),
        ],
        compiler_params=_cparams("parallel", "arbitrary"),
        name="hy_mixer",
    )(x, gmix[None, :], w_in.astype(BF16), v_norm[None, :], w_s, b_s[:, :, None], cos2, sin2, dmask, qdec, kdec, cdec,
      ret_norm[None, :], w_out.astype(BF16))


def _mla_proj_body(x_ref, gmix_ref, wdown_ref, qn_ref, kvn_ref, wq_ref, wqrot_ref, wk_ref, wv_ref, cq_ref, sq_ref,
                   ck_ref, sk_ref, q_ref, k_ref, v_ref):
    h = _rms(x_ref[0], gmix_ref[...]).astype(BF16)
    down = _dot(h, wdown_ref[...])
    kv_end = MLA_Q_RANK + MLA_KV_RANK
    cq = _rms(down[:, :MLA_Q_RANK], qn_ref[...]).astype(BF16)
    ckv = _rms(down[:, MLA_Q_RANK:kv_end], kvn_ref[...]).astype(BF16)
    k_rope = (down[:, kv_end:kv_end + LANES] * ck_ref[...] + down[:, kv_end + LANES:kv_end + 2 * LANES] * sk_ref[...])
    k_rope = k_rope.astype(BF16)
    qa = _dot(cq, wq_ref[...])
    qb = _dot(cq, wqrot_ref[...])
    kn = _dot(ckv, wk_ref[...])
    scale = MLA_QK_DIM ** -0.5
    cq_tab = cq_ref[...] * scale
    sq_tab = sq_ref[...] * scale
    for hd in range(MLA_HEADS):
        seg = slice(hd * MLA_QK_PAD, (hd + 1) * MLA_QK_PAD)
        q_ref[0, :, seg] = (qa[:, seg] * cq_tab + qb[:, seg] * sq_tab).astype(BF16)
        k_ref[0, :, hd * MLA_QK_PAD:hd * MLA_QK_PAD + MLA_NOPE_DIM] = (
            kn[:, hd * MLA_NOPE_DIM:(hd + 1) * MLA_NOPE_DIM].astype(BF16))
        k_ref[0, :, hd * MLA_QK_PAD + MLA_NOPE_DIM:(hd + 1) * MLA_QK_PAD] = k_rope
    v_ref[0] = _dot(ckv, wv_ref[...]).astype(BF16)


def _rot_half_cols(w):
    half = w.shape[-1] // 2
    return jnp.concatenate([-w[..., half:], w[..., :half]], axis=-1)


def _mla_proj(x, gmix, w_down, q_norm, w_q_up, kv_norm, w_kv_up):
    b, s, d = x.shape
    tm = MLA_TOKENS
    kv_end = MLA_Q_RANK + MLA_KV_RANK
    pad = MLA_QK_PAD - MLA_QK_DIM
    w_kr = w_down[:, kv_end:]
    zk = jnp.zeros((d, LANES - MLA_ROPE_DIM), F32)
    w_down_x = jnp.concatenate([w_down[:, :kv_end], w_kr, zk, _rot_half_cols(w_kr), zk], axis=1).astype(BF16)
    wq = w_q_up.reshape(MLA_Q_RANK, MLA_HEADS, MLA_QK_DIM)
    zq = jnp.zeros((MLA_Q_RANK, MLA_HEADS, pad), F32)
    wq_a = jnp.concatenate([wq, zq], axis=2).reshape(MLA_Q_RANK, MLA_HEADS * MLA_QK_PAD).astype(BF16)
    wq_b = jnp.concatenate([jnp.zeros((MLA_Q_RANK, MLA_HEADS, MLA_NOPE_DIM), F32),
                            _rot_half_cols(wq[:, :, MLA_NOPE_DIM:]), zq], axis=2)
    wq_b = wq_b.reshape(MLA_Q_RANK, MLA_HEADS * MLA_QK_PAD).astype(BF16)
    wkv = w_kv_up.reshape(MLA_KV_RANK, MLA_HEADS, MLA_NOPE_DIM + MLA_V_DIM)
    wk = wkv[:, :, :MLA_NOPE_DIM].reshape(MLA_KV_RANK, MLA_HEADS * MLA_NOPE_DIM).astype(BF16)
    wv = wkv[:, :, MLA_NOPE_DIM:].reshape(MLA_KV_RANK, MLA_HEADS * MLA_V_DIM).astype(BF16)
    cos, sin = _rope_angles(s, MLA_ROPE_DIM // 2)
    zt = jnp.zeros((s, pad), F32)
    cq_tab = jnp.concatenate([jnp.ones((s, MLA_NOPE_DIM), F32), cos, cos, zt], axis=1)
    sq_tab = jnp.concatenate([jnp.zeros((s, MLA_NOPE_DIM), F32), sin, sin, zt], axis=1)
    ck_tab = jnp.concatenate([cos, cos, zt], axis=1)
    sk_tab = jnp.concatenate([sin, sin, zt], axis=1)
    qk_width = MLA_HEADS * MLA_QK_PAD
    v_width = MLA_HEADS * MLA_V_DIM
    tab = lambda w: pl.BlockSpec((tm, w), lambda i, j: (j, 0))
    out = lambda w: pl.BlockSpec((1, tm, w), lambda i, j: (i, j, 0))
    return pl.pallas_call(
        _mla_proj_body,
        out_shape=(jax.ShapeDtypeStruct((b, s, qk_width), BF16), jax.ShapeDtypeStruct((b, s, qk_width), BF16),
                   jax.ShapeDtypeStruct((b, s, v_width), BF16)),
        grid=(b, s // tm),
        in_specs=[
            out(d),
            _const_spec((1, d)),
            _const_spec(w_down_x.shape),
            _const_spec((1, MLA_Q_RANK)),
            _const_spec((1, MLA_KV_RANK)),
            _const_spec(wq_a.shape),
            _const_spec(wq_b.shape),
            _const_spec(wk.shape),
            _const_spec(wv.shape),
            tab(MLA_QK_PAD), tab(MLA_QK_PAD), tab(LANES), tab(LANES),
        ],
        out_specs=(out(qk_width), out(qk_width), out(v_width)),
        compiler_params=_cparams("parallel", "parallel"),
        name="mla_proj",
    )(x, gmix[None, :], w_down_x, q_norm[None, :], kv_norm[None, :], wq_a, wq_b, wk, wv, cq_tab, sq_tab, ck_tab, sk_tab)


def _flash_body(q_ref, k_ref, v_ref, o_ref):
    seq = q_ref.shape[1]
    blk, sub = ATT_BLOCK, ATT_SUB_BLOCK
    for i in range(seq // blk):
        q = q_ref[0, i * blk:(i + 1) * blk, :]
        m = jnp.full((blk, 1), -jnp.inf, F32)
        l = jnp.zeros((blk, 1), F32)
        acc = jnp.zeros((blk, v_ref.shape[2]), F32)
        end = (i + 1) * blk
        for start in range(0, end, sub):
            w = min(sub, end - start)
            s = _dot_nt(q, k_ref[0, start:start + w, :])
            if start + w > i * blk:
                row = lax.broadcasted_iota(jnp.int32, s.shape, 0) + i * blk
                col = lax.broadcasted_iota(jnp.int32, s.shape, 1) + start
                s = jnp.where(row >= col, s, -jnp.inf)
            m_new = jnp.maximum(m, jnp.max(s, axis=1, keepdims=True))
            p = jnp.exp(s - m_new)
            alpha = jnp.exp(m - m_new)
            l = alpha * l + jnp.sum(p, axis=1, keepdims=True)
            acc = alpha * acc + _dot(p.astype(BF16), v_ref[0, start:start + w, :])
            m = m_new
        o_ref[0, i * blk:(i + 1) * blk, :] = (acc / l).astype(BF16)


def _flash(q, k, v):
    b, s, _ = q.shape
    return pl.pallas_call(
        _flash_body,
        out_shape=jax.ShapeDtypeStruct((b, s, MLA_HEADS * MLA_V_DIM), BF16),
        grid=(b, MLA_HEADS),
        in_specs=[
            pl.BlockSpec((1, s, MLA_QK_PAD), lambda bi, h: (bi, 0, h)),
            pl.BlockSpec((1, s, MLA_QK_PAD), lambda bi, h: (bi, 0, h)),
            pl.BlockSpec((1, s, MLA_V_DIM), lambda bi, h: (bi, 0, h)),
        ],
        out_specs=pl.BlockSpec((1, s, MLA_V_DIM), lambda bi, h: (bi, 0, h)),
        compiler_params=_cparams("parallel", "parallel"),
        name="mla_flash",
    )(q, k, v)


def _out_proj_body(x_ref, o_ref, w_ref, y_ref):
    y_ref[...] = x_ref[...] + _dot(o_ref[...], w_ref[...])


def _out_proj(x2, o2, w_out):
    t, d = x2.shape
    tm = MLA_TOKENS
    return pl.pallas_call(
        _out_proj_body,
        out_shape=jax.ShapeDtypeStruct((t, d), F32),
        grid=(t // tm,),
        in_specs=[pl.BlockSpec((tm, d), lambda i: (i, 0)), pl.BlockSpec((tm, o2.shape[1]), lambda i: (i, 0)),
                  _const_spec(w_out.shape)],
        out_specs=pl.BlockSpec((tm, d), lambda i: (i, 0)),
        compiler_params=_cparams("parallel"),
        name="mla_out",
    )(x2, o2, w_out.astype(BF16))


def _sort_network(n):
    def merge(lo, hi, r):
        step = 2 * r
        if step < hi - lo:
            yield from merge(lo, hi, step)
            yield from merge(lo + r, hi, step)
            yield from ((i, i + r) for i in range(lo + r, hi - r, step))
        else:
            yield (lo, lo + r)

    def sort(lo, hi):
        if hi - lo >= 1:
            mid = lo + (hi - lo) // 2
            yield from sort(lo, mid)
            yield from sort(mid + 1, hi)
            yield from merge(lo, hi, 1)

    return tuple(sort(0, n - 1))


_SORT_TOPK = _sort_network(PEER_TOPK)


def _top_sorted(slabs):
    groups = []
    for g0 in range(0, len(slabs), PEER_TOPK):
        v = list(slabs[g0:g0 + PEER_TOPK])
        for i, j in _SORT_TOPK:
            v[i], v[j] = jnp.maximum(v[i], v[j]), jnp.minimum(v[i], v[j])
        groups.append(v)
    while len(groups) > 1:
        merged = []
        for a, b in zip(groups[0::2], groups[1::2]):
            v = [jnp.maximum(a[i], b[PEER_TOPK - 1 - i]) for i in range(PEER_TOPK)]
            d = PEER_TOPK // 2
            while d >= 1:
                for i in range(PEER_TOPK):
                    if not i & d:
                        v[i], v[i + d] = jnp.maximum(v[i], v[i + d]), jnp.minimum(v[i], v[i + d])
                d //= 2
            merged.append(v)
        groups = merged
    return groups[0]


def _merge_counts(s1, s2):
    n = [jnp.zeros_like(s1[0]) for _ in range(PEER_TOPK)]
    front = [s1[a] + s2[0] for a in range(PEER_TOPK)]
    for _ in range(PEER_TOPK):
        m = functools.reduce(jnp.maximum, front)
        first = jnp.full_like(m, float(PEER_TOPK))
        for a in reversed(range(PEER_TOPK)):
            first = jnp.where(front[a] == m, float(a), first)
        hits = [first == float(a) for a in range(PEER_TOPK)]
        n_hit = jnp.zeros_like(m)
        for a in range(PEER_TOPK):
            n_hit = jnp.where(hits[a], n[a], n_hit)
        nxt = jnp.full_like(m, -jnp.inf)
        for b in range(1, PEER_TOPK):
            nxt = jnp.where(n_hit == float(b - 1), s2[b], nxt)
        for a in range(PEER_TOPK):
            n[a] = jnp.where(hits[a], n[a] + 1.0, n[a])
            front[a] = jnp.where(hits[a], s1[a] + nxt, front[a])
    return n


_RANK_CHECKSUM = float(sum(range(PEER_TOPK)) + (PEER_N_KEYS - PEER_TOPK) * PEER_TOPK)


def _sel_tile(lt, s_ref, r_ref, t_ref, e1_ref, nk_ref, e2_ref, rb_ref):
    keys = range(PEER_N_KEYS)
    top = [_top_sorted([s_ref[p, lt, k] for k in keys]) for p in range(2)]
    n = _merge_counts(top[0], top[1])

    e1s = [jnp.exp(v - top[0][0]) for v in top[0]]
    e2s = [jnp.exp(v - top[1][0]) for v in top[1]]
    z = jnp.zeros_like(e1s[0])
    for b in range(PEER_TOPK):
        row = jnp.zeros_like(z)
        for a in range(PEER_TOPK):
            row = row + jnp.where(n[a] > float(b), e1s[a], 0.0)
        z = z + row * e2s[b]
    inv_z = 1.0 / z

    def rank(s, sorted_vals):
        r = jnp.full_like(s, float(PEER_TOPK))
        for a in reversed(range(PEER_TOPK)):
            r = jnp.where(s >= sorted_vals[a], float(a), r)
        return r

    def shifted(r, seen):
        seen = list(seen)
        back = jnp.zeros_like(r)
        for a in range(PEER_TOPK):
            hit = r == float(a)
            back = jnp.where(hit, seen[a], back)
            seen[a] = jnp.where(hit, seen[a] + 1.0, seen[a])
        return jnp.minimum(r + back, float(PEER_TOPK)), tuple(seen)

    none_seen = tuple(jnp.zeros_like(z) for _ in range(PEER_TOPK))

    def any_lane(mask):
        return jnp.max(jnp.where(mask, 1.0, 0.0)) > 0.0

    total = jnp.zeros_like(z)
    for k in keys:
        r = rank(s_ref[1, lt, k], top[1])
        r_ref[k] = r
        total = total + r
        t_ref[k] = jnp.exp(s_ref[1, lt, k] - top[1][0])

    @pl.when(any_lane(total != _RANK_CHECKSUM))
    def _():
        def fix(k, seen):
            r_ref[k], seen = shifted(r_ref[k], seen)
            return seen

        lax.fori_loop(0, PEER_N_KEYS, fix, none_seen)

    count = jnp.zeros_like(z)
    for k in keys:
        s = s_ref[0, lt, k]
        nk = jnp.zeros_like(s)
        for a in reversed(range(PEER_TOPK)):
            nk = jnp.where(s >= top[0][a], n[a], nk)
        nk_ref[lt, k] = nk
        e1_ref[lt, k] = jnp.exp(s - top[0][0]) * inv_z
        count = count + jnp.where(s >= top[0][PEER_TOPK - 1], 1.0, 0.0)
    tied = count != float(PEER_TOPK)
    for a in range(PEER_TOPK - 1):
        tied = tied | (top[0][a] == top[0][a + 1])

    @pl.when(any_lane(tied))
    def _():
        def fix(k, seen):
            r, seen = shifted(rank(s_ref[0, lt, k], top[0]), seen)
            nk = jnp.zeros_like(r)
            for a in range(PEER_TOPK):
                nk = jnp.where(r == float(a), n[a], nk)
            nk_ref[lt, k] = nk
            return seen

        lax.fori_loop(0, PEER_N_KEYS, fix, none_seen)

    e2p = pltpu.bitcast(jnp.swapaxes(t_ref[...], 0, 1).astype(BF16), PAIRS)
    rbp = pltpu.bitcast(jnp.swapaxes(r_ref[...], 0, 1).astype(BF16), PAIRS)
    for h in range(PEER_HEADS):
        e2_ref[h, lt] = e2p[h]
        rb_ref[h, lt] = rbp[h]


def _fold_keys_body(keys_ref, wq_ref, o_ref):
    o_ref[0, 0] = jnp.dot(keys_ref[0], wq_ref[0, 0], preferred_element_type=F32, precision=lax.Precision.HIGHEST)


def _fold_keys(sub_keys, w_q):
    d = w_q.shape[0]
    wq_t = w_q.reshape(d, PEER_HEADS, 2, PEER_HALF).transpose(2, 1, 3, 0)
    return pl.pallas_call(
        _fold_keys_body,
        out_shape=jax.ShapeDtypeStruct((2, PEER_HEADS, PEER_N_KEYS, d), F32),
        grid=(2, PEER_HEADS),
        in_specs=[pl.BlockSpec((1, PEER_N_KEYS, PEER_HALF), lambda p, h: (p, 0, 0)),
                  pl.BlockSpec((1, 1, PEER_HALF, d), lambda p, h: (p, h, 0, 0))],
        out_specs=pl.BlockSpec((1, 1, PEER_N_KEYS, d), lambda p, h: (p, h, 0, 0)),
        compiler_params=_cparams("parallel", "parallel"),
        name="fold_keys",
    )(sub_keys, wq_t)


def _peer_sel_body(x_ref, g_ref, ws_ref, *rest):
    _peer_sel_core(x_ref[...], g_ref, ws_ref, *rest)


def _peer_sel_proj_body(x_ref, o_ref, wo_ref, g_ref, ws_ref, xn_ref, *rest):
    x = x_ref[...] + _dot(o_ref[...], wo_ref[...])
    xn_ref[...] = x
    _peer_sel_core(x, g_ref, ws_ref, *rest)


def _peer_sel_core(x, g_ref, ws_ref, hb_ref, e1_ref, nk_ref, e2_ref, rb_ref, s_ref, r_ref, t_ref):
    hb = _rms(x, g_ref[...]).astype(BF16)
    hb_ref[...] = pltpu.bitcast(hb, PAIRS)
    tiles = x.shape[0] // LANES
    for p in range(2):
        sc = _dot_nt(ws_ref[p], hb)
        sc = sc.reshape(PEER_N_KEYS, PEER_HEADS, tiles * LANES)
        for lt in range(tiles):
            s_ref[p, lt] = sc[:, :, lt * LANES:(lt + 1) * LANES]

    def tile(lt, _):
        _sel_tile(lt, s_ref, r_ref, t_ref, e1_ref, nk_ref, e2_ref, rb_ref)
        return 0

    lax.fori_loop(0, tiles, tile, 0)


def _peer_select(x2, gain, w_q, sub_keys, o2=None, w_out=None):
    t, d = x2.shape
    tb = SEL_TOKENS
    tiles = tb // LANES
    ws = _fold_keys(sub_keys, w_q).transpose(0, 2, 1, 3).reshape(2, PEER_N_KEYS * PEER_HEADS, d).astype(BF16)
    key_shape = jax.ShapeDtypeStruct((t // LANES, PEER_N_KEYS, PEER_HEADS, LANES), F32)
    key_spec = pl.BlockSpec((tiles, PEER_N_KEYS, PEER_HEADS, LANES), lambda i: (i, 0, 0, 0))
    pair_shape = jax.ShapeDtypeStruct((PEER_HEADS, t // LANES, PEER_N_KEYS // 2, LANES), PAIRS)
    pair_spec = pl.BlockSpec((PEER_HEADS, tiles, PEER_N_KEYS // 2, LANES), lambda i: (0, i, 0, 0))
    slab = (PEER_N_KEYS, PEER_HEADS, LANES)
    tok_spec = pl.BlockSpec((tb, d), lambda i: (i, 0))
    out_shape = (jax.ShapeDtypeStruct((t // 2, d), PAIRS), key_shape, key_shape, pair_shape, pair_shape)
    out_specs = (pl.BlockSpec((tb // 2, d), lambda i: (i, 0)), key_spec, key_spec, pair_spec, pair_spec)
    in_specs = [_const_spec((1, d)), _const_spec(ws.shape)]
    call = functools.partial(
        pl.pallas_call,
        grid=(t // tb,),
        scratch_shapes=[pltpu.VMEM((2, tiles) + slab, F32), pltpu.VMEM(slab, F32), pltpu.VMEM(slab, F32)],
        compiler_params=_cparams("parallel"),
        name="peer_select",
    )
    if o2 is None:
        return (x2,) + tuple(call(_peer_sel_body, out_shape=out_shape, in_specs=[tok_spec] + in_specs,
                                  out_specs=out_specs)(x2, gain[None, :], ws))
    o_spec = pl.BlockSpec((tb, o2.shape[1]), lambda i: (i, 0))
    return call(_peer_sel_proj_body, out_shape=(jax.ShapeDtypeStruct((t, d), F32),) + out_shape,
                in_specs=[tok_spec, o_spec, _const_spec(w_out.shape)] + in_specs,
                out_specs=(tok_spec,) + out_specs)(x2, o2, w_out.astype(BF16), gain[None, :], ws)


def _key_rows_bf16(row):
    packed = jnp.broadcast_to(row, (BF16_SUBLANES, row.shape[1])).astype(BF16)
    return jnp.concatenate([packed] * (PEER_N_KEYS // BF16_SUBLANES), axis=0)


def _peer_main_body(final_norm, x_ref, hb_ref, u_ref, vt_ref, e1_ref, n_ref, e2_ref, rb_ref, fg_ref, o_ref, acc_ref,
                    a_ref):
    s = pl.program_id(1)
    last = pl.num_programs(1) - 1
    piece_tiles = PEER_PIECE // LANES
    pieces = range(acc_ref.shape[0])

    def gates_and_values(p):
        g_tiles = []
        for tb in range(p * piece_tiles, (p + 1) * piece_tiles):
            g_keys = []
            for k0 in range(0, PEER_KEY_GROUP, PEER_KEY_REUSE):
                ks = range(k0, k0 + PEER_KEY_REUSE)
                w = {}
                for h in range(PEER_HEADS):
                    rb_t = pltpu.bitcast(rb_ref[h, tb], BF16)
                    e2_t = pltpu.bitcast(e2_ref[h, tb], BF16)
                    for k in ks:
                        n_b = _key_rows_bf16(n_ref[tb, k, h:h + 1, :])
                        e1_b = _key_rows_bf16(e1_ref[tb, k, h:h + 1, :])
                        term = jnp.where(rb_t < n_b, e2_t * e1_b, 0.0)
                        w[k] = term if h == 0 else w[k] + term
                for k in ks:
                    g_keys.append(w[k] * _gelu(a_ref[tb, k * PEER_N_KEYS:(k + 1) * PEER_N_KEYS, :].astype(BF16)))
            g_tiles.append(jnp.concatenate(g_keys, axis=0))
        acc_ref[p] += _dot(pltpu.bitcast(vt_ref[0], BF16), jnp.concatenate(g_tiles, axis=1))

    def activations(p):
        hb_piece = pltpu.bitcast(hb_ref[p * (PEER_PIECE // 2):(p + 1) * (PEER_PIECE // 2), :], BF16)
        a_piece = _dot_nt(pltpu.bitcast(u_ref[...], BF16), hb_piece)
        for t in range(piece_tiles):
            a_ref[p * piece_tiles + t] = a_piece[:, t * LANES:(t + 1) * LANES]

    @pl.when(s == 0)
    def _():
        acc_ref[...] = jnp.zeros_like(acc_ref)
        for p in pieces:
            activations(p)

    @pl.when((s > 0) & (s < last))
    def _():
        for p in pieces:
            gates_and_values(p)
            activations(p)

    @pl.when(s == last)
    def _():
        for p in pieces:
            gates_and_values(p)
        for p in pieces:
            rows = slice(p * PEER_PIECE, (p + 1) * PEER_PIECE)
            y = x_ref[rows, :] + acc_ref[p].T
            if final_norm:
                y = _rms(y, fg_ref[...])
            o_ref[rows, :] = y


def _pack_rows_body(x_ref, o_ref):
    o_ref[...] = pltpu.bitcast(x_ref[0].astype(BF16), PAIRS)


def _pack_rows(tables, layer):
    _, rows, d = tables.shape
    blk = min(rows, 1 << ((PACK_ELEMS // d).bit_length() - 1))
    return pl.pallas_call(
        _pack_rows_body,
        out_shape=jax.ShapeDtypeStruct((rows // 2, d), PAIRS),
        grid=(rows // blk,),
        in_specs=[pl.BlockSpec((1, blk, d), lambda i: (layer, i, 0))],
        out_specs=pl.BlockSpec((blk // 2, d), lambda i: (i, 0)),
        compiler_params=_cparams("parallel"),
        name="pack_rows",
    )(tables)


def _pack_transposed_body(x_ref, o_ref):
    o_ref[0] = pltpu.bitcast(x_ref[0].T.astype(BF16), PAIRS)


def _pack_transposed_chunks(tables, layer, chunk):
    _, rows, d = tables.shape
    return pl.pallas_call(
        _pack_transposed_body,
        out_shape=jax.ShapeDtypeStruct((rows // chunk, d // 2, chunk), PAIRS),
        grid=(rows // chunk,),
        in_specs=[pl.BlockSpec((1, chunk, d), lambda i: (layer, i, 0))],
        out_specs=pl.BlockSpec((1, d // 2, chunk), lambda i: (i, 0, 0)),
        compiler_params=_cparams("parallel"),
        name="pack_transposed",
    )(tables)


def _peer_experts(x2, hb, e1, n, e2, rb, u_tabs, v_tabs, layer, final_gain=None):
    t, d = x2.shape
    n_experts = u_tabs.shape[1]
    tb, eb = PEER_TOKENS, PEER_EXPERTS
    chunks = n_experts // eb
    final_norm = final_gain is not None
    fg = (final_gain if final_norm else jnp.ones((d,), F32))[None, :]
    tok_spec = pl.BlockSpec((tb, d), lambda i, s: (i, 0))
    hb_spec = pl.BlockSpec((tb // 2, d), lambda i, s: (i, 0))
    sel_spec = pl.BlockSpec((PEER_HEADS, tb // LANES, PEER_N_KEYS // 2, LANES), lambda i, s: (0, i, 0, 0))
    key_spec = pl.BlockSpec((tb // LANES, PEER_KEY_GROUP, PEER_HEADS, LANES),
                            lambda i, s: (i, jnp.clip(s - 1, 0, chunks - 1), 0, 0))
    return pl.pallas_call(
        functools.partial(_peer_main_body, final_norm),
        out_shape=jax.ShapeDtypeStruct((t, d), F32),
        grid=(t // tb, chunks + 1),
        in_specs=[
            tok_spec,
            hb_spec,
            pl.BlockSpec((eb // 2, d), lambda i, s: (jnp.minimum(s, chunks - 1), 0)),
            pl.BlockSpec((1, d // 2, eb), lambda i, s: (jnp.maximum(s - 1, 0), 0, 0)),
            key_spec, key_spec, sel_spec, sel_spec,
            _const_spec((1, d)),
        ],
        out_specs=pl.BlockSpec((tb, d), lambda i, s: (i, 0)),
        scratch_shapes=[pltpu.VMEM((tb // PEER_PIECE, d, PEER_PIECE), F32), pltpu.VMEM((tb // LANES, eb, LANES), F32)],
        compiler_params=_cparams("parallel", "arbitrary"),
        name="peer_experts",
    )(x2, hb, _pack_rows(u_tabs, layer), _pack_transposed_chunks(v_tabs, layer, eb), e1, n, e2, rb, fg)


def _peer(x2, gain, w_q, sub_keys, u_tabs, v_tabs, layer, final_gain=None, o2=None, w_out=None):
    x2, hb, e1, n, e2, rb = _peer_select(x2, gain, w_q, sub_keys, o2, w_out)
    return _peer_experts(x2, hb, e1, n, e2, rb, u_tabs, v_tabs, layer, final_gain)


def kernel(x, norm_mix, norm_ffn, hy_w_in, gm_v_norm, gm_w_s, gm_b_s, ret_norm, hy_w_out, mla_w_down, mla_q_norm,
           mla_w_q_up, mla_kv_norm, mla_w_kv_up, mla_w_out, peer_w_q, peer_sub_keys, peer_u, peer_v, final_norm):
    b, s, d = x.shape
    depth = norm_mix.shape[0]
    for layer in range(depth):
        j = layer // 2
        if layer % 2 == 0:
            x = _hy_mixer(x, norm_mix[layer], hy_w_in[j], gm_v_norm[j], gm_w_s[j], gm_b_s[j], ret_norm[j], hy_w_out[j])
            pending = {}
        else:
            q, k, v = _mla_proj(x, norm_mix[layer], mla_w_down[j], mla_q_norm[j], mla_w_q_up[j], mla_kv_norm[j],
                                mla_w_kv_up[j])
            pending = dict(o2=_flash(q, k, v).reshape(b * s, -1), w_out=mla_w_out[j])
        last = layer == depth - 1
        x2 = _peer(x.reshape(b * s, d), norm_ffn[layer], peer_w_q[layer], peer_sub_keys[layer], peer_u, peer_v, layer,
                   final_norm if last else None, **pending)
        x = x2.reshape(b, s, d)
    return x
```
